```python
import jax, jax.numpy as jnp
from jax import lax
import numpy as np

D_MODEL = 2048
BATCH = 8
SEQ = 4096
DEPTH = 4

N_A_LAYERS = DEPTH // 2
N_B_LAYERS = DEPTH - N_A_LAYERS
D_FF = 5632
GMLP_CHUNK = 128
GMLP_D_GATE = D_MODEL
GMLP_GROUP_WIDTH = 128
GMLP_GROUPS = GMLP_D_GATE // GMLP_GROUP_WIDTH
HEAD_DIM = 128
N_HEADS = D_MODEL // HEAD_DIM
DILATED_GROUPS = ((128, 1), (512, 4), (2048, 16))
N_GROUPS = len(DILATED_GROUPS)
ATTN_BLOCK = 128
REL_WINDOW = 128
EPS = 1e-6

kernel_name = "yoco_gmlp_dilated_alibi_macaron"


def rms_norm(x, g):
    xf = x.astype(jnp.float32)
    y = xf * lax.rsqrt(jnp.mean(xf * xf, axis=-1, keepdims=True) + EPS)
    return (y * g.astype(jnp.float32)).astype(x.dtype)


def swiglu(h, w_gate, w_up, w_down):
    return (jax.nn.silu(h @ w_gate) * (h @ w_up)) @ w_down


def gmlp_mixer(h, w_in, v_norm, w_s, b_s, w_out):
    bsz, seq, _ = h.shape
    z = jax.nn.gelu(h @ w_in)
    u, v = z[..., :GMLP_D_GATE], z[..., GMLP_D_GATE:]
    v = rms_norm(v, v_norm)
    v = v.reshape(bsz, seq // GMLP_CHUNK, GMLP_CHUNK, GMLP_GROUPS, GMLP_GROUP_WIDTH)
    causal = jnp.tril(jnp.ones((GMLP_CHUNK, GMLP_CHUNK), dtype=w_s.dtype))
    ws = w_s * causal[None]
    sv = jnp.einsum('gpq,bnqgc->bnpgc', ws, v) + b_s.T[None, None, :, :, None]
    return (u * sv.reshape(bsz, seq, GMLP_D_GATE)) @ w_out


def dilated_branch(q, k, v, dil, slopes):
    bsz, seq, nh, dh = q.shape
    L = seq // dil
    n = bsz * dil

    def to_sub(t):
        t = t.reshape(bsz, L, dil, nh, dh).transpose(0, 2, 1, 3, 4)
        return t.reshape(n, L, nh, dh)

    def from_sub(t):
        rest = t.shape[2:]
        t = t.reshape((bsz, dil, L) + rest)
        t = jnp.swapaxes(t, 1, 2)
        return t.reshape((bsz, seq) + rest)

    nb = -(-L // ATTN_BLOCK)
    Lp = nb * ATTN_BLOCK
    pad = Lp - L
    qs = jnp.pad(to_sub(q), ((0, 0), (0, pad), (0, 0), (0, 0))).reshape(n, nb, ATTN_BLOCK, nh, dh)

    def band(t):
        tp = jnp.pad(to_sub(t), ((0, 0), (ATTN_BLOCK, pad), (0, 0), (0, 0)))
        prev = tp[:, :Lp].reshape(n, nb, ATTN_BLOCK, nh, dh)
        cur = tp[:, ATTN_BLOCK:].reshape(n, nb, ATTN_BLOCK, nh, dh)
        return jnp.concatenate([prev, cur], axis=2)

    kb, vb = band(k), band(v)
    s = jnp.einsum('nbqhd,nbkhd->nbhqk', qs, kb, preferred_element_type=jnp.float32)
    qi = jnp.arange(ATTN_BLOCK)[:, None]
    kj = jnp.arange(2 * ATTN_BLOCK)[None, :]
    delta = qi + ATTN_BLOCK - kj
    j_abs = jnp.arange(nb)[:, None, None] * ATTN_BLOCK - ATTN_BLOCK + kj[None]
    valid = (delta >= 0)[None] & (delta <= REL_WINDOW)[None] & (j_abs >= 0)
    alibi = -slopes[:, None, None] * (delta * dil).astype(jnp.float32)[None]
    s = jnp.where(valid[None, :, None], s + alibi[None, None], -jnp.inf)
    m = jnp.max(s, axis=-1, keepdims=True)
    p = jnp.exp(s - m)
    l = jnp.sum(p, axis=-1, keepdims=True)
    o = jnp.einsum('nbhqk,nbkhd->nbqhd', p / l, vb.astype(jnp.float32))
    lse = (m + jnp.log(l))[..., 0]
    lse = lse.transpose(0, 1, 3, 2).reshape(n, Lp, nh)[:, :L]
    o = o.reshape(n, Lp, nh, dh)[:, :L]
    return from_sub(o), from_sub(lse)


def dilated_mixer(h, k_sh, v_sh, w_q, q_norm, w_o, slopes):
    bsz, seq, _ = h.shape
    q = (h @ w_q).reshape(bsz, seq, N_GROUPS, N_HEADS, HEAD_DIM)
    q = rms_norm(q, q_norm[:, None, :]) * (HEAD_DIM ** -0.5)
    outs, lses = [], []
    for g, (_, dil) in enumerate(DILATED_GROUPS):
        o, lse = dilated_branch(q[:, :, g], k_sh[:, :, g], v_sh[:, :, g], dil, slopes)
        outs.append(o)
        lses.append(lse)
    wts = jax.nn.softmax(jnp.stack(lses, 0), axis=0)
    o = jnp.sum(wts[..., None] * jnp.stack(outs, 0), axis=0)
    return o.astype(h.dtype).reshape(bsz, seq, N_HEADS * HEAD_DIM) @ w_o


def _fwd_setup_inputs(seed: int = 0) -> dict:
    key = jax.random.key(seed)
    ks = iter(jax.random.split(key, 32))

    def nrm(shape, scale):
        return jax.random.normal(next(ks), shape, dtype=jnp.float32) * scale

    def gain(shape):
        return 1.0 + nrm(shape, 0.02)

    D, F = D_MODEL, D_FF
    qkv_w = N_GROUPS * N_HEADS * HEAD_DIM
    return {
        "x": nrm((BATCH, SEQ, D), 1.0),
        "ffn1_norm": gain((DEPTH, D)),
        "ffn1_w_gate": nrm((DEPTH, D, F), D ** -0.5),
        "ffn1_w_up": nrm((DEPTH, D, F), D ** -0.5),
        "ffn1_w_down": nrm((DEPTH, F, D), F ** -0.5),
        "mix_norm": gain((DEPTH, D)),
        "ffn2_norm": gain((DEPTH, D)),
        "ffn2_w_gate": nrm((DEPTH, D, F), D ** -0.5),
        "ffn2_w_up": nrm((DEPTH, D, F), D ** -0.5),
        "ffn2_w_down": nrm((DEPTH, F, D), F ** -0.5),
        "gmlp_w_in": nrm((N_A_LAYERS, D, 2 * GMLP_D_GATE), D ** -0.5),
        "gmlp_v_norm": gain((N_A_LAYERS, GMLP_D_GATE)),
        "gmlp_w_s": nrm((N_A_LAYERS, GMLP_GROUPS, GMLP_CHUNK, GMLP_CHUNK), GMLP_CHUNK ** -0.5),
        "gmlp_b_s": 1.0 + nrm((N_A_LAYERS, GMLP_GROUPS, GMLP_CHUNK), 0.02),
        "gmlp_w_out": nrm((N_A_LAYERS, GMLP_D_GATE, D), GMLP_D_GATE ** -0.5),
        "kv_norm": gain((D,)),
        "w_kv": nrm((D, 2 * qkv_w), D ** -0.5),
        "k_norm": gain((N_GROUPS, HEAD_DIM)),
        "attn_w_q": nrm((N_B_LAYERS, D, qkv_w), D ** -0.5),
        "attn_q_norm": gain((N_B_LAYERS, N_GROUPS, HEAD_DIM)),
        "attn_w_o": nrm((N_B_LAYERS, N_HEADS * HEAD_DIM, D), (N_HEADS * HEAD_DIM) ** -0.5),
    }


def _fwd_reference(x, ffn1_norm, ffn1_w_gate, ffn1_w_up, ffn1_w_down, mix_norm,
              ffn2_norm, ffn2_w_gate, ffn2_w_up, ffn2_w_down,
              gmlp_w_in, gmlp_v_norm, gmlp_w_s, gmlp_b_s, gmlp_w_out,
              kv_norm, w_kv, k_norm, attn_w_q, attn_q_norm, attn_w_o):
    bsz, seq, _ = x.shape
    slopes = jnp.exp2(-8.0 * jnp.arange(1, N_HEADS + 1, dtype=jnp.float32) / N_HEADS)
    k_sh = v_sh = None
    for l in range(DEPTH):
        x = x + 0.5 * swiglu(rms_norm(x, ffn1_norm[l]), ffn1_w_gate[l], ffn1_w_up[l], ffn1_w_down[l])
        h = rms_norm(x, mix_norm[l])
        if l < N_A_LAYERS:
            x = x + gmlp_mixer(h, gmlp_w_in[l], gmlp_v_norm[l], gmlp_w_s[l], gmlp_b_s[l], gmlp_w_out[l])
        else:
            j = l - N_A_LAYERS
            x = x + dilated_mixer(h, k_sh, v_sh, attn_w_q[j], attn_q_norm[j], attn_w_o[j], slopes)
        x = x + 0.5 * swiglu(rms_norm(x, ffn2_norm[l]), ffn2_w_gate[l], ffn2_w_up[l], ffn2_w_down[l])
        if l == N_A_LAYERS - 1:
            kv = (rms_norm(x, kv_norm) @ w_kv).reshape(bsz, seq, 2, N_GROUPS, N_HEADS, HEAD_DIM)
            k_sh = rms_norm(kv[:, :, 0], k_norm[:, None, :])
            v_sh = kv[:, :, 1]
    return x


import jax as _jax
import jax.numpy as _jnp

TWIN_FORMAT = 'train_step'
FWD_PARAMS = ['x', 'ffn1_norm', 'ffn1_w_gate', 'ffn1_w_up', 'ffn1_w_down', 'mix_norm', 'ffn2_norm', 'ffn2_w_gate', 'ffn2_w_up', 'ffn2_w_down', 'gmlp_w_in', 'gmlp_v_norm', 'gmlp_w_s', 'gmlp_b_s', 'gmlp_w_out', 'kv_norm', 'w_kv', 'k_norm', 'attn_w_q', 'attn_q_norm', 'attn_w_o']
TWIN_WEIGHTS = ['ffn1_norm', 'ffn1_w_gate', 'ffn1_w_up', 'ffn1_w_down', 'mix_norm', 'ffn2_norm', 'ffn2_w_gate', 'ffn2_w_up', 'ffn2_w_down', 'gmlp_w_in', 'gmlp_v_norm', 'gmlp_w_s', 'gmlp_b_s', 'gmlp_w_out', 'kv_norm', 'w_kv', 'k_norm', 'attn_w_q', 'attn_q_norm', 'attn_w_o']
TWIN_DIFF_INPUT = 'x'
TWIN_INPUTS = ['x', 'ffn1_norm', 'ffn1_w_gate', 'ffn1_w_up', 'ffn1_w_down', 'mix_norm', 'ffn2_norm', 'ffn2_w_gate', 'ffn2_w_up', 'ffn2_w_down', 'gmlp_w_in', 'gmlp_v_norm', 'gmlp_w_s', 'gmlp_b_s', 'gmlp_w_out', 'kv_norm', 'w_kv', 'k_norm', 'attn_w_q', 'attn_q_norm', 'attn_w_o', 'loss_target', 'm_ffn1_norm', 'm_ffn1_w_gate', 'm_ffn1_w_up', 'm_ffn1_w_down', 'm_mix_norm', 'm_ffn2_norm', 'm_ffn2_w_gate', 'm_ffn2_w_up', 'm_ffn2_w_down', 'm_gmlp_w_in', 'm_gmlp_v_norm', 'm_gmlp_w_s', 'm_gmlp_b_s', 'm_gmlp_w_out', 'm_kv_norm', 'm_w_kv', 'm_k_norm', 'm_attn_w_q', 'm_attn_q_norm', 'm_attn_w_o', 'v_ffn1_norm', 'v_ffn1_w_gate', 'v_ffn1_w_up', 'v_ffn1_w_down', 'v_mix_norm', 'v_ffn2_norm', 'v_ffn2_w_gate', 'v_ffn2_w_up', 'v_ffn2_w_down', 'v_gmlp_w_in', 'v_gmlp_v_norm', 'v_gmlp_w_s', 'v_gmlp_b_s', 'v_gmlp_w_out', 'v_kv_norm', 'v_w_kv', 'v_k_norm', 'v_attn_w_q', 'v_attn_q_norm', 'v_attn_w_o']
TWIN_OUTPUTS = ['loss', 'grad_x', 'grad_ffn1_norm', 'grad_ffn1_w_gate', 'grad_ffn1_w_up', 'grad_ffn1_w_down', 'grad_mix_norm', 'grad_ffn2_norm', 'grad_ffn2_w_gate', 'grad_ffn2_w_up', 'grad_ffn2_w_down', 'grad_gmlp_w_in', 'grad_gmlp_v_norm', 'grad_gmlp_w_s', 'grad_gmlp_b_s', 'grad_gmlp_w_out', 'grad_kv_norm', 'grad_w_kv', 'grad_k_norm', 'grad_attn_w_q', 'grad_attn_q_norm', 'grad_attn_w_o', 'delta_ffn1_norm', 'delta_ffn1_w_gate', 'delta_ffn1_w_up', 'delta_ffn1_w_down', 'delta_mix_norm', 'delta_ffn2_norm', 'delta_ffn2_w_gate', 'delta_ffn2_w_up', 'delta_ffn2_w_down', 'delta_gmlp_w_in', 'delta_gmlp_v_norm', 'delta_gmlp_w_s', 'delta_gmlp_b_s', 'delta_gmlp_w_out', 'delta_kv_norm', 'delta_w_kv', 'delta_k_norm', 'delta_attn_w_q', 'delta_attn_q_norm', 'delta_attn_w_o', 'new_m_ffn1_norm', 'new_m_ffn1_w_gate', 'new_m_ffn1_w_up', 'new_m_ffn1_w_down', 'new_m_mix_norm', 'new_m_ffn2_norm', 'new_m_ffn2_w_gate', 'new_m_ffn2_w_up', 'new_m_ffn2_w_down', 'new_m_gmlp_w_in', 'new_m_gmlp_v_norm', 'new_m_gmlp_w_s', 'new_m_gmlp_b_s', 'new_m_gmlp_w_out', 'new_m_kv_norm', 'new_m_w_kv', 'new_m_k_norm', 'new_m_attn_w_q', 'new_m_attn_q_norm', 'new_m_attn_w_o', 'new_v_ffn1_norm', 'new_v_ffn1_w_gate', 'new_v_ffn1_w_up', 'new_v_ffn1_w_down', 'new_v_mix_norm', 'new_v_ffn2_norm', 'new_v_ffn2_w_gate', 'new_v_ffn2_w_up', 'new_v_ffn2_w_down', 'new_v_gmlp_w_in', 'new_v_gmlp_v_norm', 'new_v_gmlp_w_s', 'new_v_gmlp_b_s', 'new_v_gmlp_w_out', 'new_v_kv_norm', 'new_v_w_kv', 'new_v_k_norm', 'new_v_attn_w_q', 'new_v_attn_q_norm', 'new_v_attn_w_o']
TWIN_LEAF_KINDS = {'loss': 'loss', 'grad_x': 'grad_x', 'grad_ffn1_norm': 'grad_w', 'grad_ffn1_w_gate': 'grad_w', 'grad_ffn1_w_up': 'grad_w', 'grad_ffn1_w_down': 'grad_w', 'grad_mix_norm': 'grad_w', 'grad_ffn2_norm': 'grad_w', 'grad_ffn2_w_gate': 'grad_w', 'grad_ffn2_w_up': 'grad_w', 'grad_ffn2_w_down': 'grad_w', 'grad_gmlp_w_in': 'grad_w', 'grad_gmlp_v_norm': 'grad_w', 'grad_gmlp_w_s': 'grad_w', 'grad_gmlp_b_s': 'grad_w', 'grad_gmlp_w_out': 'grad_w', 'grad_kv_norm': 'grad_w', 'grad_w_kv': 'grad_w', 'grad_k_norm': 'grad_w', 'grad_attn_w_q': 'grad_w', 'grad_attn_q_norm': 'grad_w', 'grad_attn_w_o': 'grad_w', 'delta_ffn1_norm': 'delta_w', 'delta_ffn1_w_gate': 'delta_w', 'delta_ffn1_w_up': 'delta_w', 'delta_ffn1_w_down': 'delta_w', 'delta_mix_norm': 'delta_w', 'delta_ffn2_norm': 'delta_w', 'delta_ffn2_w_gate': 'delta_w', 'delta_ffn2_w_up': 'delta_w', 'delta_ffn2_w_down': 'delta_w', 'delta_gmlp_w_in': 'delta_w', 'delta_gmlp_v_norm': 'delta_w', 'delta_gmlp_w_s': 'delta_w', 'delta_gmlp_b_s': 'delta_w', 'delta_gmlp_w_out': 'delta_w', 'delta_kv_norm': 'delta_w', 'delta_w_kv': 'delta_w', 'delta_k_norm': 'delta_w', 'delta_attn_w_q': 'delta_w', 'delta_attn_q_norm': 'delta_w', 'delta_attn_w_o': 'delta_w', 'new_m_ffn1_norm': 'new_m', 'new_m_ffn1_w_gate': 'new_m', 'new_m_ffn1_w_up': 'new_m', 'new_m_ffn1_w_down': 'new_m', 'new_m_mix_norm': 'new_m', 'new_m_ffn2_norm': 'new_m', 'new_m_ffn2_w_gate': 'new_m', 'new_m_ffn2_w_up': 'new_m', 'new_m_ffn2_w_down': 'new_m', 'new_m_gmlp_w_in': 'new_m', 'new_m_gmlp_v_norm': 'new_m', 'new_m_gmlp_w_s': 'new_m', 'new_m_gmlp_b_s': 'new_m', 'new_m_gmlp_w_out': 'new_m', 'new_m_kv_norm': 'new_m', 'new_m_w_kv': 'new_m', 'new_m_k_norm': 'new_m', 'new_m_attn_w_q': 'new_m', 'new_m_attn_q_norm': 'new_m', 'new_m_attn_w_o': 'new_m', 'new_v_ffn1_norm': 'new_v', 'new_v_ffn1_w_gate': 'new_v', 'new_v_ffn1_w_up': 'new_v', 'new_v_ffn1_w_down': 'new_v', 'new_v_mix_norm': 'new_v', 'new_v_ffn2_norm': 'new_v', 'new_v_ffn2_w_gate': 'new_v', 'new_v_ffn2_w_up': 'new_v', 'new_v_ffn2_w_down': 'new_v', 'new_v_gmlp_w_in': 'new_v', 'new_v_gmlp_v_norm': 'new_v', 'new_v_gmlp_w_s': 'new_v', 'new_v_gmlp_b_s': 'new_v', 'new_v_gmlp_w_out': 'new_v', 'new_v_kv_norm': 'new_v', 'new_v_w_kv': 'new_v', 'new_v_k_norm': 'new_v', 'new_v_attn_w_q': 'new_v', 'new_v_attn_q_norm': 'new_v', 'new_v_attn_w_o': 'new_v'}


def _forward(args):
    return _fwd_reference(*[args[k] for k in FWD_PARAMS])


def _output_shape():
    def fwd():
        inp = _fwd_setup_inputs(0)
        return _fwd_reference(*[inp[k] for k in FWD_PARAMS])
    out = _jax.eval_shape(fwd)
    return out.shape, out.dtype

N_MICROBATCH = 1
ADAM_LR = 0.001
ADAM_B1 = 0.9
ADAM_B2 = 0.999
ADAM_EPS = 1e-08
ADAM_WD = 0.01
ADAM_STEP = 10
PER_EXAMPLE_BATCH_AXIS = {'x': 0, 'loss_target': 0}
SHARED_INPUTS = []
_WEIGHT_DTYPES = {'ffn1_norm': _jnp.float32, 'ffn1_w_gate': _jnp.float32, 'ffn1_w_up': _jnp.float32, 'ffn1_w_down': _jnp.float32, 'mix_norm': _jnp.float32, 'ffn2_norm': _jnp.float32, 'ffn2_w_gate': _jnp.float32, 'ffn2_w_up': _jnp.float32, 'ffn2_w_down': _jnp.float32, 'gmlp_w_in': _jnp.float32, 'gmlp_v_norm': _jnp.float32, 'gmlp_w_s': _jnp.float32, 'gmlp_b_s': _jnp.float32, 'gmlp_w_out': _jnp.float32, 'kv_norm': _jnp.float32, 'w_kv': _jnp.float32, 'k_norm': _jnp.float32, 'attn_w_q': _jnp.float32, 'attn_q_norm': _jnp.float32, 'attn_w_o': _jnp.float32}
MOMENT_SCALE = {'ffn1_norm': 3.073086e+00, 'ffn1_w_gate': 1.178557e-01, 'ffn1_w_up': 1.462760e-01, 'ffn1_w_down': 2.370721e-01, 'mix_norm': 8.675513e+00, 'ffn2_norm': 3.075444e+00, 'ffn2_w_gate': 1.388225e-01, 'ffn2_w_up': 1.867891e-01, 'ffn2_w_down': 3.018918e-01, 'gmlp_w_in': 1.080591e+00, 'gmlp_v_norm': 3.525204e+00, 'gmlp_w_s': 2.313554e+00, 'gmlp_b_s': 7.395994e+00, 'gmlp_w_out': 3.770046e+00, 'kv_norm': 4.726298e+00, 'w_kv': 9.371758e-01, 'k_norm': 7.187514e+00, 'attn_w_q': 6.876020e-02, 'attn_q_norm': 3.599154e+00, 'attn_w_o': 1.423081e+00}


def _to_microbatches(a, axis):
    t = _jnp.moveaxis(a, axis, 0)
    t = t.reshape((N_MICROBATCH, t.shape[0] // N_MICROBATCH) + t.shape[1:])
    return _jnp.moveaxis(t, 1, axis + 1)


def setup_inputs(seed: int = 0) -> dict:
    inp = _fwd_setup_inputs(seed)
    key = _jax.random.fold_in(_jax.random.key(seed), 7919)
    shape, _ = _output_shape()
    out = dict(inp)
    out["loss_target"] = _jax.random.normal(_jax.random.fold_in(key, 0), shape, _jnp.float32)
    for i, name in enumerate(TWIN_WEIGHTS):
        w = inp[name].astype(_jnp.float32)
        if MOMENT_SCALE is None:
            s = _jnp.sqrt(_jnp.mean(_jnp.square(w)) + 1e-30)
        else:
            s = MOMENT_SCALE[name]
        km, kv = _jax.random.split(_jax.random.fold_in(key, i + 1))
        out[name] = w
        out["m_" + name] = s * _jax.random.normal(km, w.shape, _jnp.float32)
        out["v_" + name] = (s * s) * _jax.random.uniform(kv, w.shape, _jnp.float32, 0.5, 1.5)
    if N_MICROBATCH > 1:
        for name, axis in PER_EXAMPLE_BATCH_AXIS.items():
            out[name] = _to_microbatches(out[name], axis)
    return {'x': out['x'], 'ffn1_norm': out['ffn1_norm'], 'ffn1_w_gate': out['ffn1_w_gate'], 'ffn1_w_up': out['ffn1_w_up'], 'ffn1_w_down': out['ffn1_w_down'], 'mix_norm': out['mix_norm'], 'ffn2_norm': out['ffn2_norm'], 'ffn2_w_gate': out['ffn2_w_gate'], 'ffn2_w_up': out['ffn2_w_up'], 'ffn2_w_down': out['ffn2_w_down'], 'gmlp_w_in': out['gmlp_w_in'], 'gmlp_v_norm': out['gmlp_v_norm'], 'gmlp_w_s': out['gmlp_w_s'], 'gmlp_b_s': out['gmlp_b_s'], 'gmlp_w_out': out['gmlp_w_out'], 'kv_norm': out['kv_norm'], 'w_kv': out['w_kv'], 'k_norm': out['k_norm'], 'attn_w_q': out['attn_w_q'], 'attn_q_norm': out['attn_q_norm'], 'attn_w_o': out['attn_w_o'], 'loss_target': out['loss_target'], 'm_ffn1_norm': out['m_ffn1_norm'], 'm_ffn1_w_gate': out['m_ffn1_w_gate'], 'm_ffn1_w_up': out['m_ffn1_w_up'], 'm_ffn1_w_down': out['m_ffn1_w_down'], 'm_mix_norm': out['m_mix_norm'], 'm_ffn2_norm': out['m_ffn2_norm'], 'm_ffn2_w_gate': out['m_ffn2_w_gate'], 'm_ffn2_w_up': out['m_ffn2_w_up'], 'm_ffn2_w_down': out['m_ffn2_w_down'], 'm_gmlp_w_in': out['m_gmlp_w_in'], 'm_gmlp_v_norm': out['m_gmlp_v_norm'], 'm_gmlp_w_s': out['m_gmlp_w_s'], 'm_gmlp_b_s': out['m_gmlp_b_s'], 'm_gmlp_w_out': out['m_gmlp_w_out'], 'm_kv_norm': out['m_kv_norm'], 'm_w_kv': out['m_w_kv'], 'm_k_norm': out['m_k_norm'], 'm_attn_w_q': out['m_attn_w_q'], 'm_attn_q_norm': out['m_attn_q_norm'], 'm_attn_w_o': out['m_attn_w_o'], 'v_ffn1_norm': out['v_ffn1_norm'], 'v_ffn1_w_gate': out['v_ffn1_w_gate'], 'v_ffn1_w_up': out['v_ffn1_w_up'], 'v_ffn1_w_down': out['v_ffn1_w_down'], 'v_mix_norm': out['v_mix_norm'], 'v_ffn2_norm': out['v_ffn2_norm'], 'v_ffn2_w_gate': out['v_ffn2_w_gate'], 'v_ffn2_w_up': out['v_ffn2_w_up'], 'v_ffn2_w_down': out['v_ffn2_w_down'], 'v_gmlp_w_in': out['v_gmlp_w_in'], 'v_gmlp_v_norm': out['v_gmlp_v_norm'], 'v_gmlp_w_s': out['v_gmlp_w_s'], 'v_gmlp_b_s': out['v_gmlp_b_s'], 'v_gmlp_w_out': out['v_gmlp_w_out'], 'v_kv_norm': out['v_kv_norm'], 'v_w_kv': out['v_w_kv'], 'v_k_norm': out['v_k_norm'], 'v_attn_w_q': out['v_attn_w_q'], 'v_attn_q_norm': out['v_attn_q_norm'], 'v_attn_w_o': out['v_attn_w_o']}


def _loss(weights, diff, rest, loss_target):
    with _jax.named_scope("forward"):
        args = {**rest, TWIN_DIFF_INPUT: diff, **{k: w.astype(_WEIGHT_DTYPES[k]) for k, w in weights.items()}}
        y = _forward(args)
    with _jax.named_scope("loss_head"):
        err = _jnp.square(y.astype(_jnp.float32) - loss_target)
        return 0.5 * _jnp.sum(_jnp.mean(err, axis=-1)) if err.ndim else 0.5 * err


def _adamw(w, g, m, v):
    m = ADAM_B1 * m + (1.0 - ADAM_B1) * g
    v = ADAM_B2 * v + (1.0 - ADAM_B2) * _jnp.square(g)
    m_hat = m / (1.0 - ADAM_B1 ** ADAM_STEP)
    v_hat = v / (1.0 - ADAM_B2 ** ADAM_STEP)
    delta = -ADAM_LR * (m_hat / (_jnp.sqrt(v_hat) + ADAM_EPS) + ADAM_WD * w)
    return delta, m, v


def reference(x, ffn1_norm, ffn1_w_gate, ffn1_w_up, ffn1_w_down, mix_norm, ffn2_norm, ffn2_w_gate, ffn2_w_up, ffn2_w_down, gmlp_w_in, gmlp_v_norm, gmlp_w_s, gmlp_b_s, gmlp_w_out, kv_norm, w_kv, k_norm, attn_w_q, attn_q_norm, attn_w_o, loss_target, m_ffn1_norm, m_ffn1_w_gate, m_ffn1_w_up, m_ffn1_w_down, m_mix_norm, m_ffn2_norm, m_ffn2_w_gate, m_ffn2_w_up, m_ffn2_w_down, m_gmlp_w_in, m_gmlp_v_norm, m_gmlp_w_s, m_gmlp_b_s, m_gmlp_w_out, m_kv_norm, m_w_kv, m_k_norm, m_attn_w_q, m_attn_q_norm, m_attn_w_o, v_ffn1_norm, v_ffn1_w_gate, v_ffn1_w_up, v_ffn1_w_down, v_mix_norm, v_ffn2_norm, v_ffn2_w_gate, v_ffn2_w_up, v_ffn2_w_down, v_gmlp_w_in, v_gmlp_v_norm, v_gmlp_w_s, v_gmlp_b_s, v_gmlp_w_out, v_kv_norm, v_w_kv, v_k_norm, v_attn_w_q, v_attn_q_norm, v_attn_w_o):
    given = dict(x=x, ffn1_norm=ffn1_norm, ffn1_w_gate=ffn1_w_gate, ffn1_w_up=ffn1_w_up, ffn1_w_down=ffn1_w_down, mix_norm=mix_norm, ffn2_norm=ffn2_norm, ffn2_w_gate=ffn2_w_gate, ffn2_w_up=ffn2_w_up, ffn2_w_down=ffn2_w_down, gmlp_w_in=gmlp_w_in, gmlp_v_norm=gmlp_v_norm, gmlp_w_s=gmlp_w_s, gmlp_b_s=gmlp_b_s, gmlp_w_out=gmlp_w_out, kv_norm=kv_norm, w_kv=w_kv, k_norm=k_norm, attn_w_q=attn_w_q, attn_q_norm=attn_q_norm, attn_w_o=attn_w_o, loss_target=loss_target, m_ffn1_norm=m_ffn1_norm, m_ffn1_w_gate=m_ffn1_w_gate, m_ffn1_w_up=m_ffn1_w_up, m_ffn1_w_down=m_ffn1_w_down, m_mix_norm=m_mix_norm, m_ffn2_norm=m_ffn2_norm, m_ffn2_w_gate=m_ffn2_w_gate, m_ffn2_w_up=m_ffn2_w_up, m_ffn2_w_down=m_ffn2_w_down, m_gmlp_w_in=m_gmlp_w_in, m_gmlp_v_norm=m_gmlp_v_norm, m_gmlp_w_s=m_gmlp_w_s, m_gmlp_b_s=m_gmlp_b_s, m_gmlp_w_out=m_gmlp_w_out, m_kv_norm=m_kv_norm, m_w_kv=m_w_kv, m_k_norm=m_k_norm, m_attn_w_q=m_attn_w_q, m_attn_q_norm=m_attn_q_norm, m_attn_w_o=m_attn_w_o, v_ffn1_norm=v_ffn1_norm, v_ffn1_w_gate=v_ffn1_w_gate, v_ffn1_w_up=v_ffn1_w_up, v_ffn1_w_down=v_ffn1_w_down, v_mix_norm=v_mix_norm, v_ffn2_norm=v_ffn2_norm, v_ffn2_w_gate=v_ffn2_w_gate, v_ffn2_w_up=v_ffn2_w_up, v_ffn2_w_down=v_ffn2_w_down, v_gmlp_w_in=v_gmlp_w_in, v_gmlp_v_norm=v_gmlp_v_norm, v_gmlp_w_s=v_gmlp_w_s, v_gmlp_b_s=v_gmlp_b_s, v_gmlp_w_out=v_gmlp_w_out, v_kv_norm=v_kv_norm, v_w_kv=v_w_kv, v_k_norm=v_k_norm, v_attn_w_q=v_attn_w_q, v_attn_q_norm=v_attn_q_norm, v_attn_w_o=v_attn_w_o)
    weights = {n: given[n] for n in TWIN_WEIGHTS}
    shared = {n: given[n] for n in SHARED_INPUTS}
    per_example = {n: given[n] for n in ['x']}
    grad_fn = _jax.value_and_grad(_loss, argnums=(0, 1))

    def one_microbatch(ex, loss_target):
        ex = dict(ex)
        diff = ex.pop(TWIN_DIFF_INPUT)
        return grad_fn(weights, diff, {**shared, **ex}, loss_target)

    if N_MICROBATCH == 1:
        loss, (grad_w, grad_x) = one_microbatch(per_example, given["loss_target"])
    else:
        def body(carry, xs):
            loss_sum, grad_sum = carry
            l_k, (gw_k, gx_k) = one_microbatch(xs[0], xs[1])
            with _jax.named_scope("update"):
                return (loss_sum + l_k, _jax.tree.map(_jnp.add, grad_sum, gw_k)), gx_k

        init = (_jnp.zeros((), _jnp.float32), _jax.tree.map(_jnp.zeros_like, weights))
        (loss, grad_w), grad_x = _jax.lax.scan(body, init, (per_example, given["loss_target"]))
    with _jax.named_scope("update"):
        delta_w, new_m, new_v = {}, {}, {}
        for n in TWIN_WEIGHTS:
            delta_w[n], new_m[n], new_v[n] = _adamw(weights[n], grad_w[n], given["m_" + n], given["v_" + n])
    return (loss, grad_x, *[grad_w[n] for n in TWIN_WEIGHTS], *[delta_w[n] for n in TWIN_WEIGHTS],
            *[new_m[n] for n in TWIN_WEIGHTS], *[new_v[n] for n in TWIN_WEIGHTS])
```

```python
import functools
import math

import jax
import jax.numpy as jnp
from jax import lax
from jax.experimental import pallas as pl
from jax.experimental.pallas import tpu as pltpu

F32 = jnp.float32
BF = jnp.bfloat16
N_DEV = 8
N_CHIP = 4
LANE = 128
EPS = 1e-6
NEG = -1e30
DILATIONS = (1, 4, 16)
ADAM_LR, ADAM_B1, ADAM_B2, ADAM_EPS, ADAM_WD, ADAM_STEP = 0.001, 0.9, 0.999, 1e-08, 0.01, 10
GELU_C0, GELU_C1 = 0.7978845608028654, 0.044715
VMEM_MB = 2 ** 20

MESH_T = pl.DeviceIdType.MESH
ANY = pl.BlockSpec(memory_space=pl.ANY)
NT = (((1,), (1,)), ((), ()))
TN = (((0,), (0,)), ((), ()))


def _tile(n, target, mult):
    best = None
    for t in range(mult, min(n, target) + 1, mult):
        if n % t == 0:
            best = t
    if best is None:
        best = n
    return best


def _cp(sem, vmem_mb=48):
    return pltpu.CompilerParams(dimension_semantics=sem, vmem_limit_bytes=vmem_mb * VMEM_MB)


def _dot(a, b, dims=None):
    if dims is None:
        return jnp.dot(a, b, preferred_element_type=F32)
    return lax.dot_general(a, b, dims, preferred_element_type=F32)


def _rms_hat(xv):
    r = lax.rsqrt(jnp.mean(xv * xv, axis=-1, keepdims=True) + EPS)
    return xv * r, r


def _rms_bwd(dhn, xv, gain, dres):
    xhat, r = _rms_hat(xv)
    dxhat = dhn * gain
    dx = dres + r * (dxhat - xhat * jnp.mean(dxhat * xhat, axis=-1, keepdims=True))
    return dx, jnp.sum(dhn * xhat, axis=0, keepdims=True)


def _accum_rows(ref, row, first):
    val = jnp.broadcast_to(row, ref.shape)

    @pl.when(first)
    def _():
        ref[...] = val

    @pl.when(jnp.logical_not(first))
    def _():
        ref[...] += val


def _gelu(z):
    t = jnp.tanh(GELU_C0 * (z + GELU_C1 * z * z * z))
    return 0.5 * z * (1.0 + t), t


def _gelu_grad(z, t):
    return 0.5 * (1.0 + t) + 0.5 * z * (1.0 - t * t) * GELU_C0 * (1.0 + 3.0 * GELU_C1 * z * z)


def _me():
    return lax.axis_index("x"), lax.axis_index("y"), lax.axis_index("c")


def _allgather(xs, name):
    def body(x_ref, out_ref, send_sems, recv_sems, local_sem):
        x, y, c = _me()
        me, sibling = (x, y, c), (x, y, 1 - c)
        chips = [(1 - x, y), (x, 1 - y), (1 - x, 1 - y)]

        def slot(px, py, pc):
            return out_ref.at[4 * px + 2 * py + pc]

        def copy(k, block, to, src=None):
            return pltpu.make_async_remote_copy(
                src_ref=slot(*block) if src is None else src, dst_ref=slot(*block),
                send_sem=send_sems.at[k], recv_sem=recv_sems.at[k], device_id=to, device_id_type=MESH_T)

        mine = pltpu.make_async_copy(x_ref, slot(*me), local_sem)
        mine.start()
        first = [copy(0, me, sibling, src=x_ref)]
        first += [copy(1 + j, me, (*chip, c), src=x_ref) for j, chip in enumerate(chips)]
        for cp in first:
            cp.start()
        passed = [copy(4 + j, (*chip, c), sibling) for j, chip in enumerate(chips)]
        for j, chip in enumerate(chips):
            copy(1 + j, (*chip, c), me).wait_recv()
            passed[j].start()
        copy(0, sibling, me).wait_recv()
        for j, chip in enumerate(chips):
            copy(4 + j, (*chip, 1 - c), me).wait_recv()
        for cp in first + passed:
            cp.wait_send()
        mine.wait()

    return pl.pallas_call(
        body, name=name, out_shape=jax.ShapeDtypeStruct((N_DEV,) + xs.shape, xs.dtype),
        in_specs=[ANY], out_specs=ANY,
        scratch_shapes=[pltpu.SemaphoreType.DMA((7,)), pltpu.SemaphoreType.DMA((7,)), pltpu.SemaphoreType.DMA(())],
    )(xs)


def _core_exchange(p, name):
    def body(p_ref, r_ref, send_sems, recv_sems):
        x, y, c = _me()
        cps = []
        for k in range(N_CHIP):
            cp = pltpu.make_async_remote_copy(
                src_ref=p_ref.at[2 * k + (1 - c)], dst_ref=r_ref.at[k],
                send_sem=send_sems.at[k], recv_sem=recv_sems.at[k], device_id=(x, y, 1 - c), device_id_type=MESH_T)
            cp.start()
            cps.append(cp)
        for cp in cps:
            cp.wait()

    return pl.pallas_call(
        body, name=name, out_shape=jax.ShapeDtypeStruct((N_CHIP,) + p.shape[1:], p.dtype),
        in_specs=[ANY], out_specs=ANY,
        scratch_shapes=[pltpu.SemaphoreType.DMA((N_CHIP,)), pltpu.SemaphoreType.DMA((N_CHIP,))],
    )(p)


def _chip_exchange(q, name):
    def body(q_ref, r_ref, send_sems, recv_sems):
        x, y, c = _me()
        chips = [(1 - x, y), (x, 1 - y), (1 - x, 1 - y)]
        cps = []
        for t, (px, py) in enumerate(chips):
            cp = pltpu.make_async_remote_copy(
                src_ref=q_ref.at[2 * px + py], dst_ref=r_ref.at[t],
                send_sem=send_sems.at[t], recv_sem=recv_sems.at[t], device_id=(px, py, c), device_id_type=MESH_T)
            cp.start()
            cps.append(cp)
        for cp in cps:
            cp.wait()

    return pl.pallas_call(
        body, name=name, out_shape=jax.ShapeDtypeStruct((3,) + q.shape[1:], q.dtype),
        in_specs=[ANY], out_specs=ANY,
        scratch_shapes=[pltpu.SemaphoreType.DMA((3,)), pltpu.SemaphoreType.DMA((3,))],
    )(q)


def _pair_add(p, r1, core):
    _, rows, cols = p.shape
    tr = _tile(rows, 512, 16)
    p4 = p.reshape(N_CHIP, 2, rows, cols)

    def body(core_ref, p_ref, r_ref, q_ref):
        q_ref[...] = (p_ref[...].astype(F32) + r_ref[...].astype(F32)).astype(BF)

    grid_spec = pltpu.PrefetchScalarGridSpec(
        num_scalar_prefetch=1, grid=(N_CHIP, rows // tr),
        in_specs=[pl.BlockSpec((None, None, tr, cols), lambda k, i, cr: (k, cr[0], i, 0)),
                  pl.BlockSpec((None, tr, cols), lambda k, i, cr: (k, i, 0))],
        out_specs=pl.BlockSpec((None, tr, cols), lambda k, i, cr: (k, i, 0)))
    return pl.pallas_call(
        body, name="pair_add", grid_spec=grid_spec, out_shape=jax.ShapeDtypeStruct((N_CHIP, rows, cols), BF),
        compiler_params=_cp(("parallel", "parallel")),
    )(core, p4, r1)


def _adam_math(g, w, m, v):
    m2 = ADAM_B1 * m + (1.0 - ADAM_B1) * g
    v2 = ADAM_B2 * v + (1.0 - ADAM_B2) * (g * g)
    m_hat = m2 / (1.0 - ADAM_B1 ** ADAM_STEP)
    v_hat = v2 / (1.0 - ADAM_B2 ** ADAM_STEP)
    delta = -ADAM_LR * (m_hat / (jnp.sqrt(v_hat) + ADAM_EPS) + ADAM_WD * w)
    return delta, m2, v2


def _adamw_shard(q, r2, chip, w, m, v):
    rows, cols = w.shape
    tr = _tile(rows, 256, 16)

    def body(chip_ref, q_ref, r_ref, w_ref, m_ref, v_ref, g_ref, d_ref, m2_ref, v2_ref):
        g = q_ref[...].astype(F32) + r_ref[0].astype(F32) + r_ref[1].astype(F32) + r_ref[2].astype(F32)
        d, m2, v2 = _adam_math(g, w_ref[...], m_ref[...], v_ref[...])
        g_ref[...] = g
        d_ref[...] = d
        m2_ref[...] = m2
        v2_ref[...] = v2

    blk = pl.BlockSpec((tr, cols), lambda i, cr: (i, 0))
    grid_spec = pltpu.PrefetchScalarGridSpec(
        num_scalar_prefetch=1, grid=(rows // tr,),
        in_specs=[pl.BlockSpec((None, tr, cols), lambda i, cr: (cr[0], i, 0)),
                  pl.BlockSpec((3, tr, cols), lambda i, cr: (0, i, 0)), blk, blk, blk],
        out_specs=[blk, blk, blk, blk])
    out = jax.ShapeDtypeStruct((rows, cols), F32)
    return pl.pallas_call(
        body, name="adamw_shard", grid_spec=grid_spec, out_shape=[out, out, out, out],
        compiler_params=_cp(("parallel",)),
    )(chip, q, r2, w, m, v)


def _adamw_replicated(parts, w, m, v):
    rows, cols = w.shape
    tr = _tile(rows, 512, 8)

    def body(p_ref, w_ref, m_ref, v_ref, g_ref, d_ref, m2_ref, v2_ref):
        g = p_ref[0]
        for k in range(1, N_DEV):
            g = g + p_ref[k]
        d, m2, v2 = _adam_math(g, w_ref[...], m_ref[...], v_ref[...])
        g_ref[...] = g
        d_ref[...] = d
        m2_ref[...] = m2
        v2_ref[...] = v2

    blk = pl.BlockSpec((tr, cols), lambda i: (i, 0))
    out = jax.ShapeDtypeStruct((rows, cols), F32)
    return pl.pallas_call(
        body, name="adamw_replicated", grid=(rows // tr,),
        in_specs=[pl.BlockSpec((N_DEV, tr, cols), lambda i: (0, i, 0)), blk, blk, blk],
        out_specs=[blk, blk, blk, blk], out_shape=[out, out, out, out],
        compiler_params=_cp(("parallel",)),
    )(parts, w, m, v)


def _ffn_fwd(x, gain, wg, wu, wd):
    S, D = x.shape
    nsh, _, fs = wg.shape
    tm = _tile(S, 512, 16)

    def body(x_ref, g_ref, wg_ref, wu_ref, wd_ref, y_ref, hn_ref, a_ref, b_ref, acc_ref):
        j = pl.program_id(1)

        @pl.when(j == 0)
        def _():
            xhat, _ = _rms_hat(x_ref[...])
            hn_ref[...] = (xhat * g_ref[...]).astype(BF)
            acc_ref[...] = jnp.zeros_like(acc_ref)

        hn = hn_ref[...]
        a = _dot(hn, wg_ref[...])
        b = _dot(hn, wu_ref[...])
        a_ref[...] = a.astype(BF)
        b_ref[...] = b.astype(BF)
        act = (a * jax.nn.sigmoid(a) * b).astype(BF)
        acc_ref[...] += _dot(act, wd_ref[...])

        @pl.when(j == nsh - 1)
        def _():
            y_ref[...] = x_ref[...] + 0.5 * acc_ref[...]

    row = pl.BlockSpec((tm, D), lambda m, j: (m, 0))
    hid = pl.BlockSpec((None, tm, fs), lambda m, j: (j, m, 0))
    return pl.pallas_call(
        body, name="ffn_fwd", grid=(S // tm, nsh),
        in_specs=[row, pl.BlockSpec((1, D), lambda m, j: (0, 0)),
                  pl.BlockSpec((None, D, fs), lambda m, j: (j, 0, 0)),
                  pl.BlockSpec((None, D, fs), lambda m, j: (j, 0, 0)),
                  pl.BlockSpec((None, fs, D), lambda m, j: (j, 0, 0))],
        out_specs=[row, row, hid, hid],
        out_shape=[jax.ShapeDtypeStruct((S, D), F32), jax.ShapeDtypeStruct((S, D), BF),
                   jax.ShapeDtypeStruct((nsh, S, fs), BF), jax.ShapeDtypeStruct((nsh, S, fs), BF)],
        scratch_shapes=[pltpu.VMEM((tm, D), F32)],
        compiler_params=_cp(("parallel", "arbitrary"), 56),
    )(x, gain, wg, wu, wd)


def _ffn_bwd_dx(dy, x, gain, a, b, wg, wu, wd):
    S, D = x.shape
    nsh, _, fs = wg.shape
    tm = _tile(S, 256, 16)

    def body(dy_ref, x_ref, g_ref, a_ref, b_ref, wg_ref, wu_ref, wd_ref,
             dx_ref, da_ref, db_ref, act_ref, dyh_ref, dg_ref, acc_ref):
        m, j = pl.program_id(0), pl.program_id(1)

        @pl.when(j == 0)
        def _():
            dyh_ref[...] = (0.5 * dy_ref[...]).astype(BF)
            acc_ref[...] = jnp.zeros_like(acc_ref)

        dact = _dot(dyh_ref[...], wd_ref[...], NT)
        av = a_ref[...].astype(F32)
        bv = b_ref[...].astype(F32)
        sg = jax.nn.sigmoid(av)
        sil = av * sg
        da = (dact * bv * (sg * (1.0 + av * (1.0 - sg)))).astype(BF)
        db = (dact * sil).astype(BF)
        da_ref[...] = da
        db_ref[...] = db
        act_ref[...] = (sil * bv).astype(BF)
        acc_ref[...] += _dot(da, wg_ref[...], NT) + _dot(db, wu_ref[...], NT)

        @pl.when(j == nsh - 1)
        def _():
            dx, dgain = _rms_bwd(acc_ref[...], x_ref[...], g_ref[...], dy_ref[...])
            dx_ref[...] = dx
            _accum_rows(dg_ref, dgain, m == 0)

    row = pl.BlockSpec((tm, D), lambda m, j: (m, 0))
    hid = pl.BlockSpec((None, tm, fs), lambda m, j: (j, m, 0))
    hshape = jax.ShapeDtypeStruct((nsh, S, fs), BF)
    return pl.pallas_call(
        body, name="ffn_bwd_dx", grid=(S // tm, nsh),
        in_specs=[row, row, pl.BlockSpec((1, D), lambda m, j: (0, 0)), hid, hid,
                  pl.BlockSpec((None, D, fs), lambda m, j: (j, 0, 0)),
                  pl.BlockSpec((None, D, fs), lambda m, j: (j, 0, 0)),
                  pl.BlockSpec((None, fs, D), lambda m, j: (j, 0, 0))],
        out_specs=[row, hid, hid, hid, row, pl.BlockSpec((8, D), lambda m, j: (0, 0))],
        out_shape=[jax.ShapeDtypeStruct((S, D), F32), hshape, hshape, hshape,
                   jax.ShapeDtypeStruct((S, D), BF), jax.ShapeDtypeStruct((8, D), F32)],
        scratch_shapes=[pltpu.VMEM((tm, D), F32)],
        compiler_params=_cp(("arbitrary", "arbitrary"), 56),
    )(dy, x, gain, a, b, wg, wu, wd)


def _opspec(arr, kind, tm, grid_mj):
    if kind == "full":
        return pl.BlockSpec((tm, arr.shape[1]), lambda *g: (grid_mj(*g)[0], 0)), arr.shape[1]
    if kind == "cols":
        n = arr.shape[1] // N_DEV
        return pl.BlockSpec((tm, n), lambda *g: grid_mj(*g)), n
    n = arr.shape[2]
    return pl.BlockSpec((None, tm, n), lambda *g: (grid_mj(*g)[1], grid_mj(*g)[0], 0)), n


def _wgrad(a, a_kind, b, b_kind, name):
    S = a.shape[0] if a_kind != "stack" else a.shape[1]
    tm = _tile(S, 512, 16)
    mj = lambda j, m: (m, j)
    a_spec, ka = _opspec(a, a_kind, tm, mj)
    b_spec, nb = _opspec(b, b_kind, tm, mj)
    n_m = S // tm

    def body(a_ref, b_ref, o_ref, acc_ref):
        m = pl.program_id(1)

        @pl.when(m == 0)
        def _():
            acc_ref[...] = jnp.zeros_like(acc_ref)

        acc_ref[...] += _dot(a_ref[...].astype(BF), b_ref[...].astype(BF), TN)

        @pl.when(m == n_m - 1)
        def _():
            o_ref[...] = acc_ref[...].astype(BF)

    return pl.pallas_call(
        body, name=name, grid=(N_DEV, n_m), in_specs=[a_spec, b_spec],
        out_specs=pl.BlockSpec((None, ka, nb), lambda j, m: (j, 0, 0)),
        out_shape=jax.ShapeDtypeStruct((N_DEV, ka, nb), BF),
        scratch_shapes=[pltpu.VMEM((ka, nb), F32)],
        compiler_params=_cp(("parallel", "arbitrary"), 48),
    )(a, b)


def _normproj(x, gain, w, name):
    S, D = x.shape
    _, _, n = w.shape
    tm = _tile(S, 512, 16)

    def body(x_ref, g_ref, w_ref, hn_ref, y_ref):
        @pl.when(pl.program_id(1) == 0)
        def _():
            xhat, _ = _rms_hat(x_ref[...])
            hn_ref[...] = (xhat * g_ref[...]).astype(BF)

        y_ref[...] = _dot(hn_ref[...], w_ref[...])

    row = pl.BlockSpec((tm, D), lambda m, j: (m, 0))
    return pl.pallas_call(
        body, name=name, grid=(S // tm, N_DEV),
        in_specs=[row, pl.BlockSpec((1, D), lambda m, j: (0, 0)), pl.BlockSpec((None, D, n), lambda m, j: (j, 0, 0))],
        out_specs=[row, pl.BlockSpec((tm, n), lambda m, j: (m, j))],
        out_shape=[jax.ShapeDtypeStruct((S, D), BF), jax.ShapeDtypeStruct((S, N_DEV * n), F32)],
        compiler_params=_cp(("parallel", "arbitrary"), 48),
    )(x, gain, w)


def _rowproj(x, t, w, name):
    S, D = x.shape
    _, k, _ = w.shape
    tm = _tile(S, 512, 16)

    def body(x_ref, t_ref, w_ref, y_ref):
        j = pl.program_id(1)
        part = _dot(t_ref[...], w_ref[...])

        @pl.when(j == 0)
        def _():
            y_ref[...] = x_ref[...] + part

        @pl.when(j > 0)
        def _():
            y_ref[...] += part

    row = pl.BlockSpec((tm, D), lambda m, j: (m, 0))
    return pl.pallas_call(
        body, name=name, grid=(S // tm, N_DEV),
        in_specs=[row, pl.BlockSpec((tm, k), lambda m, j: (m, j)), pl.BlockSpec((None, k, D), lambda m, j: (j, 0, 0))],
        out_specs=row, out_shape=jax.ShapeDtypeStruct((S, D), F32),
        compiler_params=_cp(("parallel", "arbitrary"), 48),
    )(x, t, w)


def _dgrad_row(dy, w, out_dtype, name):
    S, D = dy.shape
    _, k, _ = w.shape
    tm = _tile(S, 512, 16)

    def body(dy_ref, w_ref, dt_ref, dyb_ref):
        @pl.when(pl.program_id(1) == 0)
        def _():
            dyb_ref[...] = dy_ref[...].astype(BF)

        dt_ref[...] = _dot(dyb_ref[...], w_ref[...], NT).astype(out_dtype)

    return pl.pallas_call(
        body, name=name, grid=(S // tm, N_DEV),
        in_specs=[pl.BlockSpec((tm, D), lambda m, j: (m, 0)), pl.BlockSpec((None, k, D), lambda m, j: (j, 0, 0))],
        out_specs=pl.BlockSpec((tm, k), lambda m, j: (m, j)),
        out_shape=jax.ShapeDtypeStruct((S, N_DEV * k), out_dtype),
        scratch_shapes=[pltpu.VMEM((tm, D), BF)],
        compiler_params=_cp(("parallel", "arbitrary"), 48),
    )(dy, w)


def _dgrad_col_norm(dres, x, gain, dz, w, name):
    S, D = x.shape
    _, _, n = w.shape
    tm = _tile(S, 256, 16)

    def body(dres_ref, x_ref, g_ref, dz_ref, w_ref, dx_ref, dg_ref, acc_ref):
        m, j = pl.program_id(0), pl.program_id(1)

        @pl.when(j == 0)
        def _():
            acc_ref[...] = jnp.zeros_like(acc_ref)

        acc_ref[...] += _dot(dz_ref[...], w_ref[...], NT)

        @pl.when(j == N_DEV - 1)
        def _():
            dx, dgain = _rms_bwd(acc_ref[...], x_ref[...], g_ref[...], dres_ref[...])
            dx_ref[...] = dx
            _accum_rows(dg_ref, dgain, m == 0)

    row = pl.BlockSpec((tm, D), lambda m, j: (m, 0))
    return pl.pallas_call(
        body, name=name, grid=(S // tm, N_DEV),
        in_specs=[row, row, pl.BlockSpec((1, D), lambda m, j: (0, 0)), pl.BlockSpec((tm, n), lambda m, j: (m, j)),
                  pl.BlockSpec((None, D, n), lambda m, j: (j, 0, 0))],
        out_specs=[row, pl.BlockSpec((8, D), lambda m, j: (0, 0))],
        out_shape=[jax.ShapeDtypeStruct((S, D), F32), jax.ShapeDtypeStruct((8, D), F32)],
        scratch_shapes=[pltpu.VMEM((tm, D), F32)],
        compiler_params=_cp(("arbitrary", "arbitrary"), 48),
    )(dres, x, gain, dz, w)


def _headnorm_fwd(xa, col0, width, gain_row, scale, name):
    S = xa.shape[0]
    cb = _tile(width, 1024, LANE)
    tm = _tile(S, 512, 8)
    off = col0 // cb

    def body(x_ref, g_ref, y_ref):
        for c in range(cb // LANE):
            sl = slice(c * LANE, (c + 1) * LANE)
            xhat, _ = _rms_hat(x_ref[:, sl])
            y_ref[:, sl] = xhat * (g_ref[:, sl] * scale)

    return pl.pallas_call(
        body, name=name, grid=(S // tm, width // cb),
        in_specs=[pl.BlockSpec((tm, cb), lambda m, c: (m, off + c)), pl.BlockSpec((1, cb), lambda m, c: (0, c))],
        out_specs=pl.BlockSpec((tm, cb), lambda m, c: (m, c)),
        out_shape=jax.ShapeDtypeStruct((S, width), F32),
        compiler_params=_cp(("parallel", "parallel")),
    )(xa, gain_row)


def _headnorm_bwd(dy, xa, col0, gain_row, scale, name):
    S, width = dy.shape
    cb = _tile(width, 1024, LANE)
    tm = _tile(S, 512, 16)
    off = col0 // cb

    def body(dy_ref, x_ref, g_ref, dx_ref, dg_ref):
        m = pl.program_id(1)
        rows = []
        for c in range(cb // LANE):
            sl = slice(c * LANE, (c + 1) * LANE)
            dx, dgain = _rms_bwd(dy_ref[:, sl] * scale, x_ref[:, sl], g_ref[:, sl], 0.0)
            dx_ref[:, sl] = dx.astype(BF)
            rows.append(dgain)
        _accum_rows(dg_ref, jnp.concatenate(rows, axis=1), m == 0)

    return pl.pallas_call(
        body, name=name, grid=(width // cb, S // tm),
        in_specs=[pl.BlockSpec((tm, cb), lambda c, m: (m, c)), pl.BlockSpec((tm, cb), lambda c, m: (m, off + c)),
                  pl.BlockSpec((1, cb), lambda c, m: (0, c))],
        out_specs=[pl.BlockSpec((tm, cb), lambda c, m: (m, c)), pl.BlockSpec((8, cb), lambda c, m: (0, c))],
        out_shape=[jax.ShapeDtypeStruct((S, width), BF), jax.ShapeDtypeStruct((8, width), F32)],
        compiler_params=_cp(("parallel", "arbitrary")),
    )(dy, xa, gain_row)


def _causal():
    p = lax.broadcasted_iota(jnp.int32, (LANE, LANE), 0)
    q = lax.broadcasted_iota(jnp.int32, (LANE, LANE), 1)
    return p >= q


def _gmlp_fwd(z, v_gain, ws, bias):
    S, dg2 = z.shape
    dg = dg2 // 2
    G = dg // LANE

    def body(z_ref, vg_ref, ws_ref, bias_ref, t_ref):
        u, _ = _gelu(z_ref[:, :dg])
        v, _ = _gelu(z_ref[:, dg:])
        vhat, _ = _rms_hat(v)
        vn = (vhat * vg_ref[...]).astype(BF)
        mask = _causal()
        for g in range(G):
            sl = slice(g * LANE, (g + 1) * LANE)
            wm = jnp.where(mask, ws_ref[g], 0.0).astype(BF)
            sv = _dot(wm, vn[:, sl]) + bias_ref[:, sl]
            t_ref[:, sl] = (u[:, sl] * sv).astype(BF)

    return pl.pallas_call(
        body, name="gmlp_fwd", grid=(S // LANE,),
        in_specs=[pl.BlockSpec((LANE, dg2), lambda n: (n, 0)), pl.BlockSpec((1, dg), lambda n: (0, 0)),
                  pl.BlockSpec((G, LANE, LANE), lambda n: (0, 0, 0)), pl.BlockSpec((LANE, dg), lambda n: (0, 0))],
        out_specs=pl.BlockSpec((LANE, dg), lambda n: (n, 0)),
        out_shape=jax.ShapeDtypeStruct((S, dg), BF),
        compiler_params=_cp(("parallel",)),
    )(z, v_gain, ws, bias)


def _gmlp_bwd(z, dt, v_gain, ws, bias):
    S, dg2 = z.shape
    dg = dg2 // 2
    G = dg // LANE

    def body(z_ref, dt_ref, vg_ref, ws_ref, bias_ref, dz_ref, dws_ref, db_ref, dvg_ref, dvn_ref):
        n = pl.program_id(0)
        zu, zv = z_ref[:, :dg], z_ref[:, dg:]
        u, tu = _gelu(zu)
        v, tv = _gelu(zv)
        vhat, r = _rms_hat(v)
        vn = (vhat * vg_ref[...]).astype(BF)
        mask = _causal()

        @pl.when(n == 0)
        def _():
            dws_ref[...] = jnp.zeros_like(dws_ref)
            db_ref[...] = jnp.zeros_like(db_ref)

        for g in range(G):
            sl = slice(g * LANE, (g + 1) * LANE)
            wm = jnp.where(mask, ws_ref[g], 0.0).astype(BF)
            sv = _dot(wm, vn[:, sl]) + bias_ref[:, sl]
            dtg = dt_ref[:, sl].astype(F32)
            dz_ref[:, sl] = (dtg * sv * _gelu_grad(zu[:, sl], tu[:, sl])).astype(BF)
            dsv = dtg * u[:, sl]
            dsvb = dsv.astype(BF)
            dvn_ref[:, sl] = _dot(wm, dsvb, TN)
            dws_ref[g] += jnp.where(mask, _dot(dsvb, vn[:, sl], NT), 0.0)
            db_ref[:, sl] += jnp.broadcast_to(jnp.sum(dsv, axis=1, keepdims=True), (LANE, LANE))

        dvn = dvn_ref[...]
        dxhat = dvn * vg_ref[...]
        dv = r * (dxhat - vhat * jnp.mean(dxhat * vhat, axis=-1, keepdims=True))
        dz_ref[:, dg:] = (dv * _gelu_grad(zv, tv)).astype(BF)
        _accum_rows(dvg_ref, jnp.sum(dvn * vhat, axis=0, keepdims=True), n == 0)

    return pl.pallas_call(
        body, name="gmlp_bwd", grid=(S // LANE,),
        in_specs=[pl.BlockSpec((LANE, dg2), lambda n: (n, 0)), pl.BlockSpec((LANE, dg), lambda n: (n, 0)),
                  pl.BlockSpec((1, dg), lambda n: (0, 0)), pl.BlockSpec((G, LANE, LANE), lambda n: (0, 0, 0)),
                  pl.BlockSpec((LANE, dg), lambda n: (0, 0))],
        out_specs=[pl.BlockSpec((LANE, dg2), lambda n: (n, 0)), pl.BlockSpec((G, LANE, LANE), lambda n: (0, 0, 0)),
                   pl.BlockSpec((LANE, dg), lambda n: (0, 0)), pl.BlockSpec((8, dg), lambda n: (0, 0))],
        out_shape=[jax.ShapeDtypeStruct((S, dg2), BF), jax.ShapeDtypeStruct((G, LANE, LANE), F32),
                   jax.ShapeDtypeStruct((LANE, dg), F32), jax.ShapeDtypeStruct((8, dg), F32)],
        scratch_shapes=[pltpu.VMEM((LANE, dg), F32)],
        compiler_params=_cp(("arbitrary",)),
    )(z, dt, v_gain, ws, bias)


def _sub(ref, r, dil):
    return ref[...] if dil == 1 else ref[pl.ds(r, LANE, stride=dil), :]


def _sub_store(ref, r, dil, val):
    if dil == 1:
        ref[...] = val
    else:
        ref[pl.ds(r, LANE, stride=dil), :] = val


def _slope_times_dil(h, n_heads, dil, shape):
    hv = jnp.zeros(shape, F32) + (h + 1).astype(F32)
    return jnp.exp(hv * (-8.0 / n_heads * math.log(2.0))) * float(dil)


def _band_bias(h, n_heads, b, dil):
    qi = lax.broadcasted_iota(jnp.int32, (LANE, 2 * LANE), 0)
    kj = lax.broadcasted_iota(jnp.int32, (LANE, 2 * LANE), 1)
    delta = qi + LANE - kj
    valid = (delta >= 0) & (delta <= LANE) & ((kj >= LANE) | (b > 0))
    return jnp.where(valid, -_slope_times_dil(h, n_heads, dil, (LANE, 2 * LANE)) * delta.astype(F32), NEG)


def _attn_specs(S, H, g, dil):
    W = LANE * dil
    nb = S // W
    col = g * H
    cur = lambda c0: pl.BlockSpec((W, LANE), lambda h, b: (b, c0 + h))
    prev = lambda c0: pl.BlockSpec((W, LANE), lambda h, b: (jnp.maximum(b - 1, 0), c0 + h))
    nxt = lambda c0: pl.BlockSpec((W, LANE), lambda h, b: (jnp.minimum(b + 1, nb - 1), c0 + h))
    return W, nb, col, cur, prev, nxt


def _attn_fwd(q, k, kv, g, dil):
    S = q.shape[0]
    H = q.shape[1] // (3 * LANE)
    W, nb, col, cur, prev, _ = _attn_specs(S, H, g, dil)
    vcol = 3 * H + col

    def body(q_ref, kc_ref, kp_ref, vc_ref, vp_ref, o_ref, l_ref, qs, ks, vs, os_, ls):
        h, b = pl.program_id(0), pl.program_id(1)
        for r in range(dil):
            qs[r] = _sub(q_ref, r, dil)
            ks[r, :LANE] = _sub(kp_ref, r, dil)
            ks[r, LANE:] = _sub(kc_ref, r, dil)
            vs[r, :LANE] = _sub(vp_ref, r, dil)
            vs[r, LANE:] = _sub(vc_ref, r, dil)
        bias = _band_bias(h, H, b, dil)

        def step(r, carry):
            s = _dot(qs[r].astype(BF), ks[r].astype(BF), NT) + bias
            mx = jnp.max(s, axis=-1, keepdims=True)
            p = jnp.exp(s - mx)
            l = jnp.sum(p, axis=-1, keepdims=True)
            os_[r] = _dot((p / l).astype(BF), vs[r].astype(BF))
            ls[r] = jnp.broadcast_to(mx + jnp.log(l), (LANE, LANE))
            return carry

        lax.fori_loop(0, dil, step, 0)
        for r in range(dil):
            _sub_store(o_ref, r, dil, os_[r])
            _sub_store(l_ref, r, dil, ls[r])

    out = jax.ShapeDtypeStruct((S, H * LANE), F32)
    return pl.pallas_call(
        body, name="attn_fwd_d%d" % dil, grid=(H, nb),
        in_specs=[cur(col), cur(col), prev(col), cur(vcol), prev(vcol)],
        out_specs=[cur(0), cur(0)], out_shape=[out, out],
        scratch_shapes=[pltpu.VMEM((dil, LANE, LANE), F32), pltpu.VMEM((dil, 2 * LANE, LANE), F32),
                        pltpu.VMEM((dil, 2 * LANE, LANE), F32), pltpu.VMEM((dil, LANE, LANE), F32),
                        pltpu.VMEM((dil, LANE, LANE), F32)],
        compiler_params=_cp(("parallel", "parallel")),
    )(q, k, k, kv, kv)


def _attn_combine(os_, ls_):
    S, C = os_[0].shape
    tm = _tile(S, 256, 16)

    def body(o0, o1, o2, l0, l1, l2, o_ref, lse_ref):
        a, b, c = l0[...], l1[...], l2[...]
        mx = jnp.maximum(jnp.maximum(a, b), c)
        ea, eb, ec = jnp.exp(a - mx), jnp.exp(b - mx), jnp.exp(c - mx)
        den = ea + eb + ec
        o_ref[...] = ((ea * o0[...] + eb * o1[...] + ec * o2[...]) / den).astype(BF)
        lse_ref[...] = mx + jnp.log(den)

    blk = pl.BlockSpec((tm, C), lambda m: (m, 0))
    return pl.pallas_call(
        body, name="attn_combine", grid=(S // tm,), in_specs=[blk] * 6, out_specs=[blk, blk],
        out_shape=[jax.ShapeDtypeStruct((S, C), BF), jax.ShapeDtypeStruct((S, C), F32)],
        compiler_params=_cp(("parallel",)),
    )(*os_, *ls_)


def _attn_delta(do, o):
    S, C = do.shape
    tm = _tile(S, 512, 16)

    def body(do_ref, o_ref, d_ref):
        for c in range(C // LANE):
            sl = slice(c * LANE, (c + 1) * LANE)
            prod = do_ref[:, sl].astype(BF).astype(F32) * o_ref[:, sl].astype(F32)
            d_ref[:, sl] = jnp.broadcast_to(jnp.sum(prod, axis=-1, keepdims=True), (tm, LANE))

    blk = pl.BlockSpec((tm, C), lambda m: (m, 0))
    return pl.pallas_call(
        body, name="attn_delta", grid=(S // tm,), in_specs=[blk, blk], out_specs=blk,
        out_shape=jax.ShapeDtypeStruct((S, C), F32), compiler_params=_cp(("parallel",)),
    )(do, o)


def _attn_bwd_dq(q, k, kv, do, lse, dl, g, dil):
    S = q.shape[0]
    H = q.shape[1] // (3 * LANE)
    W, nb, col, cur, prev, _ = _attn_specs(S, H, g, dil)
    vcol = 3 * H + col

    def body(q_ref, kc_ref, kp_ref, vc_ref, vp_ref, do_ref, l_ref, d_ref, dq_ref, qs, ks, vs, dos, ls, ds_, dqs):
        h, b = pl.program_id(0), pl.program_id(1)
        for r in range(dil):
            qs[r] = _sub(q_ref, r, dil)
            ks[r, :LANE] = _sub(kp_ref, r, dil)
            ks[r, LANE:] = _sub(kc_ref, r, dil)
            vs[r, :LANE] = _sub(vp_ref, r, dil)
            vs[r, LANE:] = _sub(vc_ref, r, dil)
            dos[r] = _sub(do_ref, r, dil)
            ls[r] = _sub(l_ref, r, dil)
            ds_[r] = _sub(d_ref, r, dil)
        bias = _band_bias(h, H, b, dil)

        def step(r, carry):
            kb = ks[r].astype(BF)
            s = _dot(qs[r].astype(BF), kb, NT) + bias
            p = jnp.exp(s - ls[r][:, :1])
            dp = _dot(dos[r].astype(BF), vs[r].astype(BF), NT)
            dsc = p * (dp - ds_[r][:, :1])
            dqs[r] = _dot(dsc.astype(BF), kb)
            return carry

        lax.fori_loop(0, dil, step, 0)
        for r in range(dil):
            _sub_store(dq_ref, r, dil, dqs[r])

    sq = pltpu.VMEM((dil, LANE, LANE), F32)
    sk = pltpu.VMEM((dil, 2 * LANE, LANE), F32)
    return pl.pallas_call(
        body, name="attn_bwd_dq_d%d" % dil, grid=(H, nb),
        in_specs=[cur(col), cur(col), prev(col), cur(vcol), prev(vcol), cur(0), cur(0), cur(0)],
        out_specs=cur(0), out_shape=jax.ShapeDtypeStruct((S, H * LANE), F32),
        scratch_shapes=[sq, sk, sk, sq, sq, sq, sq],
        compiler_params=_cp(("parallel", "parallel")),
    )(q, k, k, kv, kv, do, lse, dl)


def _attn_bwd_dkv(q, k, kv, do, lse, dl, g, dil):
    S = q.shape[0]
    H = q.shape[1] // (3 * LANE)
    W, nb, col, cur, _, nxt = _attn_specs(S, H, g, dil)
    vcol = 3 * H + col

    def body(k_ref, v_ref, qc_ref, qn_ref, doc_ref, don_ref, lc_ref, ln_ref, dc_ref, dn_ref, dk_ref, dv_ref,
             ks, vs, qs, dos, ls, ds_, dks, dvs):
        h, b = pl.program_id(0), pl.program_id(1)
        for r in range(dil):
            ks[r] = _sub(k_ref, r, dil)
            vs[r] = _sub(v_ref, r, dil)
            for i, (qr, dr, lr, er) in enumerate(((qc_ref, doc_ref, lc_ref, dc_ref), (qn_ref, don_ref, ln_ref, dn_ref))):
                qs[i, r] = _sub(qr, r, dil)
                dos[i, r] = _sub(dr, r, dil)
                ls[i, r] = _sub(lr, r, dil)
                ds_[i, r] = _sub(er, r, dil)
        qi = lax.broadcasted_iota(jnp.int32, (LANE, LANE), 0)
        kj = lax.broadcasted_iota(jnp.int32, (LANE, LANE), 1)
        sd = _slope_times_dil(h, H, dil, (LANE, LANE))
        bias_c = jnp.where(qi >= kj, -sd * (qi - kj).astype(F32), NEG)
        bias_n = jnp.where((qi <= kj) & (b < nb - 1), -sd * (qi + LANE - kj).astype(F32), NEG)

        def step(r, carry):
            kb = ks[r].astype(BF)
            vb = vs[r].astype(BF)
            dk = jnp.zeros((LANE, LANE), F32)
            dv = jnp.zeros((LANE, LANE), F32)
            for i, bias in enumerate((bias_c, bias_n)):
                qb = qs[i, r].astype(BF)
                dob = dos[i, r].astype(BF)
                s = _dot(qb, kb, NT) + bias
                p = jnp.exp(s - ls[i, r][:, :1])
                dp = _dot(dob, vb, NT)
                dsc = p * (dp - ds_[i, r][:, :1])
                dv = dv + _dot(p.astype(BF), dob, TN)
                dk = dk + _dot(dsc.astype(BF), qb, TN)
            dks[r] = dk
            dvs[r] = dv
            return carry

        lax.fori_loop(0, dil, step, 0)
        for r in range(dil):
            _sub_store(dk_ref, r, dil, dks[r])
            _sub_store(dv_ref, r, dil, dvs[r])

    s1 = pltpu.VMEM((dil, LANE, LANE), F32)
    s2 = pltpu.VMEM((2, dil, LANE, LANE), F32)
    out = jax.ShapeDtypeStruct((S, H * LANE), F32)
    return pl.pallas_call(
        body, name="attn_bwd_dkv_d%d" % dil, grid=(H, nb),
        in_specs=[cur(col), cur(vcol), cur(col), nxt(col), cur(0), nxt(0), cur(0), nxt(0), cur(0), nxt(0)],
        out_specs=[cur(0), cur(0)], out_shape=[out, out],
        scratch_shapes=[s1, s1, s2, s2, s2, s2, s1, s1],
        compiler_params=_cp(("parallel", "parallel")),
    )(k, kv, q, q, do, do, lse, lse, dl, dl)


def _loss_head(y, target):
    S, D = y.shape
    tm = _tile(S, 512, 8)

    def body(y_ref, t_ref, dy_ref, l_ref):
        e = y_ref[...] - t_ref[...]
        dy_ref[...] = e * (1.0 / D)
        part = jnp.broadcast_to(jnp.sum(jnp.sum(e * e, axis=1, keepdims=True), axis=0, keepdims=True) * (0.5 / D), (8, LANE))
        _accum_rows(l_ref, part, pl.program_id(0) == 0)

    blk = pl.BlockSpec((tm, D), lambda m: (m, 0))
    return pl.pallas_call(
        body, name="loss_head", grid=(S // tm,), in_specs=[blk, blk],
        out_specs=[blk, pl.BlockSpec((8, LANE), lambda m: (0, 0))],
        out_shape=[jax.ShapeDtypeStruct((S, D), F32), jax.ShapeDtypeStruct((8, LANE), F32)],
        compiler_params=_cp(("arbitrary",)),
    )(y, target)


def _add2(a, b, name):
    S, C = a.shape
    tm = _tile(S, 128, 8)

    def body(a_ref, b_ref, o_ref):
        o_ref[...] = a_ref[...] + b_ref[...]

    blk = pl.BlockSpec((tm, C), lambda m: (m, 0))
    return pl.pallas_call(
        body, name=name, grid=(S // tm,), in_specs=[blk, blk], out_specs=blk,
        out_shape=jax.ShapeDtypeStruct((S, C), a.dtype), compiler_params=_cp(("parallel",)),
    )(a, b)


def _bf(w):
    return w.astype(BF)


def kernel(x, ffn1_norm, ffn1_w_gate, ffn1_w_up, ffn1_w_down, mix_norm, ffn2_norm, ffn2_w_gate, ffn2_w_up, ffn2_w_down, gmlp_w_in, gmlp_v_norm, gmlp_w_s, gmlp_b_s, gmlp_w_out, kv_norm, w_kv, k_norm, attn_w_q, attn_q_norm, attn_w_o, loss_target, m_ffn1_norm, m_ffn1_w_gate, m_ffn1_w_up, m_ffn1_w_down, m_mix_norm, m_ffn2_norm, m_ffn2_w_gate, m_ffn2_w_up, m_ffn2_w_down, m_gmlp_w_in, m_gmlp_v_norm, m_gmlp_w_s, m_gmlp_b_s, m_gmlp_w_out, m_kv_norm, m_w_kv, m_k_norm, m_attn_w_q, m_attn_q_norm, m_attn_w_o, v_ffn1_norm, v_ffn1_w_gate, v_ffn1_w_up, v_ffn1_w_down, v_mix_norm, v_ffn2_norm, v_ffn2_w_gate, v_ffn2_w_up, v_ffn2_w_down, v_gmlp_w_in, v_gmlp_v_norm, v_gmlp_w_s, v_gmlp_b_s, v_gmlp_w_out, v_kv_norm, v_w_kv, v_k_norm, v_attn_w_q, v_attn_q_norm, v_attn_w_o):
    names = ["ffn1_norm", "ffn1_w_gate", "ffn1_w_up", "ffn1_w_down", "mix_norm", "ffn2_norm", "ffn2_w_gate",
             "ffn2_w_up", "ffn2_w_down", "gmlp_w_in", "gmlp_v_norm", "gmlp_w_s", "gmlp_b_s", "gmlp_w_out",
             "kv_norm", "w_kv", "k_norm", "attn_w_q", "attn_q_norm", "attn_w_o"]
    W = dict(zip(names, [ffn1_norm, ffn1_w_gate, ffn1_w_up, ffn1_w_down, mix_norm, ffn2_norm, ffn2_w_gate,
                         ffn2_w_up, ffn2_w_down, gmlp_w_in, gmlp_v_norm, gmlp_w_s, gmlp_b_s, gmlp_w_out,
                         kv_norm, w_kv, k_norm, attn_w_q, attn_q_norm, attn_w_o]))
    M = dict(zip(names, [m_ffn1_norm, m_ffn1_w_gate, m_ffn1_w_up, m_ffn1_w_down, m_mix_norm, m_ffn2_norm, m_ffn2_w_gate,
                         m_ffn2_w_up, m_ffn2_w_down, m_gmlp_w_in, m_gmlp_v_norm, m_gmlp_w_s, m_gmlp_b_s, m_gmlp_w_out,
                         m_kv_norm, m_w_kv, m_k_norm, m_attn_w_q, m_attn_q_norm, m_attn_w_o]))
    V = dict(zip(names, [v_ffn1_norm, v_ffn1_w_gate, v_ffn1_w_up, v_ffn1_w_down, v_mix_norm, v_ffn2_norm, v_ffn2_w_gate,
                         v_ffn2_w_up, v_ffn2_w_down, v_gmlp_w_in, v_gmlp_v_norm, v_gmlp_w_s, v_gmlp_b_s, v_gmlp_w_out,
                         v_kv_norm, v_w_kv, v_k_norm, v_attn_w_q, v_attn_q_norm, v_attn_w_o]))

    depth = ffn1_norm.shape[0]
    n_a = gmlp_w_in.shape[0]
    S, D = x.shape[1], x.shape[2]
    H = D // LANE
    n_grp = len(DILATIONS)
    hw = H * LANE
    xi, yi, ci = _me()
    core = jnp.reshape(ci, (1,)).astype(jnp.int32)
    chip = jnp.reshape(2 * xi + yi, (1,)).astype(jnp.int32)
    dev = 4 * xi + 2 * yi + ci
    q_scale = LANE ** -0.5

    def gain(v):
        return v.reshape(1, -1)

    def head_gain(g3):
        return jnp.tile(g3[:, None, :], (1, H, 1)).reshape(1, n_grp * hw)

    v_gain_all = _allgather(jnp.pad(gmlp_v_norm, ((0, 8 - n_a), (0, 0))), "allgather_vnorm")
    v_gain_all = jnp.transpose(v_gain_all[:, :n_a], (1, 0, 2)).reshape(n_a, -1)

    cur = x.reshape(S, D)
    saved = []
    k_sh = kv_raw = kv_hn = kv_x = w_kv_g = None

    for l in range(depth):
        rec = {}
        for tag, (nn, wgn, wun, wdn) in (("f1", ("ffn1_norm", "ffn1_w_gate", "ffn1_w_up", "ffn1_w_down")),
                                         ("f2", ("ffn2_norm", "ffn2_w_gate", "ffn2_w_up", "ffn2_w_down"))):
            if tag == "f2":
                rec["mix_x"] = cur
                if l < n_a:
                    w_in = _allgather(_bf(gmlp_w_in[l]), "allgather_gmlp_in")
                    w_out = _allgather(_bf(gmlp_w_out[l]), "allgather_proj_out")
                    bias = jnp.repeat(gmlp_b_s[l].T, LANE, axis=1)
                    hm, z = _normproj(cur, gain(mix_norm[l]), w_in, "gmlp_in")
                    t = _gmlp_fwd(z, gain(v_gain_all[l]), gmlp_w_s[l], bias)
                    cur = _rowproj(cur, t, w_out, "proj_out")
                    rec.update(w_in=w_in, w_out=w_out, bias=bias, hm=hm, z=z, t=t)
                else:
                    jj = l - n_a
                    w_q = _allgather(_bf(attn_w_q[jj]), "allgather_attn_q")
                    w_o = _allgather(_bf(attn_w_o[jj]), "allgather_proj_out")
                    hm, q_raw = _normproj(cur, gain(mix_norm[l]), w_q, "attn_q")
                    qg = head_gain(attn_q_norm[jj])
                    q = _headnorm_fwd(q_raw, 0, n_grp * hw, qg, q_scale, "headnorm_q")
                    outs = [_attn_fwd(q, k_sh, kv_raw, g, dil) for g, dil in enumerate(DILATIONS)]
                    o, lse = _attn_combine([o_ for o_, _ in outs], [l_ for _, l_ in outs])
                    cur = _rowproj(cur, o, w_o, "proj_out")
                    rec.update(w_q=w_q, w_o=w_o, hm=hm, q_raw=q_raw, qg=qg, q=q, o=o, lse=lse)
            wg = _allgather(_bf(W[wgn][l]), "allgather_ffn_in")
            wu = _allgather(_bf(W[wun][l]), "allgather_ffn_in")
            wd = _allgather(_bf(W[wdn][l]), "allgather_ffn_out")
            rec[tag + "_x"] = cur
            cur, hn, a, b = _ffn_fwd(cur, gain(W[nn][l]), wg, wu, wd)
            rec[tag] = (wg, wu, wd, hn, a, b)
        if l == n_a - 1:
            w_kv_g = _allgather(_bf(w_kv), "allgather_kv")
            kv_x = cur
            kv_hn, kv_raw = _normproj(cur, gain(kv_norm), w_kv_g, "kv_proj")
            k_gain = head_gain(k_norm)
            k_sh = _headnorm_fwd(kv_raw, 0, n_grp * hw, k_gain, 1.0, "headnorm_k")
        saved.append(rec)

    dcur, loss_part = _loss_head(cur, loss_target.reshape(S, D))
    loss = lax.psum(loss_part[0, 0], ("x", "y", "c"))

    grads = {}
    small = {}

    def put(dct, name, l, val, n_layers):
        dct.setdefault(name, [None] * n_layers)[l] = val

    dk_acc = dv_acc = None
    for l in reversed(range(depth)):
        rec = saved[l]
        if l == n_a - 1:
            dk_raw, dkg = _headnorm_bwd(dk_acc, kv_raw, 0, k_gain, 1.0, "headnorm_k_bwd")
            dkv = jnp.concatenate([dk_raw, dv_acc.astype(BF)], axis=1)
            grads["w_kv"] = [_wgrad(kv_hn, "full", dkv, "cols", "wgrad_kv")]
            dcur, dg = _dgrad_col_norm(dcur, kv_x, gain(kv_norm), dkv, w_kv_g, "dgrad_kv")
            small["kv_norm"] = [dg[0]]
            small["k_norm"] = [dkg[0].reshape(n_grp, H, LANE).sum(axis=1)]
        for tag, (nn, wgn, wun, wdn) in (("f2", ("ffn2_norm", "ffn2_w_gate", "ffn2_w_up", "ffn2_w_down")),
                                         ("f1", ("ffn1_norm", "ffn1_w_gate", "ffn1_w_up", "ffn1_w_down"))):
            wg, wu, wd, hn, a, b = rec[tag]
            dcur, da, db, act, dyh, dg = _ffn_bwd_dx(dcur, rec[tag + "_x"], gain(W[nn][l]), a, b, wg, wu, wd)
            put(grads, wgn, l, _wgrad(hn, "full", da, "stack", "wgrad_ffn_in"), depth)
            put(grads, wun, l, _wgrad(hn, "full", db, "stack", "wgrad_ffn_in"), depth)
            put(grads, wdn, l, _wgrad(act, "stack", dyh, "full", "wgrad_ffn_out"), depth)
            put(small, nn, l, dg[0], depth)
            if tag == "f2":
                mix_x = rec["mix_x"]
                if l < n_a:
                    dt = _dgrad_row(dcur, rec["w_out"], BF, "dgrad_gmlp_out")
                    put(grads, "gmlp_w_out", l, _wgrad(rec["t"], "cols", dcur, "full", "wgrad_proj_out"), n_a)
                    dz, dws, dbias, dvg = _gmlp_bwd(rec["z"], dt, gain(v_gain_all[l]), gmlp_w_s[l], rec["bias"])
                    put(grads, "gmlp_w_in", l, _wgrad(rec["hm"], "full", dz, "cols", "wgrad_gmlp_in"), n_a)
                    dcur, dg = _dgrad_col_norm(dcur, mix_x, gain(mix_norm[l]), dz, rec["w_in"], "dgrad_gmlp_in")
                    put(small, "gmlp_w_s", l, dws, n_a)
                    put(small, "gmlp_b_s", l, dbias[:, ::LANE].T, n_a)
                    put(small, "gmlp_v_norm", l, dvg[0], n_a)
                else:
                    jj = l - n_a
                    do = _dgrad_row(dcur, rec["w_o"], F32, "dgrad_attn_out")
                    put(grads, "attn_w_o", jj, _wgrad(rec["o"], "cols", dcur, "full", "wgrad_proj_out"), depth - n_a)
                    dl = _attn_delta(do, rec["o"])
                    dqs, dks, dvs = [], [], []
                    for g, dil in enumerate(DILATIONS):
                        dqs.append(_attn_bwd_dq(rec["q"], k_sh, kv_raw, do, rec["lse"], dl, g, dil))
                        dk_g, dv_g = _attn_bwd_dkv(rec["q"], k_sh, kv_raw, do, rec["lse"], dl, g, dil)
                        dks.append(dk_g)
                        dvs.append(dv_g)
                    dq = jnp.concatenate(dqs, axis=1)
                    dk_l = jnp.concatenate(dks, axis=1)
                    dv_l = jnp.concatenate(dvs, axis=1)
                    dk_acc = dk_l if dk_acc is None else _add2(dk_acc, dk_l, "add_dk")
                    dv_acc = dv_l if dv_acc is None else _add2(dv_acc, dv_l, "add_dk")
                    dq_raw, dqg = _headnorm_bwd(dq, rec["q_raw"], 0, rec["qg"], q_scale, "headnorm_q_bwd")
                    put(grads, "attn_w_q", jj, _wgrad(rec["hm"], "full", dq_raw, "cols", "wgrad_attn_q"), depth - n_a)
                    dcur, dg = _dgrad_col_norm(dcur, mix_x, gain(mix_norm[l]), dq_raw, rec["w_q"], "dgrad_attn_q")
                    put(small, "attn_q_norm", jj, dqg[0].reshape(n_grp, H, LANE).sum(axis=1), depth - n_a)
                put(small, "mix_norm", l, dg[0], depth)
    grad_x = dcur.reshape(1, S, D)

    out_g, out_d, out_m, out_v = {}, {}, {}, {}
    for name, parts in grads.items():
        res = []
        for l, p in enumerate(parts):
            w_l, m_l, v_l = (W[name], M[name], V[name]) if W[name].ndim == 2 else (W[name][l], M[name][l], V[name][l])
            r1 = _core_exchange(p, "rs_core_exchange")
            qsum = _pair_add(p, r1, core)
            r2 = _chip_exchange(qsum, "rs_chip_exchange")
            res.append(_adamw_shard(qsum, r2, chip, w_l, m_l, v_l))
        for dct, i in ((out_g, 0), (out_d, 1), (out_m, 2), (out_v, 3)):
            dct[name] = res[0][i] if W[name].ndim == 2 else jnp.stack([r[i] for r in res])

    small_names = [n for n in names if n in small]
    full_shape = {n: (W[n].shape if n != "gmlp_v_norm" else (n_a, v_gain_all.shape[1])) for n in small_names}
    flat = jnp.concatenate([jnp.stack(small[n]).reshape(-1) if W[n].ndim > 1 else small[n][0].reshape(-1)
                            for n in small_names])
    n_flat = flat.shape[0]
    rows = -(-n_flat // (8 * LANE)) * 8

    def pack(parts_list):
        v = jnp.concatenate([p.reshape(-1) for p in parts_list])
        return jnp.pad(v, (0, rows * LANE - n_flat)).reshape(rows, LANE)

    def full_of(dct, n, fill):
        if n != "gmlp_v_norm":
            return dct[n]
        sh = dct[n].shape[1]
        return lax.dynamic_update_slice(jnp.full(full_shape[n], fill, F32), dct[n], (0, dev * sh))

    g_all = _allgather(jnp.pad(flat, (0, rows * LANE - n_flat)).reshape(rows, LANE), "allgather_small_grads")
    w_p = pack([full_of(W, n, 0.0) for n in small_names])
    m_p = pack([full_of(M, n, 0.0) for n in small_names])
    v_p = pack([full_of(V, n, 1.0) for n in small_names])
    packed = _adamw_replicated(g_all, w_p, m_p, v_p)
    offs = 0
    for n in small_names:
        size = math.prod(full_shape[n])
        for dct, arr in zip((out_g, out_d, out_m, out_v), packed):
            val = arr.reshape(-1)[offs:offs + size].reshape(full_shape[n])
            if n == "gmlp_v_norm":
                sh = W[n].shape[1]
                val = lax.dynamic_slice(val, (0, dev * sh), (n_a, sh))
            dct[n] = val
        offs += size

    return (loss, grad_x, *[out_g[n] for n in names], *[out_d[n] for n in names],
            *[out_m[n] for n in names], *[out_v[n] for n in names])
```

```python
import math

import jax
import jax.numpy as jnp
from jax import lax
from jax.experimental import pallas as pl
from jax.experimental.pallas import tpu as pltpu

F32 = jnp.float32
BF = jnp.bfloat16
N_DEV = 8
N_CHIP = 4
LANE = 128
ATT_WIN = 16 * LANE
EPS = 1e-6
NEG = -1e30
DILATIONS = (1, 4, 16)
ADAM_LR, ADAM_B1, ADAM_B2, ADAM_EPS, ADAM_WD, ADAM_STEP = 0.001, 0.9, 0.999, 1e-08, 0.01, 10
GELU_C0, GELU_C1 = 0.7978845608028654, 0.044715
VMEM_MB = 2 ** 20
BUDGET_US = {"ffn_fwd": 360.0, "normproj": 150.0, "rowproj": 100.0,
             "ffn_bwd_dx": 300.0, "wgrad": 90.0, "dgrad_col_norm": 150.0}
COST_US_PER_ELEM = {"gather": 1.0e-4, "core": 1.4e-5, "chip": 9.0e-5}

MESH_T = pl.DeviceIdType.MESH
ANY = pl.BlockSpec(memory_space=pl.ANY)
DMA_SEM = pltpu.SemaphoreType.DMA
NT = (((1,), (1,)), ((), ()))
TN = (((0,), (0,)), ((), ()))


def _tile(n, target, mult):
    best = None
    for t in range(mult, min(n, target) + 1, mult):
        if n % t == 0:
            best = t
    if best is None:
        best = n
    return best


def _dot(a, b, dims=None):
    if dims is None:
        return jnp.dot(a, b, preferred_element_type=F32)
    return lax.dot_general(a, b, dims, preferred_element_type=F32)


def _rms_hat(xv):
    r = lax.rsqrt(jnp.mean(xv * xv, axis=-1, keepdims=True) + EPS)
    return xv * r, r


def _rms_bwd(dhn, xv, gain, dres):
    xhat, r = _rms_hat(xv)
    dxhat = dhn * gain
    dx = dres + r * (dxhat - xhat * jnp.mean(dxhat * xhat, axis=-1, keepdims=True))
    return dx, jnp.sum(dhn * xhat, axis=0, keepdims=True)


def _accum_rows(ref, row, first):
    val = jnp.broadcast_to(row, ref.shape)

    @pl.when(first)
    def _():
        ref[...] = val

    @pl.when(jnp.logical_not(first))
    def _():
        ref[...] += val


def _gelu(z):
    t = jnp.tanh(GELU_C0 * (z + GELU_C1 * z * z * z))
    return 0.5 * z * (1.0 + t), t


def _gelu_grad(z, t):
    return 0.5 * (1.0 + t) + 0.5 * z * (1.0 - t * t) * GELU_C0 * (1.0 + 3.0 * GELU_C1 * z * z)


def _me():
    return lax.axis_index("x"), lax.axis_index("y"), lax.axis_index("c")


def _gather_phase(phase, x_ref, out_ref, send_sems, recv_sems, local_sem):
    x, y, c = _me()
    me, sibling = (x, y, c), (x, y, 1 - c)
    chips = [(1 - x, y), (x, 1 - y), (1 - x, 1 - y)]

    def slot(px, py, pc):
        return out_ref.at[4 * px + 2 * py + pc]

    def copy(k, block, to, src=None):
        return pltpu.make_async_remote_copy(
            src_ref=slot(*block) if src is None else src, dst_ref=slot(*block),
            send_sem=send_sems.at[k], recv_sem=recv_sems.at[k], device_id=to, device_id_type=MESH_T)

    mine = pltpu.make_async_copy(x_ref, slot(*me), local_sem)
    first = [copy(0, me, sibling, src=x_ref)]
    first += [copy(1 + j, me, (*chip, c), src=x_ref) for j, chip in enumerate(chips)]
    passed = [copy(4 + j, (*chip, c), sibling) for j, chip in enumerate(chips)]
    if phase == 0:
        mine.start()
        for cp in first:
            cp.start()
    elif phase == 1:
        for j, chip in enumerate(chips):
            copy(1 + j, (*chip, c), me).wait_recv()
            passed[j].start()
    else:
        copy(0, sibling, me).wait_recv()
        for j, chip in enumerate(chips):
            copy(4 + j, (*chip, 1 - c), me).wait_recv()
        for cp in first + passed:
            cp.wait_send()
        mine.wait()


def _exchange_phase(phase, kind, src_ref, dst_ref, send_sems, recv_sems):
    x, y, c = _me()
    if kind == "core":
        plan = [(2 * k + (1 - c), k, (x, y, 1 - c)) for k in range(N_CHIP)]
    else:
        plan = [(2 * px + py, t, (px, py, c)) for t, (px, py) in enumerate([(1 - x, y), (x, 1 - y), (1 - x, 1 - y)])]
    cps = [pltpu.make_async_remote_copy(
        src_ref=src_ref.at[s], dst_ref=dst_ref.at[d], send_sem=send_sems.at[i], recv_sem=recv_sems.at[i],
        device_id=to, device_id_type=MESH_T) for i, (s, d, to) in enumerate(plan)]
    if phase == 0:
        for cp in cps:
            cp.start()
    elif phase == 2:
        for cp in cps:
            cp.wait()


_N_COPIES = {"gather": 7, "core": N_CHIP, "chip": 3}


class _Carried:
    def __init__(self, items=()):
        self.items = list(items)

    def arrays(self):
        return [a for _, a in self.items]

    def out_shapes(self):
        lead = {"gather": lambda a: (N_DEV,) + a.shape, "core": lambda a: (N_CHIP,) + a.shape[1:],
                "chip": lambda a: (3,) + a.shape[1:]}
        return [jax.ShapeDtypeStruct(lead[k](a), a.dtype) for k, a in self.items]

    def scratch(self):
        res = []
        for k, _ in self.items:
            res += [DMA_SEM((_N_COPIES[k],)), DMA_SEM((_N_COPIES[k],))]
            if k == "gather":
                res.append(DMA_SEM(()))
        return res

    def emit(self, phase, in_refs, out_refs, scr):
        i = 0
        for (kind, _), src, dst in zip(self.items, in_refs, out_refs):
            if kind == "gather":
                _gather_phase(phase, src, dst, scr[i], scr[i + 1], scr[i + 2])
                i += 3
            else:
                _exchange_phase(phase, kind, src, dst, scr[i], scr[i + 1])
                i += 2


def _comm_only(carried, name):
    nc = len(carried.items)

    def body(*refs):
        for phase in range(3):
            carried.emit(phase, refs[:nc], refs[nc:2 * nc], refs[2 * nc:])

    return pl.pallas_call(
        body, name=name, out_shape=carried.out_shapes(), in_specs=[ANY] * nc, out_specs=[ANY] * nc,
        scratch_shapes=carried.scratch(),
    )(*carried.arrays())


def _pcall(main, *, name, grid, in_specs, out_specs, out_shape, args, scratch=(), sem=None, vmem=48,
           carried=None, aliases=None):
    params = pltpu.CompilerParams(dimension_semantics=sem, vmem_limit_bytes=vmem * VMEM_MB)
    n_in, n_out, n_scr = len(in_specs), len(out_specs), len(scratch)
    if carried is None or not carried.items:
        outs = pl.pallas_call(
            main, name=name, grid=grid, in_specs=in_specs, out_specs=out_specs, out_shape=out_shape,
            scratch_shapes=list(scratch), compiler_params=params, input_output_aliases=aliases or {},
        )(*args)
        return list(outs), []
    nc = len(carried.items)
    total = math.prod(grid)

    def body(*refs):
        ins, cin = refs[:n_in], refs[n_in:n_in + nc]
        o0 = n_in + nc
        outs, cout = refs[o0:o0 + n_out], refs[o0 + n_out:o0 + n_out + nc]
        s0 = o0 + n_out + nc
        scr, cscr = refs[s0:s0 + n_scr], refs[s0 + n_scr:]
        step = 0
        for d, n in enumerate(grid):
            step = step * n + pl.program_id(d)

        @pl.when(step == 0)
        def _():
            carried.emit(0, cin, cout, cscr)

        main(*ins, *outs, *scr)

        @pl.when(step == total // 2)
        def _():
            carried.emit(1, cin, cout, cscr)

        @pl.when(step == total - 1)
        def _():
            carried.emit(2, cin, cout, cscr)

    outs = pl.pallas_call(
        body, name=name, grid=grid, in_specs=list(in_specs) + [ANY] * nc, out_specs=list(out_specs) + [ANY] * nc,
        out_shape=list(out_shape) + carried.out_shapes(), scratch_shapes=list(scratch) + carried.scratch(),
        compiler_params=params, input_output_aliases=aliases or {},
    )(*args, *carried.arrays())
    return list(outs[:n_out]), list(outs[n_out:])


def _pair_add(p, r1, core):
    _, rows, cols = p.shape
    tr = _tile(rows, 512, 16)
    p4 = p.reshape(N_CHIP, 2, rows, cols)

    def body(core_ref, p_ref, r_ref, q_ref):
        q_ref[...] = (p_ref[...].astype(F32) + r_ref[...].astype(F32)).astype(BF)

    grid_spec = pltpu.PrefetchScalarGridSpec(
        num_scalar_prefetch=1, grid=(N_CHIP, rows // tr),
        in_specs=[pl.BlockSpec((None, None, tr, cols), lambda k, i, cr: (k, cr[0], i, 0)),
                  pl.BlockSpec((None, tr, cols), lambda k, i, cr: (k, i, 0))],
        out_specs=pl.BlockSpec((None, tr, cols), lambda k, i, cr: (k, i, 0)))
    return pl.pallas_call(
        body, name="pair_add", grid_spec=grid_spec, out_shape=jax.ShapeDtypeStruct((N_CHIP, rows, cols), BF),
        compiler_params=pltpu.CompilerParams(dimension_semantics=("parallel", "parallel")),
    )(core, p4, r1)


def _adam_math(g, w, m, v):
    m2 = ADAM_B1 * m + (1.0 - ADAM_B1) * g
    v2 = ADAM_B2 * v + (1.0 - ADAM_B2) * (g * g)
    m_hat = m2 / (1.0 - ADAM_B1 ** ADAM_STEP)
    v_hat = v2 / (1.0 - ADAM_B2 ** ADAM_STEP)
    delta = -ADAM_LR * (m_hat / (jnp.sqrt(v_hat) + ADAM_EPS) + ADAM_WD * w)
    return delta, m2, v2


def _adamw_shard(q, r2, chip, w, m, v, layer, prev):
    n_layers, rows, cols = w.shape
    tr = _tile(rows, 256, 16)

    def body(chip_ref, q_ref, r_ref, w_ref, m_ref, v_ref, *rest):
        g_ref, d_ref, m2_ref, v2_ref = rest[-4:]
        g = q_ref[...].astype(F32) + r_ref[0].astype(F32) + r_ref[1].astype(F32) + r_ref[2].astype(F32)
        d, m2, v2 = _adam_math(g, w_ref[...], m_ref[...], v_ref[...])
        g_ref[...] = g
        d_ref[...] = d
        m2_ref[...] = m2
        v2_ref[...] = v2

    blk = pl.BlockSpec((None, tr, cols), lambda i, cr: (layer, i, 0))
    in_specs = [pl.BlockSpec((None, tr, cols), lambda i, cr: (cr[0], i, 0)),
                pl.BlockSpec((3, tr, cols), lambda i, cr: (0, i, 0)), blk, blk, blk]
    args = [chip, q, r2, w, m, v]
    aliases = {}
    if prev is not None:
        in_specs += [ANY] * 4
        args += list(prev)
        aliases = {6 + i: i for i in range(4)}
    grid_spec = pltpu.PrefetchScalarGridSpec(
        num_scalar_prefetch=1, grid=(rows // tr,), in_specs=in_specs, out_specs=[blk, blk, blk, blk])
    out = jax.ShapeDtypeStruct((n_layers, rows, cols), F32)
    return pl.pallas_call(
        body, name="adamw_shard", grid_spec=grid_spec, out_shape=[out, out, out, out], input_output_aliases=aliases,
        compiler_params=pltpu.CompilerParams(dimension_semantics=("parallel",)),
    )(*args)


def _adamw_replicated(parts, w, m, v):
    rows, cols = w.shape
    tr = _tile(rows, 512, 8)

    def body(p_ref, w_ref, m_ref, v_ref, g_ref, d_ref, m2_ref, v2_ref):
        g = p_ref[0]
        for k in range(1, N_DEV):
            g = g + p_ref[k]
        d, m2, v2 = _adam_math(g, w_ref[...], m_ref[...], v_ref[...])
        g_ref[...] = g
        d_ref[...] = d
        m2_ref[...] = m2
        v2_ref[...] = v2

    blk = pl.BlockSpec((tr, cols), lambda i: (i, 0))
    out = jax.ShapeDtypeStruct((rows, cols), F32)
    outs, _ = _pcall(body, name="adamw_replicated", grid=(rows // tr,),
                     in_specs=[pl.BlockSpec((N_DEV, tr, cols), lambda i: (0, i, 0)), blk, blk, blk],
                     out_specs=[blk, blk, blk, blk], out_shape=[out, out, out, out], sem=("parallel",),
                     args=(parts, w, m, v))
    return outs


def _ffn_fwd(x, gain, wg, wu, wd, carried=None):
    S, D = x.shape
    nsh, _, fs = wg.shape
    tm = _tile(S, 512, 16)

    def body(x_ref, g_ref, wg_ref, wu_ref, wd_ref, y_ref, hn_ref, a_ref, b_ref, acc_ref):
        j = pl.program_id(1)

        @pl.when(j == 0)
        def _():
            xhat, _ = _rms_hat(x_ref[...])
            hn_ref[...] = (xhat * g_ref[...]).astype(BF)
            acc_ref[...] = jnp.zeros_like(acc_ref)

        hn = hn_ref[...]
        a = _dot(hn, wg_ref[...])
        b = _dot(hn, wu_ref[...])
        a_ref[...] = a.astype(BF)
        b_ref[...] = b.astype(BF)
        act = (a * jax.nn.sigmoid(a) * b).astype(BF)
        acc_ref[...] += _dot(act, wd_ref[...])

        @pl.when(j == nsh - 1)
        def _():
            y_ref[...] = x_ref[...] + 0.5 * acc_ref[...]

    row = pl.BlockSpec((tm, D), lambda m, j: (m, 0))
    hid = pl.BlockSpec((None, tm, fs), lambda m, j: (j, m, 0))
    return _pcall(
        body, name="ffn_fwd", grid=(S // tm, nsh),
        in_specs=[row, pl.BlockSpec((1, D), lambda m, j: (0, 0)),
                  pl.BlockSpec((None, D, fs), lambda m, j: (j, 0, 0)),
                  pl.BlockSpec((None, D, fs), lambda m, j: (j, 0, 0)),
                  pl.BlockSpec((None, fs, D), lambda m, j: (j, 0, 0))],
        out_specs=[row, row, hid, hid],
        out_shape=[jax.ShapeDtypeStruct((S, D), F32), jax.ShapeDtypeStruct((S, D), BF),
                   jax.ShapeDtypeStruct((nsh, S, fs), BF), jax.ShapeDtypeStruct((nsh, S, fs), BF)],
        scratch=[pltpu.VMEM((tm, D), F32)], sem=("parallel", "arbitrary"), vmem=56,
        args=(x, gain, wg, wu, wd), carried=carried)


def _ffn_bwd_dx(dy, x, gain, a, b, wg, wu, wd, carried=None):
    S, D = x.shape
    nsh, _, fs = wg.shape
    tm = _tile(S, 512, 16)

    def body(dy_ref, x_ref, g_ref, a_ref, b_ref, wg_ref, wu_ref, wd_ref,
             dx_ref, da_ref, db_ref, act_ref, dyh_ref, dg_ref, acc_ref):
        m, j = pl.program_id(0), pl.program_id(1)

        @pl.when(j == 0)
        def _():
            dyh_ref[...] = (0.5 * dy_ref[...]).astype(BF)
            acc_ref[...] = jnp.zeros_like(acc_ref)

        dact = _dot(dyh_ref[...], wd_ref[...], NT)
        av = a_ref[...].astype(F32)
        bv = b_ref[...].astype(F32)
        sg = jax.nn.sigmoid(av)
        sil = av * sg
        da = (dact * bv * (sg * (1.0 + av * (1.0 - sg)))).astype(BF)
        db = (dact * sil).astype(BF)
        da_ref[...] = da
        db_ref[...] = db
        act_ref[...] = (sil * bv).astype(BF)
        acc_ref[...] += _dot(da, wg_ref[...], NT) + _dot(db, wu_ref[...], NT)

        @pl.when(j == nsh - 1)
        def _():
            dx, dgain = _rms_bwd(acc_ref[...], x_ref[...], g_ref[...], dy_ref[...])
            dx_ref[...] = dx
            _accum_rows(dg_ref, dgain, m == 0)

    row = pl.BlockSpec((tm, D), lambda m, j: (m, 0))
    row1 = pl.BlockSpec((tm, D), lambda m, j: (m, 0), pipeline_mode=pl.Buffered(1))
    hid = pl.BlockSpec((None, tm, fs), lambda m, j: (j, m, 0))
    hshape = jax.ShapeDtypeStruct((nsh, S, fs), BF)
    return _pcall(
        body, name="ffn_bwd_dx", grid=(S // tm, nsh),
        in_specs=[row1, row1, pl.BlockSpec((1, D), lambda m, j: (0, 0)), hid, hid,
                  pl.BlockSpec((None, D, fs), lambda m, j: (j, 0, 0)),
                  pl.BlockSpec((None, D, fs), lambda m, j: (j, 0, 0)),
                  pl.BlockSpec((None, fs, D), lambda m, j: (j, 0, 0))],
        out_specs=[row1, hid, hid, hid, row1, pl.BlockSpec((8, D), lambda m, j: (0, 0))],
        out_shape=[jax.ShapeDtypeStruct((S, D), F32), hshape, hshape, hshape,
                   jax.ShapeDtypeStruct((S, D), BF), jax.ShapeDtypeStruct((8, D), F32)],
        scratch=[pltpu.VMEM((tm, D), F32)], sem=("arbitrary", "arbitrary"), vmem=60,
        args=(dy, x, gain, a, b, wg, wu, wd), carried=carried)


def _opspec(arr, kind, tm, grid_mj):
    if kind == "full":
        return pl.BlockSpec((tm, arr.shape[1]), lambda *g: (grid_mj(*g)[0], 0)), arr.shape[1]
    if kind == "cols":
        n = arr.shape[1] // N_DEV
        return pl.BlockSpec((tm, n), lambda *g: grid_mj(*g)), n
    n = arr.shape[2]
    return pl.BlockSpec((None, tm, n), lambda *g: (grid_mj(*g)[1], grid_mj(*g)[0], 0)), n


def _wgrad(a, a_kind, b, b_kind, name, carried=None):
    S = a.shape[0] if a_kind != "stack" else a.shape[1]
    tm = _tile(S, 512, 16)
    mj = lambda j, m: (m, j)
    a_spec, ka = _opspec(a, a_kind, tm, mj)
    b_spec, nb = _opspec(b, b_kind, tm, mj)
    n_m = S // tm

    def body(a_ref, b_ref, o_ref, acc_ref):
        m = pl.program_id(1)

        @pl.when(m == 0)
        def _():
            acc_ref[...] = jnp.zeros_like(acc_ref)

        acc_ref[...] += _dot(a_ref[...].astype(BF), b_ref[...].astype(BF), TN)

        @pl.when(m == n_m - 1)
        def _():
            o_ref[...] = acc_ref[...].astype(BF)

    outs, cout = _pcall(
        body, name=name, grid=(N_DEV, n_m), in_specs=[a_spec, b_spec],
        out_specs=[pl.BlockSpec((None, ka, nb), lambda j, m: (j, 0, 0))],
        out_shape=[jax.ShapeDtypeStruct((N_DEV, ka, nb), BF)],
        scratch=[pltpu.VMEM((ka, nb), F32)], sem=("parallel", "arbitrary"), args=(a, b), carried=carried)
    return outs[0], cout


def _normproj(x, gain, w, name, carried=None):
    S, D = x.shape
    _, _, n = w.shape
    tm = _tile(S, 512, 16)

    def body(x_ref, g_ref, w_ref, hn_ref, y_ref):
        @pl.when(pl.program_id(1) == 0)
        def _():
            xhat, _ = _rms_hat(x_ref[...])
            hn_ref[...] = (xhat * g_ref[...]).astype(BF)

        y_ref[...] = _dot(hn_ref[...], w_ref[...])

    row = pl.BlockSpec((tm, D), lambda m, j: (m, 0))
    return _pcall(
        body, name=name, grid=(S // tm, N_DEV),
        in_specs=[row, pl.BlockSpec((1, D), lambda m, j: (0, 0)), pl.BlockSpec((None, D, n), lambda m, j: (j, 0, 0))],
        out_specs=[row, pl.BlockSpec((tm, n), lambda m, j: (m, j))],
        out_shape=[jax.ShapeDtypeStruct((S, D), BF), jax.ShapeDtypeStruct((S, N_DEV * n), F32)],
        sem=("parallel", "arbitrary"), args=(x, gain, w), carried=carried)


def _rowproj(x, t, w, name, carried=None):
    S, D = x.shape
    _, k, _ = w.shape
    tm = _tile(S, 512, 16)

    def body(x_ref, t_ref, w_ref, y_ref):
        j = pl.program_id(1)
        part = _dot(t_ref[...], w_ref[...])

        @pl.when(j == 0)
        def _():
            y_ref[...] = x_ref[...] + part

        @pl.when(j > 0)
        def _():
            y_ref[...] += part

    row = pl.BlockSpec((tm, D), lambda m, j: (m, 0))
    outs, cout = _pcall(
        body, name=name, grid=(S // tm, N_DEV),
        in_specs=[row, pl.BlockSpec((tm, k), lambda m, j: (m, j)), pl.BlockSpec((None, k, D), lambda m, j: (j, 0, 0))],
        out_specs=[row], out_shape=[jax.ShapeDtypeStruct((S, D), F32)],
        sem=("parallel", "arbitrary"), args=(x, t, w), carried=carried)
    return outs[0], cout


def _dgrad_row(dy, w, out_dtype, name):
    S, D = dy.shape
    _, k, _ = w.shape
    tm = _tile(S, 512, 16)

    def body(dy_ref, w_ref, dt_ref, dyb_ref):
        @pl.when(pl.program_id(1) == 0)
        def _():
            dyb_ref[...] = dy_ref[...].astype(BF)

        dt_ref[...] = _dot(dyb_ref[...], w_ref[...], NT).astype(out_dtype)

    outs, _ = _pcall(
        body, name=name, grid=(S // tm, N_DEV),
        in_specs=[pl.BlockSpec((tm, D), lambda m, j: (m, 0)), pl.BlockSpec((None, k, D), lambda m, j: (j, 0, 0))],
        out_specs=[pl.BlockSpec((tm, k), lambda m, j: (m, j))],
        out_shape=[jax.ShapeDtypeStruct((S, N_DEV * k), out_dtype)],
        scratch=[pltpu.VMEM((tm, D), BF)], sem=("parallel", "arbitrary"), args=(dy, w))
    return outs[0]


def _dgrad_col_norm(dres, x, gain, dz, w, name, carried=None):
    S, D = x.shape
    _, _, n = w.shape
    tm = _tile(S, 256, 16)

    def body(dres_ref, x_ref, g_ref, dz_ref, w_ref, dx_ref, dg_ref, acc_ref):
        m, j = pl.program_id(0), pl.program_id(1)

        @pl.when(j == 0)
        def _():
            acc_ref[...] = jnp.zeros_like(acc_ref)

        acc_ref[...] += _dot(dz_ref[...], w_ref[...], NT)

        @pl.when(j == N_DEV - 1)
        def _():
            dx, dgain = _rms_bwd(acc_ref[...], x_ref[...], g_ref[...], dres_ref[...])
            dx_ref[...] = dx
            _accum_rows(dg_ref, dgain, m == 0)

    row = pl.BlockSpec((tm, D), lambda m, j: (m, 0))
    return _pcall(
        body, name=name, grid=(S // tm, N_DEV),
        in_specs=[row, row, pl.BlockSpec((1, D), lambda m, j: (0, 0)), pl.BlockSpec((tm, n), lambda m, j: (m, j)),
                  pl.BlockSpec((None, D, n), lambda m, j: (j, 0, 0))],
        out_specs=[row, pl.BlockSpec((8, D), lambda m, j: (0, 0))],
        out_shape=[jax.ShapeDtypeStruct((S, D), F32), jax.ShapeDtypeStruct((8, D), F32)],
        scratch=[pltpu.VMEM((tm, D), F32)], sem=("arbitrary", "arbitrary"),
        args=(dres, x, gain, dz, w), carried=carried)


def _headnorm_fwd(xa, width, gain_row, scale, name):
    S = xa.shape[0]
    cb = _tile(width, 1024, LANE)
    tm = _tile(S, 512, 8)

    def body(x_ref, g_ref, y_ref):
        for c in range(cb // LANE):
            sl = slice(c * LANE, (c + 1) * LANE)
            xhat, _ = _rms_hat(x_ref[:, sl])
            y_ref[:, sl] = xhat * (g_ref[:, sl] * scale)

    outs, _ = _pcall(
        body, name=name, grid=(S // tm, width // cb),
        in_specs=[pl.BlockSpec((tm, cb), lambda m, c: (m, c)), pl.BlockSpec((1, cb), lambda m, c: (0, c))],
        out_specs=[pl.BlockSpec((tm, cb), lambda m, c: (m, c))],
        out_shape=[jax.ShapeDtypeStruct((S, width), F32)], sem=("parallel", "parallel"), args=(xa, gain_row))
    return outs[0]


def _headnorm_bwd(dy, xa, gain_row, scale, name, tail=None):
    S, width = dy.shape
    cb = _tile(width, 1024, LANE)
    tm = _tile(S, 512, 16)
    ncb = width // cb
    ntail = 0 if tail is None else tail.shape[1] // cb

    def body(dy_ref, x_ref, g_ref, *rest):
        dx_ref, dg_ref = rest[-2:]
        c, m = pl.program_id(0), pl.program_id(1)

        @pl.when(c < ncb)
        def _():
            rows = []
            for i in range(cb // LANE):
                sl = slice(i * LANE, (i + 1) * LANE)
                dx, dgain = _rms_bwd(dy_ref[:, sl] * scale, x_ref[:, sl], g_ref[:, sl], 0.0)
                dx_ref[:, sl] = dx.astype(BF)
                rows.append(dgain)
            _accum_rows(dg_ref, jnp.concatenate(rows, axis=1), m == 0)

        if tail is not None:
            @pl.when(c >= ncb)
            def _():
                dx_ref[...] = rest[0][...].astype(BF)

    head = lambda c: jnp.minimum(c, ncb - 1)
    in_specs = [pl.BlockSpec((tm, cb), lambda c, m: (jnp.where(c < ncb, m, 0), head(c))),
                pl.BlockSpec((tm, cb), lambda c, m: (jnp.where(c < ncb, m, 0), head(c))),
                pl.BlockSpec((1, cb), lambda c, m: (0, head(c)))]
    args = [dy, xa, gain_row]
    if tail is not None:
        in_specs.append(pl.BlockSpec((tm, cb), lambda c, m: (jnp.where(c >= ncb, m, 0), jnp.maximum(c - ncb, 0))))
        args.append(tail)
    outs, _ = _pcall(
        body, name=name, grid=(ncb + ntail, S // tm), in_specs=in_specs,
        out_specs=[pl.BlockSpec((tm, cb), lambda c, m: (m, c)), pl.BlockSpec((8, cb), lambda c, m: (0, head(c)))],
        out_shape=[jax.ShapeDtypeStruct((S, width + ntail * cb), BF), jax.ShapeDtypeStruct((8, width), F32)],
        sem=("arbitrary", "arbitrary"), args=args)
    return outs


def _causal():
    p = lax.broadcasted_iota(jnp.int32, (LANE, LANE), 0)
    q = lax.broadcasted_iota(jnp.int32, (LANE, LANE), 1)
    return p >= q


def _gmlp_fwd(z, v_gain, ws, bias):
    S, dg2 = z.shape
    dg = dg2 // 2
    G = dg // LANE

    def body(z_ref, vg_ref, ws_ref, bias_ref, t_ref):
        u, _ = _gelu(z_ref[:, :dg])
        v, _ = _gelu(z_ref[:, dg:])
        vhat, _ = _rms_hat(v)
        vn = (vhat * vg_ref[...]).astype(BF)
        mask = _causal()
        for g in range(G):
            sl = slice(g * LANE, (g + 1) * LANE)
            wm = jnp.where(mask, ws_ref[g], 0.0).astype(BF)
            sv = _dot(wm, vn[:, sl]) + bias_ref[:, sl]
            t_ref[:, sl] = (u[:, sl] * sv).astype(BF)

    outs, _ = _pcall(
        body, name="gmlp_fwd", grid=(S // LANE,),
        in_specs=[pl.BlockSpec((LANE, dg2), lambda n: (n, 0)), pl.BlockSpec((1, dg), lambda n: (0, 0)),
                  pl.BlockSpec((G, LANE, LANE), lambda n: (0, 0, 0)), pl.BlockSpec((LANE, dg), lambda n: (0, 0))],
        out_specs=[pl.BlockSpec((LANE, dg), lambda n: (n, 0))],
        out_shape=[jax.ShapeDtypeStruct((S, dg), BF)], sem=("parallel",), args=(z, v_gain, ws, bias))
    return outs[0]


def _gmlp_bwd(z, dt, v_gain, ws, bias):
    S, dg2 = z.shape
    dg = dg2 // 2
    G = dg // LANE

    def body(z_ref, dt_ref, vg_ref, ws_ref, bias_ref, dz_ref, dws_ref, db_ref, dvg_ref, dvn_ref):
        n = pl.program_id(0)
        zu, zv = z_ref[:, :dg], z_ref[:, dg:]
        u, tu = _gelu(zu)
        v, tv = _gelu(zv)
        vhat, r = _rms_hat(v)
        vn = (vhat * vg_ref[...]).astype(BF)
        mask = _causal()

        @pl.when(n == 0)
        def _():
            dws_ref[...] = jnp.zeros_like(dws_ref)
            db_ref[...] = jnp.zeros_like(db_ref)

        for g in range(G):
            sl = slice(g * LANE, (g + 1) * LANE)
            wm = jnp.where(mask, ws_ref[g], 0.0).astype(BF)
            sv = _dot(wm, vn[:, sl]) + bias_ref[:, sl]
            dtg = dt_ref[:, sl].astype(F32)
            dz_ref[:, sl] = (dtg * sv * _gelu_grad(zu[:, sl], tu[:, sl])).astype(BF)
            dsv = dtg * u[:, sl]
            dsvb = dsv.astype(BF)
            dvn_ref[:, sl] = _dot(wm, dsvb, TN)
            dws_ref[g] += jnp.where(mask, _dot(dsvb, vn[:, sl], NT), 0.0)
            db_ref[:, sl] += jnp.broadcast_to(jnp.sum(dsv, axis=1, keepdims=True), (LANE, LANE))

        dvn = dvn_ref[...]
        dxhat = dvn * vg_ref[...]
        dv = r * (dxhat - vhat * jnp.mean(dxhat * vhat, axis=-1, keepdims=True))
        dz_ref[:, dg:] = (dv * _gelu_grad(zv, tv)).astype(BF)
        _accum_rows(dvg_ref, jnp.sum(dvn * vhat, axis=0, keepdims=True), n == 0)

    outs, _ = _pcall(
        body, name="gmlp_bwd", grid=(S // LANE,),
        in_specs=[pl.BlockSpec((LANE, dg2), lambda n: (n, 0)), pl.BlockSpec((LANE, dg), lambda n: (n, 0)),
                  pl.BlockSpec((1, dg), lambda n: (0, 0)), pl.BlockSpec((G, LANE, LANE), lambda n: (0, 0, 0)),
                  pl.BlockSpec((LANE, dg), lambda n: (0, 0))],
        out_specs=[pl.BlockSpec((LANE, dg2), lambda n: (n, 0)), pl.BlockSpec((G, LANE, LANE), lambda n: (0, 0, 0)),
                   pl.BlockSpec((LANE, dg), lambda n: (0, 0)), pl.BlockSpec((8, dg), lambda n: (0, 0))],
        out_shape=[jax.ShapeDtypeStruct((S, dg2), BF), jax.ShapeDtypeStruct((G, LANE, LANE), F32),
                   jax.ShapeDtypeStruct((LANE, dg), F32), jax.ShapeDtypeStruct((8, dg), F32)],
        scratch=[pltpu.VMEM((LANE, dg), F32)], sem=("arbitrary",), args=(z, dt, v_gain, ws, bias))
    return outs


N_ENT = ATT_WIN // LANE


def _rows(ref, start, dil):
    return ref[pl.ds(start, LANE), :] if dil == 1 else ref[pl.ds(start, LANE, stride=dil), :]


def _rows_store(ref, start, dil, val):
    if dil == 1:
        ref[pl.ds(start, LANE), :] = val
    else:
        ref[pl.ds(start, LANE, stride=dil), :] = val


def _slope_times_dil(h, n_heads, dil, shape):
    hv = jnp.zeros(shape, F32) + (h + 1).astype(F32)
    return jnp.exp(hv * (-8.0 / n_heads * math.log(2.0))) * float(dil)


def _band_bias(h, n_heads, dil, has_prev):
    qi = lax.broadcasted_iota(jnp.int32, (LANE, 2 * LANE), 0)
    kj = lax.broadcasted_iota(jnp.int32, (LANE, 2 * LANE), 1)
    delta = qi + LANE - kj
    valid = (delta >= 0) & (delta <= LANE) & ((kj >= LANE) | has_prev)
    return jnp.where(valid, -_slope_times_dil(h, n_heads, dil, (LANE, 2 * LANE)) * delta.astype(F32), NEG)


def _attn_geom(S, H, g, dil):
    pb = LANE * dil
    nblk = ATT_WIN // pb
    nw = S // ATT_WIN
    win = lambda c0: pl.BlockSpec((ATT_WIN, LANE), lambda h, w: (w, c0 + h))
    prev = lambda c0: pl.BlockSpec((pb, LANE), lambda h, w: (jnp.maximum(w * nblk - 1, 0), c0 + h))
    nxt = lambda c0: pl.BlockSpec((pb, LANE), lambda h, w: (jnp.minimum((w + 1) * nblk, nw * nblk - 1), c0 + h))
    return pb, nblk, nw, g * H, win, prev, nxt


def _stage_band(dst, cur_ref, prev_ref, dil, pb, nblk):
    for blk in range(nblk):
        for r in range(dil):
            e = blk * dil + r
            dst[e, :LANE] = _rows(prev_ref, r, dil) if blk == 0 else _rows(cur_ref, (blk - 1) * pb + r, dil)
            dst[e, LANE:] = _rows(cur_ref, blk * pb + r, dil)


def _stage(dst, ref, dil, pb, nblk, lead=None):
    for blk in range(nblk):
        for r in range(dil):
            val = _rows(ref, blk * pb + r, dil)
            if lead is None:
                dst[blk * dil + r] = val
            else:
                dst[lead, blk * dil + r] = val


def _unstage(ref, src, dil, pb, nblk):
    for blk in range(nblk):
        for r in range(dil):
            _rows_store(ref, blk * pb + r, dil, src[blk * dil + r])


def _attn_fwd(q, k, kv, g, dil):
    S = q.shape[0]
    H = q.shape[1] // (3 * LANE)
    pb, nblk, nw, col, win, prev, _ = _attn_geom(S, H, g, dil)
    vcol = 3 * H + col

    def body(q_ref, kc_ref, kp_ref, vc_ref, vp_ref, o_ref, l_ref, qs, ks, vs, os_, ls):
        h, w = pl.program_id(0), pl.program_id(1)
        _stage(qs, q_ref, dil, pb, nblk)
        _stage_band(ks, kc_ref, kp_ref, dil, pb, nblk)
        _stage_band(vs, vc_ref, vp_ref, dil, pb, nblk)

        def run(lo, hi, bias):
            def step(e, carry):
                s = _dot(qs[e].astype(BF), ks[e].astype(BF), NT) + bias
                mx = jnp.max(s, axis=-1, keepdims=True)
                p = jnp.exp(s - mx)
                l = jnp.sum(p, axis=-1, keepdims=True)
                os_[e] = _dot((p / l).astype(BF), vs[e].astype(BF))
                ls[e] = jnp.broadcast_to(mx + jnp.log(l), (LANE, LANE))
                return carry
            if hi > lo:
                lax.fori_loop(lo, hi, step, 0)

        run(0, dil, _band_bias(h, H, dil, w > 0))
        run(dil, N_ENT, _band_bias(h, H, dil, True))
        _unstage(o_ref, os_, dil, pb, nblk)
        _unstage(l_ref, ls, dil, pb, nblk)

    out = jax.ShapeDtypeStruct((S, H * LANE), F32)
    sq = pltpu.VMEM((N_ENT, LANE, LANE), F32)
    sk = pltpu.VMEM((N_ENT, 2 * LANE, LANE), F32)
    outs, _ = _pcall(
        body, name="attn_fwd_d%d" % dil, grid=(H, nw),
        in_specs=[win(col), win(col), prev(col), win(vcol), prev(vcol)],
        out_specs=[win(0), win(0)], out_shape=[out, out], scratch=[sq, sk, sk, sq, sq],
        sem=("parallel", "parallel"), args=(q, k, k, kv, kv))
    return outs


def _attn_combine(os_, ls_):
    S, C = os_[0].shape
    tm = _tile(S, 256, 16)

    def body(o0, o1, o2, l0, l1, l2, o_ref, lse_ref):
        a, b, c = l0[...], l1[...], l2[...]
        mx = jnp.maximum(jnp.maximum(a, b), c)
        ea, eb, ec = jnp.exp(a - mx), jnp.exp(b - mx), jnp.exp(c - mx)
        den = ea + eb + ec
        o_ref[...] = ((ea * o0[...] + eb * o1[...] + ec * o2[...]) / den).astype(BF)
        lse_ref[...] = mx + jnp.log(den)

    blk = pl.BlockSpec((tm, C), lambda m: (m, 0))
    outs, _ = _pcall(
        body, name="attn_combine", grid=(S // tm,), in_specs=[blk] * 6, out_specs=[blk, blk],
        out_shape=[jax.ShapeDtypeStruct((S, C), BF), jax.ShapeDtypeStruct((S, C), F32)],
        sem=("parallel",), args=(*os_, *ls_))
    return outs


def _attn_delta(do, o):
    S, C = do.shape
    tm = _tile(S, 512, 16)

    def body(do_ref, o_ref, d_ref):
        for c in range(C // LANE):
            sl = slice(c * LANE, (c + 1) * LANE)
            prod = do_ref[:, sl].astype(BF).astype(F32) * o_ref[:, sl].astype(F32)
            d_ref[:, sl] = jnp.broadcast_to(jnp.sum(prod, axis=-1, keepdims=True), (tm, LANE))

    blk = pl.BlockSpec((tm, C), lambda m: (m, 0))
    outs, _ = _pcall(
        body, name="attn_delta", grid=(S // tm,), in_specs=[blk, blk], out_specs=[blk],
        out_shape=[jax.ShapeDtypeStruct((S, C), F32)], sem=("parallel",), args=(do, o))
    return outs[0]


def _attn_bwd_dq(q, k, kv, do, lse, dl, g, dil, dq_prev):
    S = q.shape[0]
    H = q.shape[1] // (3 * LANE)
    pb, nblk, nw, col, win, prev, _ = _attn_geom(S, H, g, dil)
    vcol = 3 * H + col

    def body(q_ref, kc_ref, kp_ref, vc_ref, vp_ref, do_ref, l_ref, d_ref, *rest):
        dq_ref, qs, ks, vs, dos, ls, ds_, dqs = rest[-8:]
        h, w = pl.program_id(0), pl.program_id(1)
        _stage(qs, q_ref, dil, pb, nblk)
        _stage_band(ks, kc_ref, kp_ref, dil, pb, nblk)
        _stage_band(vs, vc_ref, vp_ref, dil, pb, nblk)
        _stage(dos, do_ref, dil, pb, nblk)
        _stage(ls, l_ref, dil, pb, nblk)
        _stage(ds_, d_ref, dil, pb, nblk)

        def run(lo, hi, bias):
            def step(e, carry):
                kb = ks[e].astype(BF)
                s = _dot(qs[e].astype(BF), kb, NT) + bias
                p = jnp.exp(s - ls[e][:, :1])
                dp = _dot(dos[e].astype(BF), vs[e].astype(BF), NT)
                dsc = p * (dp - ds_[e][:, :1])
                dqs[e] = _dot(dsc.astype(BF), kb)
                return carry
            if hi > lo:
                lax.fori_loop(lo, hi, step, 0)

        run(0, dil, _band_bias(h, H, dil, w > 0))
        run(dil, N_ENT, _band_bias(h, H, dil, True))
        _unstage(dq_ref, dqs, dil, pb, nblk)

    sq = pltpu.VMEM((N_ENT, LANE, LANE), F32)
    sk = pltpu.VMEM((N_ENT, 2 * LANE, LANE), F32)
    in_specs = [win(col), win(col), prev(col), win(vcol), prev(vcol), win(0), win(0), win(0)]
    args = [q, k, k, kv, kv, do, lse, dl]
    aliases = {}
    if dq_prev is not None:
        in_specs.append(ANY)
        args.append(dq_prev)
        aliases = {8: 0}
    outs, _ = _pcall(
        body, name="attn_bwd_dq_d%d" % dil, grid=(H, nw), in_specs=in_specs,
        out_specs=[win(col)], out_shape=[jax.ShapeDtypeStruct((S, 3 * H * LANE), F32)],
        scratch=[sq, sk, sk, sq, sq, sq, sq], sem=("parallel", "parallel"), args=args, aliases=aliases)
    return outs[0]


def _attn_bwd_dkv(q, k, kv, do, lse, dl, g, dil, prev_out, accumulate):
    S = q.shape[0]
    H = q.shape[1] // (3 * LANE)
    pb, nblk, nw, col, win, _, nxt = _attn_geom(S, H, g, dil)
    vcol = 3 * H + col
    n_q = N_ENT + dil

    def body(k_ref, v_ref, qc_ref, qn_ref, doc_ref, don_ref, lc_ref, ln_ref, dc_ref, dn_ref, *rest):
        dk_ref, dv_ref, ks, vs, qs, dos, ls, ds_, dks, dvs = rest[-10:]
        h, w = pl.program_id(0), pl.program_id(1)
        _stage(ks, k_ref, dil, pb, nblk)
        _stage(vs, v_ref, dil, pb, nblk)
        for dst, cur, nx in ((qs, qc_ref, qn_ref), (dos, doc_ref, don_ref), (ls, lc_ref, ln_ref), (ds_, dc_ref, dn_ref)):
            _stage(dst, cur, dil, pb, nblk)
            for r in range(dil):
                dst[N_ENT + r] = _rows(nx, r, dil)
        qi = lax.broadcasted_iota(jnp.int32, (LANE, LANE), 0)
        kj = lax.broadcasted_iota(jnp.int32, (LANE, LANE), 1)
        sd = _slope_times_dil(h, H, dil, (LANE, LANE))
        bias_c = jnp.where(qi >= kj, -sd * (qi - kj).astype(F32), NEG)

        def run(lo, hi, has_next):
            bias_n = jnp.where((qi <= kj) & has_next, -sd * (qi + LANE - kj).astype(F32), NEG)

            def step(e, carry):
                kb = ks[e].astype(BF)
                vb = vs[e].astype(BF)
                dk = jnp.zeros((LANE, LANE), F32)
                dv = jnp.zeros((LANE, LANE), F32)
                for eq, bias in ((e, bias_c), (e + dil, bias_n)):
                    qb = qs[eq].astype(BF)
                    dob = dos[eq].astype(BF)
                    s = _dot(qb, kb, NT) + bias
                    p = jnp.exp(s - ls[eq][:, :1])
                    dp = _dot(dob, vb, NT)
                    dsc = p * (dp - ds_[eq][:, :1])
                    dv = dv + _dot(p.astype(BF), dob, TN)
                    dk = dk + _dot(dsc.astype(BF), qb, TN)
                dks[e] = dk
                dvs[e] = dv
                return carry
            if hi > lo:
                lax.fori_loop(lo, hi, step, 0)

        run(0, N_ENT - dil, True)
        run(N_ENT - dil, N_ENT, w < nw - 1)
        if accumulate:
            pk_ref, pv_ref = rest[0], rest[1]
            dk_ref[...] = pk_ref[...]
            dv_ref[...] = pv_ref[...]
            for blk in range(nblk):
                for r in range(dil):
                    e, start = blk * dil + r, blk * pb + r
                    _rows_store(dk_ref, start, dil, _rows(dk_ref, start, dil) + dks[e])
                    _rows_store(dv_ref, start, dil, _rows(dv_ref, start, dil) + dvs[e])
        else:
            _unstage(dk_ref, dks, dil, pb, nblk)
            _unstage(dv_ref, dvs, dil, pb, nblk)

    s1 = pltpu.VMEM((N_ENT, LANE, LANE), F32)
    s2 = pltpu.VMEM((n_q, LANE, LANE), F32)
    out = jax.ShapeDtypeStruct((S, 3 * H * LANE), F32)
    in_specs = [win(col), win(vcol), win(col), nxt(col), win(0), nxt(0), win(0), nxt(0), win(0), nxt(0)]
    args = [k, kv, q, q, do, do, lse, lse, dl, dl]
    aliases = {}
    if prev_out is not None:
        in_specs += [win(col), win(col)] if accumulate else [ANY, ANY]
        args += list(prev_out)
        aliases = {10: 0, 11: 1}
    outs, _ = _pcall(
        body, name="attn_bwd_dkv_d%d%s" % (dil, "_acc" if accumulate else ""), grid=(H, nw), in_specs=in_specs,
        out_specs=[win(col), win(col)], out_shape=[out, out],
        scratch=[s1, s1, s2, s2, s2, s2, s1, s1], sem=("parallel", "parallel"), vmem=56, args=args, aliases=aliases)
    return outs


def _loss_head(y, target):
    S, D = y.shape
    tm = _tile(S, 512, 8)

    def body(y_ref, t_ref, dy_ref, l_ref):
        e = y_ref[...] - t_ref[...]
        dy_ref[...] = e * (1.0 / D)
        part = jnp.broadcast_to(jnp.sum(jnp.sum(e * e, axis=1, keepdims=True), axis=0, keepdims=True) * (0.5 / D), (8, LANE))
        _accum_rows(l_ref, part, pl.program_id(0) == 0)

    blk = pl.BlockSpec((tm, D), lambda m: (m, 0))
    outs, _ = _pcall(
        body, name="loss_head", grid=(S // tm,), in_specs=[blk, blk],
        out_specs=[blk, pl.BlockSpec((8, LANE), lambda m: (0, 0))],
        out_shape=[jax.ShapeDtypeStruct((S, D), F32), jax.ShapeDtypeStruct((8, LANE), F32)],
        sem=("arbitrary",), args=(y, target))
    return outs


class _Queue:
    def __init__(self):
        self.items = []

    def push(self, kind, key, arr):
        self.items.append((kind, key, arr))

    def take(self, budget_us):
        taken, spent = [], 0.0
        while self.items and spent < budget_us:
            item = self.items.pop(0)
            taken.append(item)
            spent += COST_US_PER_ELEM[item[0]] * (item[2].size / item[2].shape[0] if item[0] != "gather" else item[2].size)
        return taken

    def take_keys(self, keys):
        taken = [it for it in self.items if it[1] in keys]
        self.items = [it for it in self.items if it[1] not in keys]
        return taken

    def take_kind(self, kind):
        taken = [it for it in self.items if it[0] == kind]
        self.items = [it for it in self.items if it[0] != kind]
        return taken


def kernel(x, ffn1_norm, ffn1_w_gate, ffn1_w_up, ffn1_w_down, mix_norm, ffn2_norm, ffn2_w_gate, ffn2_w_up, ffn2_w_down, gmlp_w_in, gmlp_v_norm, gmlp_w_s, gmlp_b_s, gmlp_w_out, kv_norm, w_kv, k_norm, attn_w_q, attn_q_norm, attn_w_o, loss_target, m_ffn1_norm, m_ffn1_w_gate, m_ffn1_w_up, m_ffn1_w_down, m_mix_norm, m_ffn2_norm, m_ffn2_w_gate, m_ffn2_w_up, m_ffn2_w_down, m_gmlp_w_in, m_gmlp_v_norm, m_gmlp_w_s, m_gmlp_b_s, m_gmlp_w_out, m_kv_norm, m_w_kv, m_k_norm, m_attn_w_q, m_attn_q_norm, m_attn_w_o, v_ffn1_norm, v_ffn1_w_gate, v_ffn1_w_up, v_ffn1_w_down, v_mix_norm, v_ffn2_norm, v_ffn2_w_gate, v_ffn2_w_up, v_ffn2_w_down, v_gmlp_w_in, v_gmlp_v_norm, v_gmlp_w_s, v_gmlp_b_s, v_gmlp_w_out, v_kv_norm, v_w_kv, v_k_norm, v_attn_w_q, v_attn_q_norm, v_attn_w_o):
    names = ["ffn1_norm", "ffn1_w_gate", "ffn1_w_up", "ffn1_w_down", "mix_norm", "ffn2_norm", "ffn2_w_gate",
             "ffn2_w_up", "ffn2_w_down", "gmlp_w_in", "gmlp_v_norm", "gmlp_w_s", "gmlp_b_s", "gmlp_w_out",
             "kv_norm", "w_kv", "k_norm", "attn_w_q", "attn_q_norm", "attn_w_o"]
    W = dict(zip(names, [ffn1_norm, ffn1_w_gate, ffn1_w_up, ffn1_w_down, mix_norm, ffn2_norm, ffn2_w_gate,
                         ffn2_w_up, ffn2_w_down, gmlp_w_in, gmlp_v_norm, gmlp_w_s, gmlp_b_s, gmlp_w_out,
                         kv_norm, w_kv, k_norm, attn_w_q, attn_q_norm, attn_w_o]))
    M = dict(zip(names, [m_ffn1_norm, m_ffn1_w_gate, m_ffn1_w_up, m_ffn1_w_down, m_mix_norm, m_ffn2_norm, m_ffn2_w_gate,
                         m_ffn2_w_up, m_ffn2_w_down, m_gmlp_w_in, m_gmlp_v_norm, m_gmlp_w_s, m_gmlp_b_s, m_gmlp_w_out,
                         m_kv_norm, m_w_kv, m_k_norm, m_attn_w_q, m_attn_q_norm, m_attn_w_o]))
    V = dict(zip(names, [v_ffn1_norm, v_ffn1_w_gate, v_ffn1_w_up, v_ffn1_w_down, v_mix_norm, v_ffn2_norm, v_ffn2_w_gate,
                         v_ffn2_w_up, v_ffn2_w_down, v_gmlp_w_in, v_gmlp_v_norm, v_gmlp_w_s, v_gmlp_b_s, v_gmlp_w_out,
                         v_kv_norm, v_w_kv, v_k_norm, v_attn_w_q, v_attn_q_norm, v_attn_w_o]))

    depth = ffn1_norm.shape[0]
    n_a = gmlp_w_in.shape[0]
    S, D = x.shape[1], x.shape[2]
    H = D // LANE
    n_grp = len(DILATIONS)
    hw = H * LANE
    xi, yi, ci = _me()
    core = jnp.reshape(ci, (1,)).astype(jnp.int32)
    chip = jnp.reshape(2 * xi + yi, (1,)).astype(jnp.int32)
    dev = 4 * xi + 2 * yi + ci
    q_scale = LANE ** -0.5
    ffn_names = (("f1", ("ffn1_norm", "ffn1_w_gate", "ffn1_w_up", "ffn1_w_down")),
                 ("f2", ("ffn2_norm", "ffn2_w_gate", "ffn2_w_up", "ffn2_w_down")))

    def gain(v):
        return v.reshape(1, -1)

    def head_gain(g3):
        return jnp.tile(g3[:, None, :], (1, H, 1)).reshape(1, n_grp * hw)

    gq = _Queue()
    gathered = {}
    gq.push("gather", "v_norm", jnp.pad(gmlp_v_norm, ((0, 8 - n_a), (0, 0))))
    for l in range(depth):
        for tag, (_, wgn, wun, wdn) in ffn_names:
            if tag == "f2":
                if l < n_a:
                    gq.push("gather", ("gmlp_w_in", l), gmlp_w_in[l].astype(BF))
                    gq.push("gather", ("gmlp_w_out", l), gmlp_w_out[l].astype(BF))
                else:
                    gq.push("gather", ("attn_w_q", l - n_a), attn_w_q[l - n_a].astype(BF))
                    gq.push("gather", ("attn_w_o", l - n_a), attn_w_o[l - n_a].astype(BF))
            for n in (wgn, wun, wdn):
                gq.push("gather", (n, l), W[n][l].astype(BF))
        if l == n_a - 1:
            gq.push("gather", "w_kv", w_kv.astype(BF))

    def land(items, outs):
        for (_, key, _), o in zip(items, outs):
            gathered[key] = o

    def need(*keys):
        items = gq.take_keys([k for k in keys if k not in gathered])
        if items:
            land(items, _comm_only(_Carried([(k, a) for k, _, a in items]), "allgather"))
        return [gathered[k] for k in keys]

    def carry(q, kind):
        items = q.take(BUDGET_US[kind])
        return items, _Carried([(k, a) for k, _, a in items])

    v_all = need("v_norm", *[(n, 0) for n in ffn_names[0][1][1:]])[0]
    v_gain_all = jnp.transpose(v_all[:, :n_a], (1, 0, 2)).reshape(n_a, -1)

    cur = x.reshape(S, D)
    saved = []
    k_sh = kv_raw = kv_hn = kv_x = k_gain = None

    for l in range(depth):
        rec = {}
        for tag, (nn, wgn, wun, wdn) in ffn_names:
            if tag == "f2":
                rec["mix_x"] = cur
                if l < n_a:
                    w_in, w_out = need(("gmlp_w_in", l), ("gmlp_w_out", l))
                    bias = jnp.repeat(gmlp_b_s[l].T, LANE, axis=1)
                    items, car = carry(gq, "normproj")
                    (hm, z), couts = _normproj(cur, gain(mix_norm[l]), w_in, "gmlp_in", car)
                    land(items, couts)
                    t = _gmlp_fwd(z, gain(v_gain_all[l]), gmlp_w_s[l], bias)
                    items, car = carry(gq, "rowproj")
                    cur, couts = _rowproj(cur, t, w_out, "proj_out", car)
                    land(items, couts)
                    rec.update(w_in=w_in, w_out=w_out, bias=bias, hm=hm, z=z, t=t)
                else:
                    jj = l - n_a
                    w_q, w_o = need(("attn_w_q", jj), ("attn_w_o", jj))
                    items, car = carry(gq, "normproj")
                    (hm, q_raw), couts = _normproj(cur, gain(mix_norm[l]), w_q, "attn_q", car)
                    land(items, couts)
                    qg = head_gain(attn_q_norm[jj])
                    q = _headnorm_fwd(q_raw, n_grp * hw, qg, q_scale, "headnorm_q")
                    outs = [_attn_fwd(q, k_sh, kv_raw, g, dil) for g, dil in enumerate(DILATIONS)]
                    o, lse = _attn_combine([o_ for o_, _ in outs], [l_ for _, l_ in outs])
                    items, car = carry(gq, "rowproj")
                    cur, couts = _rowproj(cur, o, w_o, "proj_out", car)
                    land(items, couts)
                    rec.update(w_q=w_q, w_o=w_o, hm=hm, q_raw=q_raw, qg=qg, q=q, o=o, lse=lse)
            wg, wu, wd = need((wgn, l), (wun, l), (wdn, l))
            rec[tag + "_x"] = cur
            items, car = carry(gq, "ffn_fwd")
            (cur, hn, a, b), couts = _ffn_fwd(cur, gain(W[nn][l]), wg, wu, wd, car)
            land(items, couts)
            rec[tag] = (wg, wu, wd, hn, a, b)
        if l == n_a - 1:
            (w_kv_g,) = need("w_kv")
            kv_x = cur
            items, car = carry(gq, "normproj")
            (kv_hn, kv_raw), couts = _normproj(cur, gain(kv_norm), w_kv_g, "kv_proj", car)
            land(items, couts)
            k_gain = head_gain(k_norm)
            k_sh = _headnorm_fwd(kv_raw, n_grp * hw, k_gain, 1.0, "headnorm_k")
        saved.append(rec)

    dcur, loss_part = _loss_head(cur, loss_target.reshape(S, D))
    loss = lax.psum(loss_part[0, 0], ("x", "y", "c"))

    rq = _Queue()
    reduced = {}
    small = {}

    def rs_land(items, outs):
        for (kind, key, arr), o in zip(items, outs):
            if kind == "core":
                rq.push("chip", key, _pair_add(arr, o, core))
            else:
                reduced[key] = (arr, o)

    def rs_carry(kind):
        items = rq.take(BUDGET_US[kind])
        return items, _Carried([(k, a) for k, _, a in items])

    def wgrad(key, a, a_kind, b, b_kind, name):
        items, car = rs_carry("wgrad")
        p, couts = _wgrad(a, a_kind, b, b_kind, name, car)
        rs_land(items, couts)
        rq.push("core", key, p)

    def put(name, l, val, n_layers):
        small.setdefault(name, [None] * n_layers)[l] = val

    dk_dv = None
    for l in reversed(range(depth)):
        rec = saved[l]
        if l == n_a - 1:
            dkv, dkg = _headnorm_bwd(dk_dv[0], kv_raw, k_gain, 1.0, "headnorm_k_bwd", tail=dk_dv[1])
            wgrad(("w_kv", 0), kv_hn, "full", dkv, "cols", "wgrad_kv")
            items, car = rs_carry("dgrad_col_norm")
            (dcur, dg), couts = _dgrad_col_norm(dcur, kv_x, gain(kv_norm), dkv, gathered["w_kv"], "dgrad_kv", car)
            rs_land(items, couts)
            small["kv_norm"] = [dg[0]]
            small["k_norm"] = [dkg[0].reshape(n_grp, H, LANE).sum(axis=1)]
        for tag, (nn, wgn, wun, wdn) in reversed(ffn_names):
            wg, wu, wd, hn, a, b = rec[tag]
            items, car = rs_carry("ffn_bwd_dx")
            (dcur, da, db, act, dyh, dg), couts = _ffn_bwd_dx(dcur, rec[tag + "_x"], gain(W[nn][l]), a, b, wg, wu, wd, car)
            rs_land(items, couts)
            wgrad((wgn, l), hn, "full", da, "stack", "wgrad_ffn_in")
            wgrad((wun, l), hn, "full", db, "stack", "wgrad_ffn_in")
            wgrad((wdn, l), act, "stack", dyh, "full", "wgrad_ffn_out")
            put(nn, l, dg[0], depth)
            if tag == "f2":
                mix_x = rec["mix_x"]
                if l < n_a:
                    dt = _dgrad_row(dcur, rec["w_out"], BF, "dgrad_gmlp_out")
                    wgrad(("gmlp_w_out", l), rec["t"], "cols", dcur, "full", "wgrad_proj_out")
                    dz, dws, dbias, dvg = _gmlp_bwd(rec["z"], dt, gain(v_gain_all[l]), gmlp_w_s[l], rec["bias"])
                    wgrad(("gmlp_w_in", l), rec["hm"], "full", dz, "cols", "wgrad_gmlp_in")
                    items, car = rs_carry("dgrad_col_norm")
                    (dcur, dg), couts = _dgrad_col_norm(dcur, mix_x, gain(mix_norm[l]), dz, rec["w_in"], "dgrad_gmlp_in", car)
                    rs_land(items, couts)
                    put("gmlp_w_s", l, dws, n_a)
                    put("gmlp_b_s", l, dbias[:, ::LANE].T, n_a)
                    put("gmlp_v_norm", l, dvg[0], n_a)
                else:
                    jj = l - n_a
                    do = _dgrad_row(dcur, rec["w_o"], F32, "dgrad_attn_out")
                    wgrad(("attn_w_o", jj), rec["o"], "cols", dcur, "full", "wgrad_proj_out")
                    dl = _attn_delta(do, rec["o"])
                    dq = None
                    first_layer = dk_dv is None
                    for g, dil in enumerate(DILATIONS):
                        dq = _attn_bwd_dq(rec["q"], k_sh, kv_raw, do, rec["lse"], dl, g, dil, dq)
                        dk_dv = _attn_bwd_dkv(rec["q"], k_sh, kv_raw, do, rec["lse"], dl, g, dil, dk_dv,
                                              accumulate=not first_layer)
                    dq_raw, dqg = _headnorm_bwd(dq, rec["q_raw"], rec["qg"], q_scale, "headnorm_q_bwd")
                    wgrad(("attn_w_q", jj), rec["hm"], "full", dq_raw, "cols", "wgrad_attn_q")
                    items, car = rs_carry("dgrad_col_norm")
                    (dcur, dg), couts = _dgrad_col_norm(dcur, mix_x, gain(mix_norm[l]), dq_raw, rec["w_q"], "dgrad_attn_q", car)
                    rs_land(items, couts)
                    put("attn_q_norm", jj, dqg[0].reshape(n_grp, H, LANE).sum(axis=1), depth - n_a)
                put("mix_norm", l, dg[0], depth)
    grad_x = dcur.reshape(1, S, D)

    for kind in ("core", "chip"):
        items = rq.take_kind(kind)
        if items:
            rs_land(items, _comm_only(_Carried([(k, a) for k, _, a in items]), "rs_%s_exchange" % kind))

    out_g, out_d, out_m, out_v = {}, {}, {}, {}
    for name in names:
        if (name, 0) not in reduced:
            continue
        as3 = (lambda t: t[None]) if W[name].ndim == 2 else (lambda t: t)
        res = None
        for l in range(as3(W[name]).shape[0]):
            qsum, r2 = reduced[(name, l)]
            res = _adamw_shard(qsum, r2, chip, as3(W[name]), as3(M[name]), as3(V[name]), l, res)
        for dct, val in zip((out_g, out_d, out_m, out_v), res):
            dct[name] = val[0] if W[name].ndim == 2 else val

    small_names = [n for n in names if n in small]
    full_shape = {n: (W[n].shape if n != "gmlp_v_norm" else (n_a, v_gain_all.shape[1])) for n in small_names}
    flat = jnp.concatenate([jnp.stack(small[n]).reshape(-1) if W[n].ndim > 1 else small[n][0].reshape(-1)
                            for n in small_names])
    n_flat = flat.shape[0]
    rows = -(-n_flat // (8 * LANE)) * 8

    def pack(parts_list):
        v = jnp.concatenate([p.reshape(-1) for p in parts_list])
        return jnp.pad(v, (0, rows * LANE - n_flat)).reshape(rows, LANE)

    def full_of(dct, n, fill):
        if n != "gmlp_v_norm":
            return dct[n]
        sh = dct[n].shape[1]
        return lax.dynamic_update_slice(jnp.full(full_shape[n], fill, F32), dct[n], (0, dev * sh))

    (g_all,) = _comm_only(_Carried([("gather", pack([flat]))]), "allgather_small_grads")
    w_p = pack([full_of(W, n, 0.0) for n in small_names])
    m_p = pack([full_of(M, n, 0.0) for n in small_names])
    v_p = pack([full_of(V, n, 1.0) for n in small_names])
    packed = _adamw_replicated(g_all, w_p, m_p, v_p)
    offs = 0
    for n in small_names:
        size = math.prod(full_shape[n])
        for dct, arr in zip((out_g, out_d, out_m, out_v), packed):
            val = arr.reshape(-1)[offs:offs + size].reshape(full_shape[n])
            if n == "gmlp_v_norm":
                sh = W[n].shape[1]
                val = lax.dynamic_slice(val, (0, dev * sh), (n_a, sh))
            dct[n] = val
        offs += size

    return (loss, grad_x, *[out_g[n] for n in names], *[out_d[n] for n in names],
            *[out_m[n] for n in names], *[out_v[n] for n in names])
```

```python
import math

import jax
import jax.numpy as jnp
from jax import lax
from jax.experimental import pallas as pl
from jax.experimental.pallas import tpu as pltpu

F32 = jnp.float32
BF = jnp.bfloat16
N_DEV = 8
N_CHIP = 4
LANE = 128
ATT_WIN = 16 * LANE
EPS = 1e-6
NEG = -1e30
DILATIONS = (1, 4, 16)
ADAM_LR, ADAM_B1, ADAM_B2, ADAM_EPS, ADAM_WD, ADAM_STEP = 0.001, 0.9, 0.999, 1e-08, 0.01, 10
GELU_C0, GELU_C1 = 0.7978845608028654, 0.044715
VMEM_MB = 2 ** 20
BUDGET_US = {"ffn_fwd": 380.0, "normproj": 150.0, "rowproj": 100.0, "gmlp_fwd": 50.0, "headnorm": 75.0,
             "attn_fwd": 120.0, "attn_combine": 75.0,
             "ffn_bwd_dx": 300.0, "wgrad": 90.0, "dgrad_col_norm": 150.0}
COST_US_PER_ELEM = {"gather": 1.0e-4, "core": 1.4e-5, "chip": 9.0e-5}

MESH_T = pl.DeviceIdType.MESH
ANY = pl.BlockSpec(memory_space=pl.ANY)
DMA_SEM = pltpu.SemaphoreType.DMA
NT = (((1,), (1,)), ((), ()))
TN = (((0,), (0,)), ((), ()))


def _tile(n, target, mult):
    best = None
    for t in range(mult, min(n, target) + 1, mult):
        if n % t == 0:
            best = t
    if best is None:
        best = n
    return best


def _dot(a, b, dims=None):
    if dims is None:
        return jnp.dot(a, b, preferred_element_type=F32)
    return lax.dot_general(a, b, dims, preferred_element_type=F32)


def _rms_hat(xv):
    r = lax.rsqrt(jnp.mean(xv * xv, axis=-1, keepdims=True) + EPS)
    return xv * r, r


def _rms_bwd(dhn, xv, gain, dres):
    xhat, r = _rms_hat(xv)
    dxhat = dhn * gain
    dx = dres + r * (dxhat - xhat * jnp.mean(dxhat * xhat, axis=-1, keepdims=True))
    return dx, jnp.sum(dhn * xhat, axis=0, keepdims=True)


def _accum_rows(ref, row, first):
    val = jnp.broadcast_to(row, ref.shape)

    @pl.when(first)
    def _():
        ref[...] = val

    @pl.when(jnp.logical_not(first))
    def _():
        ref[...] += val


def _gelu(z):
    t = jnp.tanh(GELU_C0 * (z + GELU_C1 * z * z * z))
    return 0.5 * z * (1.0 + t), t


def _gelu_grad(z, t):
    return 0.5 * (1.0 + t) + 0.5 * z * (1.0 - t * t) * GELU_C0 * (1.0 + 3.0 * GELU_C1 * z * z)


def _me():
    return lax.axis_index("x"), lax.axis_index("y"), lax.axis_index("c")


def _gather_phase(phase, x_ref, out_ref, send_sems, recv_sems, local_sem):
    x, y, c = _me()
    me, sibling = (x, y, c), (x, y, 1 - c)
    chips = [(1 - x, y), (x, 1 - y), (1 - x, 1 - y)]

    def slot(px, py, pc):
        return out_ref.at[4 * px + 2 * py + pc]

    def copy(k, block, to, src=None):
        return pltpu.make_async_remote_copy(
            src_ref=slot(*block) if src is None else src, dst_ref=slot(*block),
            send_sem=send_sems.at[k], recv_sem=recv_sems.at[k], device_id=to, device_id_type=MESH_T)

    mine = pltpu.make_async_copy(x_ref, slot(*me), local_sem)
    first = [copy(0, me, sibling, src=x_ref)]
    first += [copy(1 + j, me, (*chip, c), src=x_ref) for j, chip in enumerate(chips)]
    passed = [copy(4 + j, (*chip, c), sibling) for j, chip in enumerate(chips)]
    if phase == 0:
        mine.start()
        for cp in first:
            cp.start()
    elif phase == 1:
        for j, chip in enumerate(chips):
            copy(1 + j, (*chip, c), me).wait_recv()
            passed[j].start()
    else:
        copy(0, sibling, me).wait_recv()
        for j, chip in enumerate(chips):
            copy(4 + j, (*chip, 1 - c), me).wait_recv()
        for cp in first + passed:
            cp.wait_send()
        mine.wait()


def _exchange_phase(phase, kind, src_ref, dst_ref, send_sems, recv_sems):
    x, y, c = _me()
    if kind == "core":
        plan = [(2 * k + (1 - c), k, (x, y, 1 - c)) for k in range(N_CHIP)]
    else:
        plan = [(2 * px + py, t, (px, py, c)) for t, (px, py) in enumerate([(1 - x, y), (x, 1 - y), (1 - x, 1 - y)])]
    cps = [pltpu.make_async_remote_copy(
        src_ref=src_ref.at[s], dst_ref=dst_ref.at[d], send_sem=send_sems.at[i], recv_sem=recv_sems.at[i],
        device_id=to, device_id_type=MESH_T) for i, (s, d, to) in enumerate(plan)]
    if phase == 0:
        for cp in cps:
            cp.start()
    elif phase == 2:
        for cp in cps:
            cp.wait()


_N_COPIES = {"gather": 7, "core": N_CHIP, "chip": 3}


class _Carried:
    def __init__(self, items=()):
        self.items = list(items)

    def arrays(self):
        return [a for _, a in self.items]

    def out_shapes(self):
        lead = {"gather": lambda a: (N_DEV,) + a.shape, "core": lambda a: (N_CHIP,) + a.shape[1:],
                "chip": lambda a: (3,) + a.shape[1:]}
        return [jax.ShapeDtypeStruct(lead[k](a), a.dtype) for k, a in self.items]

    def scratch(self):
        res = []
        for k, _ in self.items:
            res += [DMA_SEM((_N_COPIES[k],)), DMA_SEM((_N_COPIES[k],))]
            if k == "gather":
                res.append(DMA_SEM(()))
        return res

    def emit(self, phase, in_refs, out_refs, scr):
        i = 0
        for (kind, _), src, dst in zip(self.items, in_refs, out_refs):
            if kind == "gather":
                _gather_phase(phase, src, dst, scr[i], scr[i + 1], scr[i + 2])
                i += 3
            else:
                _exchange_phase(phase, kind, src, dst, scr[i], scr[i + 1])
                i += 2


def _comm_only(carried, name):
    nc = len(carried.items)

    def body(*refs):
        for phase in range(3):
            carried.emit(phase, refs[:nc], refs[nc:2 * nc], refs[2 * nc:])

    return pl.pallas_call(
        body, name=name, out_shape=carried.out_shapes(), in_specs=[ANY] * nc, out_specs=[ANY] * nc,
        scratch_shapes=carried.scratch(),
    )(*carried.arrays())


def _pcall(main, *, name, grid, in_specs, out_specs, out_shape, args, scratch=(), sem=None, vmem=48,
           carried=None, aliases=None):
    params = pltpu.CompilerParams(dimension_semantics=sem, vmem_limit_bytes=vmem * VMEM_MB)
    n_in, n_out, n_scr = len(in_specs), len(out_specs), len(scratch)
    if carried is None or not carried.items:
        outs = pl.pallas_call(
            main, name=name, grid=grid, in_specs=in_specs, out_specs=out_specs, out_shape=out_shape,
            scratch_shapes=list(scratch), compiler_params=params, input_output_aliases=aliases or {},
        )(*args)
        return list(outs), []
    nc = len(carried.items)
    total = math.prod(grid)

    def body(*refs):
        ins, cin = refs[:n_in], refs[n_in:n_in + nc]
        o0 = n_in + nc
        outs, cout = refs[o0:o0 + n_out], refs[o0 + n_out:o0 + n_out + nc]
        s0 = o0 + n_out + nc
        scr, cscr = refs[s0:s0 + n_scr], refs[s0 + n_scr:]
        step = 0
        for d, n in enumerate(grid):
            step = step * n + pl.program_id(d)

        @pl.when(step == 0)
        def _():
            carried.emit(0, cin, cout, cscr)

        main(*ins, *outs, *scr)

        @pl.when(step == max(total - 2, 0))
        def _():
            carried.emit(1, cin, cout, cscr)

        @pl.when(step == total - 1)
        def _():
            carried.emit(2, cin, cout, cscr)

    outs = pl.pallas_call(
        body, name=name, grid=grid, in_specs=list(in_specs) + [ANY] * nc, out_specs=list(out_specs) + [ANY] * nc,
        out_shape=list(out_shape) + carried.out_shapes(), scratch_shapes=list(scratch) + carried.scratch(),
        compiler_params=params, input_output_aliases=aliases or {},
    )(*args, *carried.arrays())
    return list(outs[:n_out]), list(outs[n_out:])


def _pair_add(p, r1, core):
    _, rows, cols = p.shape
    tr = _tile(rows, 512, 16)
    p4 = p.reshape(N_CHIP, 2, rows, cols)

    def body(core_ref, p_ref, r_ref, q_ref):
        q_ref[...] = (p_ref[...].astype(F32) + r_ref[...].astype(F32)).astype(BF)

    grid_spec = pltpu.PrefetchScalarGridSpec(
        num_scalar_prefetch=1, grid=(N_CHIP, rows // tr),
        in_specs=[pl.BlockSpec((None, None, tr, cols), lambda k, i, cr: (k, cr[0], i, 0)),
                  pl.BlockSpec((None, tr, cols), lambda k, i, cr: (k, i, 0))],
        out_specs=pl.BlockSpec((None, tr, cols), lambda k, i, cr: (k, i, 0)))
    return pl.pallas_call(
        body, name="pair_add", grid_spec=grid_spec, out_shape=jax.ShapeDtypeStruct((N_CHIP, rows, cols), BF),
        compiler_params=pltpu.CompilerParams(dimension_semantics=("parallel", "parallel")),
    )(core, p4, r1)


def _adam_math(g, w, m, v):
    m2 = ADAM_B1 * m + (1.0 - ADAM_B1) * g
    v2 = ADAM_B2 * v + (1.0 - ADAM_B2) * (g * g)
    m_hat = m2 / (1.0 - ADAM_B1 ** ADAM_STEP)
    v_hat = v2 / (1.0 - ADAM_B2 ** ADAM_STEP)
    delta = -ADAM_LR * (m_hat / (jnp.sqrt(v_hat) + ADAM_EPS) + ADAM_WD * w)
    return delta, m2, v2


def _adamw_shard(q, r2, chip, w, m, v, layer, prev):
    n_layers, rows, cols = w.shape
    tr = _tile(rows, 256, 16)

    def body(chip_ref, q_ref, r_ref, w_ref, m_ref, v_ref, *rest):
        g_ref, d_ref, m2_ref, v2_ref = rest[-4:]
        g = q_ref[...].astype(F32) + r_ref[0].astype(F32) + r_ref[1].astype(F32) + r_ref[2].astype(F32)
        d, m2, v2 = _adam_math(g, w_ref[...], m_ref[...], v_ref[...])
        g_ref[...] = g
        d_ref[...] = d
        m2_ref[...] = m2
        v2_ref[...] = v2

    blk = pl.BlockSpec((None, tr, cols), lambda i, cr: (layer, i, 0))
    in_specs = [pl.BlockSpec((None, tr, cols), lambda i, cr: (cr[0], i, 0)),
                pl.BlockSpec((3, tr, cols), lambda i, cr: (0, i, 0)), blk, blk, blk]
    args = [chip, q, r2, w, m, v]
    aliases = {}
    if prev is not None:
        in_specs += [ANY] * 4
        args += list(prev)
        aliases = {6 + i: i for i in range(4)}
    grid_spec = pltpu.PrefetchScalarGridSpec(
        num_scalar_prefetch=1, grid=(rows // tr,), in_specs=in_specs, out_specs=[blk, blk, blk, blk])
    out = jax.ShapeDtypeStruct((n_layers, rows, cols), F32)
    return pl.pallas_call(
        body, name="adamw_shard", grid_spec=grid_spec, out_shape=[out, out, out, out], input_output_aliases=aliases,
        compiler_params=pltpu.CompilerParams(dimension_semantics=("parallel",)),
    )(*args)


def _adamw_replicated(parts, w, m, v):
    rows, cols = w.shape
    tr = _tile(rows, 512, 8)

    def body(p_ref, w_ref, m_ref, v_ref, g_ref, d_ref, m2_ref, v2_ref):
        g = p_ref[0]
        for k in range(1, N_DEV):
            g = g + p_ref[k]
        d, m2, v2 = _adam_math(g, w_ref[...], m_ref[...], v_ref[...])
        g_ref[...] = g
        d_ref[...] = d
        m2_ref[...] = m2
        v2_ref[...] = v2

    blk = pl.BlockSpec((tr, cols), lambda i: (i, 0))
    out = jax.ShapeDtypeStruct((rows, cols), F32)
    outs, _ = _pcall(body, name="adamw_replicated", grid=(rows // tr,),
                     in_specs=[pl.BlockSpec((N_DEV, tr, cols), lambda i: (0, i, 0)), blk, blk, blk],
                     out_specs=[blk, blk, blk, blk], out_shape=[out, out, out, out], sem=("parallel",),
                     args=(parts, w, m, v))
    return outs


def _ffn_fwd(x, gain, wg, wu, wd, carried=None):
    S, D = x.shape
    nsh, _, fs = wg.shape
    tm = _tile(S, 512, 16)

    def body(x_ref, g_ref, wg_ref, wu_ref, wd_ref, y_ref, hn_ref, act_ref, ga_ref, gb_ref, acc_ref):
        j = pl.program_id(1)

        @pl.when(j == 0)
        def _():
            xhat, _ = _rms_hat(x_ref[...])
            hn_ref[...] = (xhat * g_ref[...]).astype(BF)
            acc_ref[...] = jnp.zeros_like(acc_ref)

        hn = hn_ref[...]
        a = _dot(hn, wg_ref[...])
        b = _dot(hn, wu_ref[...])
        sg = jax.nn.sigmoid(a)
        sil = a * sg
        act = (sil * b).astype(BF)
        act_ref[...] = act
        ga_ref[...] = (b * (sg * (1.0 + a * (1.0 - sg)))).astype(BF)
        gb_ref[...] = sil.astype(BF)
        acc_ref[...] += _dot(act, wd_ref[...])

        @pl.when(j == nsh - 1)
        def _():
            y_ref[...] = x_ref[...] + 0.5 * acc_ref[...]

    row = pl.BlockSpec((tm, D), lambda m, j: (m, 0))
    hid = pl.BlockSpec((None, tm, fs), lambda m, j: (j, m, 0))
    return _pcall(
        body, name="ffn_fwd", grid=(S // tm, nsh),
        in_specs=[row, pl.BlockSpec((1, D), lambda m, j: (0, 0)),
                  pl.BlockSpec((None, D, fs), lambda m, j: (j, 0, 0)),
                  pl.BlockSpec((None, D, fs), lambda m, j: (j, 0, 0)),
                  pl.BlockSpec((None, fs, D), lambda m, j: (j, 0, 0))],
        out_specs=[row, row, hid, hid, hid],
        out_shape=[jax.ShapeDtypeStruct((S, D), F32), jax.ShapeDtypeStruct((S, D), BF)]
        + [jax.ShapeDtypeStruct((nsh, S, fs), BF)] * 3,
        scratch=[pltpu.VMEM((tm, D), F32)], sem=("parallel", "arbitrary"), vmem=58,
        args=(x, gain, wg, wu, wd), carried=carried)


def _ffn_bwd_dx(dy, x, gain, ga, gb, wg, wu, wd, carried=None):
    S, D = x.shape
    nsh, _, fs = wg.shape
    tm = _tile(S, 512, 16)

    def body(dy_ref, x_ref, g_ref, ga_ref, gb_ref, wg_ref, wu_ref, wd_ref,
             dx_ref, da_ref, db_ref, dyh_ref, dg_ref, acc_ref):
        m, j = pl.program_id(0), pl.program_id(1)

        @pl.when(j == 0)
        def _():
            dyh_ref[...] = (0.5 * dy_ref[...]).astype(BF)
            acc_ref[...] = jnp.zeros_like(acc_ref)

        dact = _dot(dyh_ref[...], wd_ref[...], NT)
        da = (dact * ga_ref[...].astype(F32)).astype(BF)
        db = (dact * gb_ref[...].astype(F32)).astype(BF)
        da_ref[...] = da
        db_ref[...] = db
        acc_ref[...] += _dot(da, wg_ref[...], NT)
        acc_ref[...] += _dot(db, wu_ref[...], NT)

        @pl.when(j == nsh - 1)
        def _():
            dx, dgain = _rms_bwd(acc_ref[...], x_ref[...], g_ref[...], dy_ref[...])
            dx_ref[...] = dx
            _accum_rows(dg_ref, dgain, m == 0)

    row = pl.BlockSpec((tm, D), lambda m, j: (m, 0))
    row1 = pl.BlockSpec((tm, D), lambda m, j: (m, 0), pipeline_mode=pl.Buffered(1))
    hid = pl.BlockSpec((None, tm, fs), lambda m, j: (j, m, 0))
    hshape = jax.ShapeDtypeStruct((nsh, S, fs), BF)
    return _pcall(
        body, name="ffn_bwd_dx", grid=(S // tm, nsh),
        in_specs=[row1, row1, pl.BlockSpec((1, D), lambda m, j: (0, 0)), hid, hid,
                  pl.BlockSpec((None, D, fs), lambda m, j: (j, 0, 0)),
                  pl.BlockSpec((None, D, fs), lambda m, j: (j, 0, 0)),
                  pl.BlockSpec((None, fs, D), lambda m, j: (j, 0, 0))],
        out_specs=[row1, hid, hid, row1, pl.BlockSpec((8, D), lambda m, j: (0, 0))],
        out_shape=[jax.ShapeDtypeStruct((S, D), F32), hshape, hshape,
                   jax.ShapeDtypeStruct((S, D), BF), jax.ShapeDtypeStruct((8, D), F32)],
        scratch=[pltpu.VMEM((tm, D), F32)], sem=("arbitrary", "arbitrary"), vmem=60,
        args=(dy, x, gain, ga, gb, wg, wu, wd), carried=carried)


def _opspec(arr, kind, tm, grid_mj):
    if kind == "full":
        return pl.BlockSpec((tm, arr.shape[1]), lambda *g: (grid_mj(*g)[0], 0)), arr.shape[1]
    if kind == "cols":
        n = arr.shape[1] // N_DEV
        return pl.BlockSpec((tm, n), lambda *g: grid_mj(*g)), n
    n = arr.shape[2]
    return pl.BlockSpec((None, tm, n), lambda *g: (grid_mj(*g)[1], grid_mj(*g)[0], 0)), n


def _wgrad(a, a_kind, b, b_kind, name, carried=None):
    S = a.shape[0] if a_kind != "stack" else a.shape[1]
    tm = _tile(S, 512, 16)
    mj = lambda j, m: (m, j)
    a_spec, ka = _opspec(a, a_kind, tm, mj)
    b_spec, nb = _opspec(b, b_kind, tm, mj)
    n_m = S // tm

    def body(a_ref, b_ref, o_ref, acc_ref):
        m = pl.program_id(1)

        @pl.when(m == 0)
        def _():
            acc_ref[...] = jnp.zeros_like(acc_ref)

        acc_ref[...] += _dot(a_ref[...].astype(BF), b_ref[...].astype(BF), TN)

        @pl.when(m == n_m - 1)
        def _():
            o_ref[...] = acc_ref[...].astype(BF)

    outs, cout = _pcall(
        body, name=name, grid=(N_DEV, n_m), in_specs=[a_spec, b_spec],
        out_specs=[pl.BlockSpec((None, ka, nb), lambda j, m: (j, 0, 0))],
        out_shape=[jax.ShapeDtypeStruct((N_DEV, ka, nb), BF)],
        scratch=[pltpu.VMEM((ka, nb), F32)], sem=("parallel", "arbitrary"), args=(a, b), carried=carried)
    return outs[0], cout


def _normproj(x, gain, w, name, carried=None):
    S, D = x.shape
    _, _, n = w.shape
    tm = _tile(S, 512, 16)

    def body(x_ref, g_ref, w_ref, hn_ref, y_ref):
        @pl.when(pl.program_id(1) == 0)
        def _():
            xhat, _ = _rms_hat(x_ref[...])
            hn_ref[...] = (xhat * g_ref[...]).astype(BF)

        y_ref[...] = _dot(hn_ref[...], w_ref[...])

    row = pl.BlockSpec((tm, D), lambda m, j: (m, 0))
    return _pcall(
        body, name=name, grid=(S // tm, N_DEV),
        in_specs=[row, pl.BlockSpec((1, D), lambda m, j: (0, 0)), pl.BlockSpec((None, D, n), lambda m, j: (j, 0, 0))],
        out_specs=[row, pl.BlockSpec((tm, n), lambda m, j: (m, j))],
        out_shape=[jax.ShapeDtypeStruct((S, D), BF), jax.ShapeDtypeStruct((S, N_DEV * n), F32)],
        sem=("parallel", "arbitrary"), args=(x, gain, w), carried=carried)


def _rowproj(x, t, w, name, carried=None):
    S, D = x.shape
    _, k, _ = w.shape
    tm = _tile(S, 512, 16)

    def body(x_ref, t_ref, w_ref, y_ref):
        j = pl.program_id(1)
        part = _dot(t_ref[...], w_ref[...])

        @pl.when(j == 0)
        def _():
            y_ref[...] = x_ref[...] + part

        @pl.when(j > 0)
        def _():
            y_ref[...] += part

    row = pl.BlockSpec((tm, D), lambda m, j: (m, 0))
    outs, cout = _pcall(
        body, name=name, grid=(S // tm, N_DEV),
        in_specs=[row, pl.BlockSpec((tm, k), lambda m, j: (m, j)), pl.BlockSpec((None, k, D), lambda m, j: (j, 0, 0))],
        out_specs=[row], out_shape=[jax.ShapeDtypeStruct((S, D), F32)],
        sem=("parallel", "arbitrary"), args=(x, t, w), carried=carried)
    return outs[0], cout


def _dgrad_row(dy, w, out_dtype, name):
    S, D = dy.shape
    _, k, _ = w.shape
    tm = _tile(S, 512, 16)

    def body(dy_ref, w_ref, dt_ref, dyb_ref):
        @pl.when(pl.program_id(1) == 0)
        def _():
            dyb_ref[...] = dy_ref[...].astype(BF)

        dt_ref[...] = _dot(dyb_ref[...], w_ref[...], NT).astype(out_dtype)

    outs, _ = _pcall(
        body, name=name, grid=(S // tm, N_DEV),
        in_specs=[pl.BlockSpec((tm, D), lambda m, j: (m, 0)), pl.BlockSpec((None, k, D), lambda m, j: (j, 0, 0))],
        out_specs=[pl.BlockSpec((tm, k), lambda m, j: (m, j))],
        out_shape=[jax.ShapeDtypeStruct((S, N_DEV * k), out_dtype)],
        scratch=[pltpu.VMEM((tm, D), BF)], sem=("parallel", "arbitrary"), args=(dy, w))
    return outs[0]


def _dgrad_col_norm(dres, x, gain, dz, w, name, carried=None):
    S, D = x.shape
    _, _, n = w.shape
    tm = _tile(S, 256, 16)

    def body(dres_ref, x_ref, g_ref, dz_ref, w_ref, dx_ref, dg_ref, acc_ref):
        m, j = pl.program_id(0), pl.program_id(1)

        @pl.when(j == 0)
        def _():
            acc_ref[...] = jnp.zeros_like(acc_ref)

        acc_ref[...] += _dot(dz_ref[...], w_ref[...], NT)

        @pl.when(j == N_DEV - 1)
        def _():
            dx, dgain = _rms_bwd(acc_ref[...], x_ref[...], g_ref[...], dres_ref[...])
            dx_ref[...] = dx
            _accum_rows(dg_ref, dgain, m == 0)

    row = pl.BlockSpec((tm, D), lambda m, j: (m, 0))
    return _pcall(
        body, name=name, grid=(S // tm, N_DEV),
        in_specs=[row, row, pl.BlockSpec((1, D), lambda m, j: (0, 0)), pl.BlockSpec((tm, n), lambda m, j: (m, j)),
                  pl.BlockSpec((None, D, n), lambda m, j: (j, 0, 0))],
        out_specs=[row, pl.BlockSpec((8, D), lambda m, j: (0, 0))],
        out_shape=[jax.ShapeDtypeStruct((S, D), F32), jax.ShapeDtypeStruct((8, D), F32)],
        scratch=[pltpu.VMEM((tm, D), F32)], sem=("arbitrary", "arbitrary"),
        args=(dres, x, gain, dz, w), carried=carried)


def _headnorm_fwd(xa, width, gain_row, scale, name, carried=None):
    S = xa.shape[0]
    cb = _tile(width, 1024, LANE)
    tm = _tile(S, 512, 8)

    def body(x_ref, g_ref, y_ref):
        for c in range(cb // LANE):
            sl = slice(c * LANE, (c + 1) * LANE)
            xhat, _ = _rms_hat(x_ref[:, sl])
            y_ref[:, sl] = xhat * (g_ref[:, sl] * scale)

    outs, couts = _pcall(
        body, name=name, grid=(S // tm, width // cb),
        in_specs=[pl.BlockSpec((tm, cb), lambda m, c: (m, c)), pl.BlockSpec((1, cb), lambda m, c: (0, c))],
        out_specs=[pl.BlockSpec((tm, cb), lambda m, c: (m, c))],
        out_shape=[jax.ShapeDtypeStruct((S, width), F32)], sem=("parallel", "parallel"), args=(xa, gain_row),
        carried=carried)
    return outs[0], couts


def _headnorm_bwd(dy, xa, gain_row, scale, name, tail=None):
    S, width = dy.shape
    cb = _tile(width, 1024, LANE)
    tm = _tile(S, 512, 16)
    ncb = width // cb
    ntail = 0 if tail is None else tail.shape[1] // cb

    def body(dy_ref, x_ref, g_ref, *rest):
        dx_ref, dg_ref = rest[-2:]
        c, m = pl.program_id(0), pl.program_id(1)

        @pl.when(c < ncb)
        def _():
            rows = []
            for i in range(cb // LANE):
                sl = slice(i * LANE, (i + 1) * LANE)
                dx, dgain = _rms_bwd(dy_ref[:, sl] * scale, x_ref[:, sl], g_ref[:, sl], 0.0)
                dx_ref[:, sl] = dx.astype(BF)
                rows.append(dgain)
            _accum_rows(dg_ref, jnp.concatenate(rows, axis=1), m == 0)

        if tail is not None:
            @pl.when(c >= ncb)
            def _():
                dx_ref[...] = rest[0][...].astype(BF)

    head = lambda c: jnp.minimum(c, ncb - 1)
    in_specs = [pl.BlockSpec((tm, cb), lambda c, m: (jnp.where(c < ncb, m, 0), head(c))),
                pl.BlockSpec((tm, cb), lambda c, m: (jnp.where(c < ncb, m, 0), head(c))),
                pl.BlockSpec((1, cb), lambda c, m: (0, head(c)))]
    args = [dy, xa, gain_row]
    if tail is not None:
        in_specs.append(pl.BlockSpec((tm, cb), lambda c, m: (jnp.where(c >= ncb, m, 0), jnp.maximum(c - ncb, 0))))
        args.append(tail)
    outs, _ = _pcall(
        body, name=name, grid=(ncb + ntail, S // tm), in_specs=in_specs,
        out_specs=[pl.BlockSpec((tm, cb), lambda c, m: (m, c)), pl.BlockSpec((8, cb), lambda c, m: (0, head(c)))],
        out_shape=[jax.ShapeDtypeStruct((S, width + ntail * cb), BF), jax.ShapeDtypeStruct((8, width), F32)],
        sem=("arbitrary", "arbitrary"), args=args)
    return outs


def _causal():
    p = lax.broadcasted_iota(jnp.int32, (LANE, LANE), 0)
    q = lax.broadcasted_iota(jnp.int32, (LANE, LANE), 1)
    return p >= q


def _gmlp_fwd(z, v_gain, ws, bias, carried=None):
    S, dg2 = z.shape
    dg = dg2 // 2
    G = dg // LANE

    def body(z_ref, vg_ref, ws_ref, bias_ref, t_ref):
        u, _ = _gelu(z_ref[:, :dg])
        v, _ = _gelu(z_ref[:, dg:])
        vhat, _ = _rms_hat(v)
        vn = (vhat * vg_ref[...]).astype(BF)
        mask = _causal()
        for g in range(G):
            sl = slice(g * LANE, (g + 1) * LANE)
            wm = jnp.where(mask, ws_ref[g], 0.0).astype(BF)
            sv = _dot(wm, vn[:, sl]) + bias_ref[:, sl]
            t_ref[:, sl] = (u[:, sl] * sv).astype(BF)

    outs, couts = _pcall(
        body, name="gmlp_fwd", grid=(S // LANE,),
        in_specs=[pl.BlockSpec((LANE, dg2), lambda n: (n, 0)), pl.BlockSpec((1, dg), lambda n: (0, 0)),
                  pl.BlockSpec((G, LANE, LANE), lambda n: (0, 0, 0)), pl.BlockSpec((LANE, dg), lambda n: (0, 0))],
        out_specs=[pl.BlockSpec((LANE, dg), lambda n: (n, 0))],
        out_shape=[jax.ShapeDtypeStruct((S, dg), BF)], sem=("parallel",), args=(z, v_gain, ws, bias),
        carried=carried)
    return outs[0], couts


def _gmlp_bwd(z, dt, v_gain, ws, bias):
    S, dg2 = z.shape
    dg = dg2 // 2
    G = dg // LANE

    def body(z_ref, dt_ref, vg_ref, ws_ref, bias_ref, dz_ref, dws_ref, db_ref, dvg_ref, dvn_ref):
        n = pl.program_id(0)
        zu, zv = z_ref[:, :dg], z_ref[:, dg:]
        u, tu = _gelu(zu)
        v, tv = _gelu(zv)
        vhat, r = _rms_hat(v)
        vn = (vhat * vg_ref[...]).astype(BF)
        mask = _causal()

        @pl.when(n == 0)
        def _():
            dws_ref[...] = jnp.zeros_like(dws_ref)
            db_ref[...] = jnp.zeros_like(db_ref)

        for g in range(G):
            sl = slice(g * LANE, (g + 1) * LANE)
            wm = jnp.where(mask, ws_ref[g], 0.0).astype(BF)
            sv = _dot(wm, vn[:, sl]) + bias_ref[:, sl]
            dtg = dt_ref[:, sl].astype(F32)
            dz_ref[:, sl] = (dtg * sv * _gelu_grad(zu[:, sl], tu[:, sl])).astype(BF)
            dsv = dtg * u[:, sl]
            dsvb = dsv.astype(BF)
            dvn_ref[:, sl] = _dot(wm, dsvb, TN)
            dws_ref[g] += jnp.where(mask, _dot(dsvb, vn[:, sl], NT), 0.0)
            db_ref[:, sl] += jnp.broadcast_to(jnp.sum(dsv, axis=1, keepdims=True), (LANE, LANE))

        dvn = dvn_ref[...]
        dxhat = dvn * vg_ref[...]
        dv = r * (dxhat - vhat * jnp.mean(dxhat * vhat, axis=-1, keepdims=True))
        dz_ref[:, dg:] = (dv * _gelu_grad(zv, tv)).astype(BF)
        _accum_rows(dvg_ref, jnp.sum(dvn * vhat, axis=0, keepdims=True), n == 0)

    outs, _ = _pcall(
        body, name="gmlp_bwd", grid=(S // LANE,),
        in_specs=[pl.BlockSpec((LANE, dg2), lambda n: (n, 0)), pl.BlockSpec((LANE, dg), lambda n: (n, 0)),
                  pl.BlockSpec((1, dg), lambda n: (0, 0)), pl.BlockSpec((G, LANE, LANE), lambda n: (0, 0, 0)),
                  pl.BlockSpec((LANE, dg), lambda n: (0, 0))],
        out_specs=[pl.BlockSpec((LANE, dg2), lambda n: (n, 0)), pl.BlockSpec((G, LANE, LANE), lambda n: (0, 0, 0)),
                   pl.BlockSpec((LANE, dg), lambda n: (0, 0)), pl.BlockSpec((8, dg), lambda n: (0, 0))],
        out_shape=[jax.ShapeDtypeStruct((S, dg2), BF), jax.ShapeDtypeStruct((G, LANE, LANE), F32),
                   jax.ShapeDtypeStruct((LANE, dg), F32), jax.ShapeDtypeStruct((8, dg), F32)],
        scratch=[pltpu.VMEM((LANE, dg), F32)], sem=("arbitrary",), args=(z, dt, v_gain, ws, bias))
    return outs


N_ENT = ATT_WIN // LANE


def _rows(ref, start, dil):
    return ref[pl.ds(start, LANE), :] if dil == 1 else ref[pl.ds(start, LANE, stride=dil), :]


def _rows_store(ref, start, dil, val):
    if dil == 1:
        ref[pl.ds(start, LANE), :] = val
    else:
        ref[pl.ds(start, LANE, stride=dil), :] = val


def _slope_times_dil(h, n_heads, dil, shape):
    hv = jnp.zeros(shape, F32) + (h + 1).astype(F32)
    return jnp.exp(hv * (-8.0 / n_heads * math.log(2.0))) * float(dil)


def _band_bias(h, n_heads, dil, has_prev):
    qi = lax.broadcasted_iota(jnp.int32, (LANE, 2 * LANE), 0)
    kj = lax.broadcasted_iota(jnp.int32, (LANE, 2 * LANE), 1)
    delta = qi + LANE - kj
    valid = (delta >= 0) & (delta <= LANE) & ((kj >= LANE) | has_prev)
    return jnp.where(valid, -_slope_times_dil(h, n_heads, dil, (LANE, 2 * LANE)) * delta.astype(F32), NEG)


def _attn_geom(S, H, g, dil):
    pb = LANE * dil
    nblk = ATT_WIN // pb
    nw = S // ATT_WIN
    win = lambda c0: pl.BlockSpec((ATT_WIN, LANE), lambda h, w: (w, c0 + h))
    prev = lambda c0: pl.BlockSpec((pb, LANE), lambda h, w: (jnp.maximum(w * nblk - 1, 0), c0 + h))
    nxt = lambda c0: pl.BlockSpec((pb, LANE), lambda h, w: (jnp.minimum((w + 1) * nblk, nw * nblk - 1), c0 + h))
    return pb, nblk, nw, g * H, win, prev, nxt


def _stage_band(dst, cur_ref, prev_ref, dil, pb, nblk):
    for blk in range(nblk):
        for r in range(dil):
            e = blk * dil + r
            dst[e, :LANE] = _rows(prev_ref, r, dil) if blk == 0 else _rows(cur_ref, (blk - 1) * pb + r, dil)
            dst[e, LANE:] = _rows(cur_ref, blk * pb + r, dil)


def _stage(dst, ref, dil, pb, nblk, lead=None):
    for blk in range(nblk):
        for r in range(dil):
            val = _rows(ref, blk * pb + r, dil)
            if lead is None:
                dst[blk * dil + r] = val
            else:
                dst[lead, blk * dil + r] = val


def _unstage(ref, src, dil, pb, nblk):
    for blk in range(nblk):
        for r in range(dil):
            _rows_store(ref, blk * pb + r, dil, src[blk * dil + r])


def _attn_fwd(q, k, kv, g, dil, carried=None):
    S = q.shape[0]
    H = q.shape[1] // (3 * LANE)
    pb, nblk, nw, col, win, prev, _ = _attn_geom(S, H, g, dil)
    vcol = 3 * H + col

    def body(q_ref, kc_ref, kp_ref, vc_ref, vp_ref, o_ref, l_ref, qs, ks, vs, os_, ls):
        h, w = pl.program_id(0), pl.program_id(1)
        _stage(qs, q_ref, dil, pb, nblk)
        _stage_band(ks, kc_ref, kp_ref, dil, pb, nblk)
        _stage_band(vs, vc_ref, vp_ref, dil, pb, nblk)

        def run(lo, hi, bias):
            def step(e, carry):
                s = _dot(qs[e].astype(BF), ks[e].astype(BF), NT) + bias
                mx = jnp.max(s, axis=-1, keepdims=True)
                p = jnp.exp(s - mx)
                l = jnp.sum(p, axis=-1, keepdims=True)
                os_[e] = _dot((p / l).astype(BF), vs[e].astype(BF))
                ls[e] = jnp.broadcast_to(mx + jnp.log(l), (LANE, LANE))
                return carry
            if hi > lo:
                lax.fori_loop(lo, hi, step, 0, unroll=True)

        run(0, dil, _band_bias(h, H, dil, w > 0))
        run(dil, N_ENT, _band_bias(h, H, dil, True))
        _unstage(o_ref, os_, dil, pb, nblk)
        _unstage(l_ref, ls, dil, pb, nblk)

    out = jax.ShapeDtypeStruct((S, H * LANE), F32)
    sq = pltpu.VMEM((N_ENT, LANE, LANE), F32)
    sk = pltpu.VMEM((N_ENT, 2 * LANE, LANE), F32)
    return _pcall(
        body, name="attn_fwd_d%d" % dil, grid=(H, nw),
        in_specs=[win(col), win(col), prev(col), win(vcol), prev(vcol)],
        out_specs=[win(0), win(0)], out_shape=[out, out], scratch=[sq, sk, sk, sq, sq],
        sem=("parallel", "parallel"), args=(q, k, k, kv, kv), carried=carried)


def _attn_combine(os_, ls_, carried=None):
    S, C = os_[0].shape
    tm = _tile(S, 256, 16)

    def body(o0, o1, o2, l0, l1, l2, o_ref, lse_ref):
        a, b, c = l0[...], l1[...], l2[...]
        mx = jnp.maximum(jnp.maximum(a, b), c)
        ea, eb, ec = jnp.exp(a - mx), jnp.exp(b - mx), jnp.exp(c - mx)
        den = ea + eb + ec
        o_ref[...] = ((ea * o0[...] + eb * o1[...] + ec * o2[...]) / den).astype(BF)
        lse_ref[...] = mx + jnp.log(den)

    blk = pl.BlockSpec((tm, C), lambda m: (m, 0))
    return _pcall(
        body, name="attn_combine", grid=(S // tm,), in_specs=[blk] * 6, out_specs=[blk, blk],
        out_shape=[jax.ShapeDtypeStruct((S, C), BF), jax.ShapeDtypeStruct((S, C), F32)],
        sem=("parallel",), args=(*os_, *ls_), carried=carried)


def _attn_delta(do, o):
    S, C = do.shape
    tm = _tile(S, 512, 16)

    def body(do_ref, o_ref, d_ref):
        for c in range(C // LANE):
            sl = slice(c * LANE, (c + 1) * LANE)
            prod = do_ref[:, sl].astype(BF).astype(F32) * o_ref[:, sl].astype(F32)
            d_ref[:, sl] = jnp.broadcast_to(jnp.sum(prod, axis=-1, keepdims=True), (tm, LANE))

    blk = pl.BlockSpec((tm, C), lambda m: (m, 0))
    outs, _ = _pcall(
        body, name="attn_delta", grid=(S // tm,), in_specs=[blk, blk], out_specs=[blk],
        out_shape=[jax.ShapeDtypeStruct((S, C), F32)], sem=("parallel",), args=(do, o))
    return outs[0]


def _attn_bwd_dq(q, k, kv, do, lse, dl, g, dil, dq_prev):
    S = q.shape[0]
    H = q.shape[1] // (3 * LANE)
    pb, nblk, nw, col, win, prev, _ = _attn_geom(S, H, g, dil)
    vcol = 3 * H + col

    def body(q_ref, kc_ref, kp_ref, vc_ref, vp_ref, do_ref, l_ref, d_ref, *rest):
        dq_ref, qs, ks, vs, dos, ls, ds_, dqs = rest[-8:]
        h, w = pl.program_id(0), pl.program_id(1)
        _stage(qs, q_ref, dil, pb, nblk)
        _stage_band(ks, kc_ref, kp_ref, dil, pb, nblk)
        _stage_band(vs, vc_ref, vp_ref, dil, pb, nblk)
        _stage(dos, do_ref, dil, pb, nblk)
        _stage(ls, l_ref, dil, pb, nblk)
        _stage(ds_, d_ref, dil, pb, nblk)

        def run(lo, hi, bias):
            def step(e, carry):
                kb = ks[e].astype(BF)
                s = _dot(qs[e].astype(BF), kb, NT) + bias
                p = jnp.exp(s - ls[e][:, :1])
                dp = _dot(dos[e].astype(BF), vs[e].astype(BF), NT)
                dsc = p * (dp - ds_[e][:, :1])
                dqs[e] = _dot(dsc.astype(BF), kb)
                return carry
            if hi > lo:
                lax.fori_loop(lo, hi, step, 0, unroll=True)

        run(0, dil, _band_bias(h, H, dil, w > 0))
        run(dil, N_ENT, _band_bias(h, H, dil, True))
        _unstage(dq_ref, dqs, dil, pb, nblk)

    sq = pltpu.VMEM((N_ENT, LANE, LANE), F32)
    sk = pltpu.VMEM((N_ENT, 2 * LANE, LANE), F32)
    in_specs = [win(col), win(col), prev(col), win(vcol), prev(vcol), win(0), win(0), win(0)]
    args = [q, k, k, kv, kv, do, lse, dl]
    aliases = {}
    if dq_prev is not None:
        in_specs.append(ANY)
        args.append(dq_prev)
        aliases = {8: 0}
    outs, _ = _pcall(
        body, name="attn_bwd_dq_d%d" % dil, grid=(H, nw), in_specs=in_specs,
        out_specs=[win(col)], out_shape=[jax.ShapeDtypeStruct((S, 3 * H * LANE), F32)],
        scratch=[sq, sk, sk, sq, sq, sq, sq], sem=("parallel", "parallel"), args=args, aliases=aliases)
    return outs[0]


def _attn_bwd_dkv(q, k, kv, do, lse, dl, g, dil, prev_out, accumulate):
    S = q.shape[0]
    H = q.shape[1] // (3 * LANE)
    pb, nblk, nw, col, win, _, nxt = _attn_geom(S, H, g, dil)
    vcol = 3 * H + col
    n_q = N_ENT + dil

    def body(k_ref, v_ref, qc_ref, qn_ref, doc_ref, don_ref, lc_ref, ln_ref, dc_ref, dn_ref, *rest):
        dk_ref, dv_ref, ks, vs, qs, dos, ls, ds_, dks, dvs = rest[-10:]
        h, w = pl.program_id(0), pl.program_id(1)
        _stage(ks, k_ref, dil, pb, nblk)
        _stage(vs, v_ref, dil, pb, nblk)
        for dst, cur, nx in ((qs, qc_ref, qn_ref), (dos, doc_ref, don_ref), (ls, lc_ref, ln_ref), (ds_, dc_ref, dn_ref)):
            _stage(dst, cur, dil, pb, nblk)
            for r in range(dil):
                dst[N_ENT + r] = _rows(nx, r, dil)
        qi = lax.broadcasted_iota(jnp.int32, (LANE, LANE), 0)
        kj = lax.broadcasted_iota(jnp.int32, (LANE, LANE), 1)
        sd = _slope_times_dil(h, H, dil, (LANE, LANE))
        bias_c = jnp.where(qi >= kj, -sd * (qi - kj).astype(F32), NEG)

        def run(lo, hi, has_next):
            bias_n = jnp.where((qi <= kj) & has_next, -sd * (qi + LANE - kj).astype(F32), NEG)

            def step(e, carry):
                kb = ks[e].astype(BF)
                vb = vs[e].astype(BF)
                dk = jnp.zeros((LANE, LANE), F32)
                dv = jnp.zeros((LANE, LANE), F32)
                for eq, bias in ((e, bias_c), (e + dil, bias_n)):
                    qb = qs[eq].astype(BF)
                    dob = dos[eq].astype(BF)
                    s = _dot(qb, kb, NT) + bias
                    p = jnp.exp(s - ls[eq][:, :1])
                    dp = _dot(dob, vb, NT)
                    dsc = p * (dp - ds_[eq][:, :1])
                    dv = dv + _dot(p.astype(BF), dob, TN)
                    dk = dk + _dot(dsc.astype(BF), qb, TN)
                dks[e] = dk
                dvs[e] = dv
                return carry
            if hi > lo:
                lax.fori_loop(lo, hi, step, 0, unroll=True)

        run(0, N_ENT - dil, True)
        run(N_ENT - dil, N_ENT, w < nw - 1)
        if accumulate:
            pk_ref, pv_ref = rest[0], rest[1]
            dk_ref[...] = pk_ref[...]
            dv_ref[...] = pv_ref[...]
            for blk in range(nblk):
                for r in range(dil):
                    e, start = blk * dil + r, blk * pb + r
                    _rows_store(dk_ref, start, dil, _rows(dk_ref, start, dil) + dks[e])
                    _rows_store(dv_ref, start, dil, _rows(dv_ref, start, dil) + dvs[e])
        else:
            _unstage(dk_ref, dks, dil, pb, nblk)
            _unstage(dv_ref, dvs, dil, pb, nblk)

    s1 = pltpu.VMEM((N_ENT, LANE, LANE), F32)
    s2 = pltpu.VMEM((n_q, LANE, LANE), F32)
    out = jax.ShapeDtypeStruct((S, 3 * H * LANE), F32)
    in_specs = [win(col), win(vcol), win(col), nxt(col), win(0), nxt(0), win(0), nxt(0), win(0), nxt(0)]
    args = [k, kv, q, q, do, do, lse, lse, dl, dl]
    aliases = {}
    if prev_out is not None:
        in_specs += [win(col), win(col)] if accumulate else [ANY, ANY]
        args += list(prev_out)
        aliases = {10: 0, 11: 1}
    outs, _ = _pcall(
        body, name="attn_bwd_dkv_d%d%s" % (dil, "_acc" if accumulate else ""), grid=(H, nw), in_specs=in_specs,
        out_specs=[win(col), win(col)], out_shape=[out, out],
        scratch=[s1, s1, s2, s2, s2, s2, s1, s1], sem=("parallel", "parallel"), vmem=56, args=args, aliases=aliases)
    return outs


def _loss_head(y, target):
    S, D = y.shape
    tm = _tile(S, 512, 8)

    def body(y_ref, t_ref, dy_ref, l_ref):
        e = y_ref[...] - t_ref[...]
        dy_ref[...] = e * (1.0 / D)
        part = jnp.broadcast_to(jnp.sum(jnp.sum(e * e, axis=1, keepdims=True), axis=0, keepdims=True) * (0.5 / D), (8, LANE))
        _accum_rows(l_ref, part, pl.program_id(0) == 0)

    blk = pl.BlockSpec((tm, D), lambda m: (m, 0))
    outs, _ = _pcall(
        body, name="loss_head", grid=(S // tm,), in_specs=[blk, blk],
        out_specs=[blk, pl.BlockSpec((8, LANE), lambda m: (0, 0))],
        out_shape=[jax.ShapeDtypeStruct((S, D), F32), jax.ShapeDtypeStruct((8, LANE), F32)],
        sem=("arbitrary",), args=(y, target))
    return outs


class _Queue:
    def __init__(self):
        self.items = []

    def push(self, kind, key, arr):
        self.items.append((kind, key, arr))

    def take(self, budget_us):
        taken, spent = [], 0.0
        while self.items and spent < budget_us:
            item = self.items.pop(0)
            taken.append(item)
            spent += COST_US_PER_ELEM[item[0]] * (item[2].size / item[2].shape[0] if item[0] != "gather" else item[2].size)
        return taken

    def take_keys(self, keys):
        taken = [it for it in self.items if it[1] in keys]
        self.items = [it for it in self.items if it[1] not in keys]
        return taken

    def take_kind(self, kind):
        taken = [it for it in self.items if it[0] == kind]
        self.items = [it for it in self.items if it[0] != kind]
        return taken


def kernel(x, ffn1_norm, ffn1_w_gate, ffn1_w_up, ffn1_w_down, mix_norm, ffn2_norm, ffn2_w_gate, ffn2_w_up, ffn2_w_down, gmlp_w_in, gmlp_v_norm, gmlp_w_s, gmlp_b_s, gmlp_w_out, kv_norm, w_kv, k_norm, attn_w_q, attn_q_norm, attn_w_o, loss_target, m_ffn1_norm, m_ffn1_w_gate, m_ffn1_w_up, m_ffn1_w_down, m_mix_norm, m_ffn2_norm, m_ffn2_w_gate, m_ffn2_w_up, m_ffn2_w_down, m_gmlp_w_in, m_gmlp_v_norm, m_gmlp_w_s, m_gmlp_b_s, m_gmlp_w_out, m_kv_norm, m_w_kv, m_k_norm, m_attn_w_q, m_attn_q_norm, m_attn_w_o, v_ffn1_norm, v_ffn1_w_gate, v_ffn1_w_up, v_ffn1_w_down, v_mix_norm, v_ffn2_norm, v_ffn2_w_gate, v_ffn2_w_up, v_ffn2_w_down, v_gmlp_w_in, v_gmlp_v_norm, v_gmlp_w_s, v_gmlp_b_s, v_gmlp_w_out, v_kv_norm, v_w_kv, v_k_norm, v_attn_w_q, v_attn_q_norm, v_attn_w_o):
    names = ["ffn1_norm", "ffn1_w_gate", "ffn1_w_up", "ffn1_w_down", "mix_norm", "ffn2_norm", "ffn2_w_gate",
             "ffn2_w_up", "ffn2_w_down", "gmlp_w_in", "gmlp_v_norm", "gmlp_w_s", "gmlp_b_s", "gmlp_w_out",
             "kv_norm", "w_kv", "k_norm", "attn_w_q", "attn_q_norm", "attn_w_o"]
    W = dict(zip(names, [ffn1_norm, ffn1_w_gate, ffn1_w_up, ffn1_w_down, mix_norm, ffn2_norm, ffn2_w_gate,
                         ffn2_w_up, ffn2_w_down, gmlp_w_in, gmlp_v_norm, gmlp_w_s, gmlp_b_s, gmlp_w_out,
                         kv_norm, w_kv, k_norm, attn_w_q, attn_q_norm, attn_w_o]))
    M = dict(zip(names, [m_ffn1_norm, m_ffn1_w_gate, m_ffn1_w_up, m_ffn1_w_down, m_mix_norm, m_ffn2_norm, m_ffn2_w_gate,
                         m_ffn2_w_up, m_ffn2_w_down, m_gmlp_w_in, m_gmlp_v_norm, m_gmlp_w_s, m_gmlp_b_s, m_gmlp_w_out,
                         m_kv_norm, m_w_kv, m_k_norm, m_attn_w_q, m_attn_q_norm, m_attn_w_o]))
    V = dict(zip(names, [v_ffn1_norm, v_ffn1_w_gate, v_ffn1_w_up, v_ffn1_w_down, v_mix_norm, v_ffn2_norm, v_ffn2_w_gate,
                         v_ffn2_w_up, v_ffn2_w_down, v_gmlp_w_in, v_gmlp_v_norm, v_gmlp_w_s, v_gmlp_b_s, v_gmlp_w_out,
                         v_kv_norm, v_w_kv, v_k_norm, v_attn_w_q, v_attn_q_norm, v_attn_w_o]))

    depth = ffn1_norm.shape[0]
    n_a = gmlp_w_in.shape[0]
    S, D = x.shape[1], x.shape[2]
    H = D // LANE
    n_grp = len(DILATIONS)
    hw = H * LANE
    xi, yi, ci = _me()
    core = jnp.reshape(ci, (1,)).astype(jnp.int32)
    chip = jnp.reshape(2 * xi + yi, (1,)).astype(jnp.int32)
    dev = 4 * xi + 2 * yi + ci
    q_scale = LANE ** -0.5
    ffn_names = (("f1", ("ffn1_norm", "ffn1_w_gate", "ffn1_w_up", "ffn1_w_down")),
                 ("f2", ("ffn2_norm", "ffn2_w_gate", "ffn2_w_up", "ffn2_w_down")))
    transposed = ("ffn1_w_gate", "ffn1_w_up", "ffn2_w_gate", "ffn2_w_up")

    def gain(v):
        return v.reshape(1, -1)

    def head_gain(g3):
        return jnp.tile(g3[:, None, :], (1, H, 1)).reshape(1, n_grp * hw)

    gq = _Queue()
    gathered = {}
    gq.push("gather", "v_norm", jnp.pad(gmlp_v_norm, ((0, 8 - n_a), (0, 0))))
    for l in range(depth):
        for tag, (_, wgn, wun, wdn) in ffn_names:
            if tag == "f2":
                if l < n_a:
                    gq.push("gather", ("gmlp_w_in", l), gmlp_w_in[l].astype(BF))
                    gq.push("gather", ("gmlp_w_out", l), gmlp_w_out[l].astype(BF))
                else:
                    gq.push("gather", ("attn_w_q", l - n_a), attn_w_q[l - n_a].astype(BF))
                    gq.push("gather", ("attn_w_o", l - n_a), attn_w_o[l - n_a].astype(BF))
            for n in (wgn, wun, wdn):
                gq.push("gather", (n, l), W[n][l].astype(BF))
        if l == n_a - 1:
            gq.push("gather", "w_kv", w_kv.astype(BF))

    def land(items, outs):
        for (_, key, _), o in zip(items, outs):
            gathered[key] = o

    def need(*keys):
        items = gq.take_keys([k for k in keys if k not in gathered])
        if items:
            land(items, _comm_only(_Carried([(k, a) for k, _, a in items]), "allgather"))
        return [gathered[k] for k in keys]

    def carry(q, kind):
        items = q.take(BUDGET_US[kind])
        return items, _Carried([(k, a) for k, _, a in items])

    v_all = need("v_norm", *[(n, 0) for n in ffn_names[0][1][1:]])[0]
    v_gain_all = jnp.transpose(v_all[:, :n_a], (1, 0, 2)).reshape(n_a, -1)

    cur = x.reshape(S, D)
    saved = []
    k_sh = kv_raw = kv_hn = kv_x = k_gain = None

    for l in range(depth):
        rec = {}
        for tag, (nn, wgn, wun, wdn) in ffn_names:
            if tag == "f2":
                rec["mix_x"] = cur
                if l < n_a:
                    w_in, w_out = need(("gmlp_w_in", l), ("gmlp_w_out", l))
                    bias = jnp.repeat(gmlp_b_s[l].T, LANE, axis=1)
                    items, car = carry(gq, "normproj")
                    (hm, z), couts = _normproj(cur, gain(mix_norm[l]), w_in, "gmlp_in", car)
                    land(items, couts)
                    items, car = carry(gq, "gmlp_fwd")
                    t, couts = _gmlp_fwd(z, gain(v_gain_all[l]), gmlp_w_s[l], bias, car)
                    land(items, couts)
                    items, car = carry(gq, "rowproj")
                    cur, couts = _rowproj(cur, t, w_out, "proj_out", car)
                    land(items, couts)
                    rec.update(w_in=w_in, w_out=w_out, bias=bias, hm=hm, z=z, t=t)
                else:
                    jj = l - n_a
                    w_q, w_o = need(("attn_w_q", jj), ("attn_w_o", jj))
                    items, car = carry(gq, "normproj")
                    (hm, q_raw), couts = _normproj(cur, gain(mix_norm[l]), w_q, "attn_q", car)
                    land(items, couts)
                    qg = head_gain(attn_q_norm[jj])
                    items, car = carry(gq, "headnorm")
                    q, couts = _headnorm_fwd(q_raw, n_grp * hw, qg, q_scale, "headnorm_q", car)
                    land(items, couts)
                    outs = []
                    for g, dil in enumerate(DILATIONS):
                        items, car = carry(gq, "attn_fwd")
                        og, couts = _attn_fwd(q, k_sh, kv_raw, g, dil, car)
                        land(items, couts)
                        outs.append(og)
                    items, car = carry(gq, "attn_combine")
                    (o, lse), couts = _attn_combine([o_ for o_, _ in outs], [l_ for _, l_ in outs], car)
                    land(items, couts)
                    items, car = carry(gq, "rowproj")
                    cur, couts = _rowproj(cur, o, w_o, "proj_out", car)
                    land(items, couts)
                    rec.update(w_q=w_q, w_o=w_o, hm=hm, q_raw=q_raw, qg=qg, q=q, o=o, lse=lse)
            wg, wu, wd = need((wgn, l), (wun, l), (wdn, l))
            rec[tag + "_x"] = cur
            items, car = carry(gq, "ffn_fwd")
            (cur, hn, act, ga, gb), couts = _ffn_fwd(cur, gain(W[nn][l]), wg, wu, wd, car)
            land(items, couts)
            rec[tag] = (wg, wu, wd, hn, act, ga, gb)
        if l == n_a - 1:
            (w_kv_g,) = need("w_kv")
            kv_x = cur
            items, car = carry(gq, "normproj")
            (kv_hn, kv_raw), couts = _normproj(cur, gain(kv_norm), w_kv_g, "kv_proj", car)
            land(items, couts)
            k_gain = head_gain(k_norm)
            items, car = carry(gq, "headnorm")
            k_sh, couts = _headnorm_fwd(kv_raw, n_grp * hw, k_gain, 1.0, "headnorm_k", car)
            land(items, couts)
        saved.append(rec)

    dcur, loss_part = _loss_head(cur, loss_target.reshape(S, D))
    loss = lax.psum(loss_part[0, 0], ("x", "y", "c"))

    rq = _Queue()
    reduced = {}
    small = {}

    def rs_land(items, outs):
        for (kind, key, arr), o in zip(items, outs):
            if kind == "core":
                rq.push("chip", key, _pair_add(arr, o, core))
            else:
                reduced[key] = (arr, o)

    def rs_carry(kind):
        items = rq.take(BUDGET_US[kind])
        return items, _Carried([(k, a) for k, _, a in items])

    def wgrad(key, a, a_kind, b, b_kind, name):
        items, car = rs_carry("wgrad")
        p, couts = _wgrad(a, a_kind, b, b_kind, name, car)
        rs_land(items, couts)
        rq.push("core", key, p)

    def put(name, l, val, n_layers):
        small.setdefault(name, [None] * n_layers)[l] = val

    dk_dv = None
    for l in reversed(range(depth)):
        rec = saved[l]
        if l == n_a - 1:
            dkv, dkg = _headnorm_bwd(dk_dv[0], kv_raw, k_gain, 1.0, "headnorm_k_bwd", tail=dk_dv[1])
            wgrad(("w_kv", 0), kv_hn, "full", dkv, "cols", "wgrad_kv")
            items, car = rs_carry("dgrad_col_norm")
            (dcur, dg), couts = _dgrad_col_norm(dcur, kv_x, gain(kv_norm), dkv, gathered["w_kv"], "dgrad_kv", car)
            rs_land(items, couts)
            small["kv_norm"] = [dg[0]]
            small["k_norm"] = [dkg[0].reshape(n_grp, H, LANE).sum(axis=1)]
        for tag, (nn, wgn, wun, wdn) in reversed(ffn_names):
            wg, wu, wd, hn, act, ga, gb = rec[tag]
            items, car = rs_carry("ffn_bwd_dx")
            (dcur, da, db, dyh, dg), couts = _ffn_bwd_dx(dcur, rec[tag + "_x"], gain(W[nn][l]), ga, gb, wg, wu, wd, car)
            rs_land(items, couts)
            wgrad((wgn, l), da, "stack", hn, "full", "wgrad_ffn_in")
            wgrad((wun, l), db, "stack", hn, "full", "wgrad_ffn_in")
            wgrad((wdn, l), act, "stack", dyh, "full", "wgrad_ffn_out")
            put(nn, l, dg[0], depth)
            if tag == "f2":
                mix_x = rec["mix_x"]
                if l < n_a:
                    dt = _dgrad_row(dcur, rec["w_out"], BF, "dgrad_gmlp_out")
                    wgrad(("gmlp_w_out", l), rec["t"], "cols", dcur, "full", "wgrad_proj_out")
                    dz, dws, dbias, dvg = _gmlp_bwd(rec["z"], dt, gain(v_gain_all[l]), gmlp_w_s[l], rec["bias"])
                    wgrad(("gmlp_w_in", l), rec["hm"], "full", dz, "cols", "wgrad_gmlp_in")
                    items, car = rs_carry("dgrad_col_norm")
                    (dcur, dg), couts = _dgrad_col_norm(dcur, mix_x, gain(mix_norm[l]), dz, rec["w_in"], "dgrad_gmlp_in", car)
                    rs_land(items, couts)
                    put("gmlp_w_s", l, dws, n_a)
                    put("gmlp_b_s", l, dbias[:, ::LANE].T, n_a)
                    put("gmlp_v_norm", l, dvg[0], n_a)
                else:
                    jj = l - n_a
                    do = _dgrad_row(dcur, rec["w_o"], F32, "dgrad_attn_out")
                    wgrad(("attn_w_o", jj), rec["o"], "cols", dcur, "full", "wgrad_proj_out")
                    dl = _attn_delta(do, rec["o"])
                    dq = None
                    first_layer = dk_dv is None
                    for g, dil in enumerate(DILATIONS):
                        dq = _attn_bwd_dq(rec["q"], k_sh, kv_raw, do, rec["lse"], dl, g, dil, dq)
                        dk_dv = _attn_bwd_dkv(rec["q"], k_sh, kv_raw, do, rec["lse"], dl, g, dil, dk_dv,
                                              accumulate=not first_layer)
                    dq_raw, dqg = _headnorm_bwd(dq, rec["q_raw"], rec["qg"], q_scale, "headnorm_q_bwd")
                    wgrad(("attn_w_q", jj), rec["hm"], "full", dq_raw, "cols", "wgrad_attn_q")
                    items, car = rs_carry("dgrad_col_norm")
                    (dcur, dg), couts = _dgrad_col_norm(dcur, mix_x, gain(mix_norm[l]), dq_raw, rec["w_q"], "dgrad_attn_q", car)
                    rs_land(items, couts)
                    put("attn_q_norm", jj, dqg[0].reshape(n_grp, H, LANE).sum(axis=1), depth - n_a)
                put("mix_norm", l, dg[0], depth)
    grad_x = dcur.reshape(1, S, D)

    for kind in ("core", "chip"):
        items = rq.take_kind(kind)
        if items:
            rs_land(items, _comm_only(_Carried([(k, a) for k, _, a in items]), "rs_%s_exchange" % kind))

    out_g, out_d, out_m, out_v = {}, {}, {}, {}
    for name in names:
        if (name, 0) not in reduced:
            continue
        if name in transposed:
            as3 = lambda t: jnp.swapaxes(t, 1, 2)
        else:
            as3 = (lambda t: t[None]) if W[name].ndim == 2 else (lambda t: t)
        res = None
        for l in range(as3(W[name]).shape[0]):
            qsum, r2 = reduced[(name, l)]
            res = _adamw_shard(qsum, r2, chip, as3(W[name]), as3(M[name]), as3(V[name]), l, res)
        for dct, val in zip((out_g, out_d, out_m, out_v), res):
            dct[name] = jnp.swapaxes(val, 1, 2) if name in transposed else (val[0] if W[name].ndim == 2 else val)

    small_names = [n for n in names if n in small]
    full_shape = {n: (W[n].shape if n != "gmlp_v_norm" else (n_a, v_gain_all.shape[1])) for n in small_names}
    flat = jnp.concatenate([jnp.stack(small[n]).reshape(-1) if W[n].ndim > 1 else small[n][0].reshape(-1)
                            for n in small_names])
    n_flat = flat.shape[0]
    rows = -(-n_flat // (8 * LANE)) * 8

    def pack(parts_list):
        v = jnp.concatenate([p.reshape(-1) for p in parts_list])
        return jnp.pad(v, (0, rows * LANE - n_flat)).reshape(rows, LANE)

    def full_of(dct, n, fill):
        if n != "gmlp_v_norm":
            return dct[n]
        sh = dct[n].shape[1]
        return lax.dynamic_update_slice(jnp.full(full_shape[n], fill, F32), dct[n], (0, dev * sh))

    (g_all,) = _comm_only(_Carried([("gather", pack([flat]))]), "allgather_small_grads")
    w_p = pack([full_of(W, n, 0.0) for n in small_names])
    m_p = pack([full_of(M, n, 0.0) for n in small_names])
    v_p = pack([full_of(V, n, 1.0) for n in small_names])
    packed = _adamw_replicated(g_all, w_p, m_p, v_p)
    offs = 0
    for n in small_names:
        size = math.prod(full_shape[n])
        for dct, arr in zip((out_g, out_d, out_m, out_v), packed):
            val = arr.reshape(-1)[offs:offs + size].reshape(full_shape[n])
            if n == "gmlp_v_norm":
                sh = W[n].shape[1]
                val = lax.dynamic_slice(val, (0, dev * sh), (n_a, sh))
            dct[n] = val
        offs += size

    return (loss, grad_x, *[out_g[n] for n in names], *[out_d[n] for n in names],
            *[out_m[n] for n in names], *[out_v[n] for n in names])
```

```python
import math

import jax
import jax.numpy as jnp
from jax import lax
from jax.experimental import pallas as pl
from jax.experimental.pallas import tpu as pltpu

F32 = jnp.float32
BF = jnp.bfloat16
N_DEV = 8
N_CHIP = 4
LANE = 128
ATT_WIN = 16 * LANE
EPS = 1e-6
NEG = -1e30
DILATIONS = (1, 4, 16)
ADAM_LR, ADAM_B1, ADAM_B2, ADAM_EPS, ADAM_WD, ADAM_STEP = 0.001, 0.9, 0.999, 1e-08, 0.01, 10
GELU_C0, GELU_C1 = 0.7978845608028654, 0.044715
VMEM_MB = 2 ** 20
BUDGET_US = {"ffn_fwd": 380.0, "normproj": 150.0, "rowproj": 100.0, "gmlp_fwd": 50.0, "headnorm": 75.0,
             "attn_fwd": 120.0, "attn_combine": 75.0,
             "ffn_bwd_dx": 480.0, "wgrad": 0.0, "dgrad_col_norm": 250.0}
COST_US_PER_ELEM = {"gather": 1.0e-4, "core": 1.4e-5, "chip": 9.0e-5}

MESH_T = pl.DeviceIdType.MESH
ANY = pl.BlockSpec(memory_space=pl.ANY)
DMA_SEM = pltpu.SemaphoreType.DMA
NT = (((1,), (1,)), ((), ()))
TN = (((0,), (0,)), ((), ()))


def _tile(n, target, mult):
    best = None
    for t in range(mult, min(n, target) + 1, mult):
        if n % t == 0:
            best = t
    if best is None:
        best = n
    return best


def _dot(a, b, dims=None):
    if dims is None:
        return jnp.dot(a, b, preferred_element_type=F32)
    return lax.dot_general(a, b, dims, preferred_element_type=F32)


def _rms_hat(xv):
    r = lax.rsqrt(jnp.mean(xv * xv, axis=-1, keepdims=True) + EPS)
    return xv * r, r


def _rms_bwd(dhn, xv, gain, dres):
    xhat, r = _rms_hat(xv)
    dxhat = dhn * gain
    dx = dres + r * (dxhat - xhat * jnp.mean(dxhat * xhat, axis=-1, keepdims=True))
    return dx, jnp.sum(dhn * xhat, axis=0, keepdims=True)


def _accum_rows(ref, row, first):
    val = jnp.broadcast_to(row, ref.shape)

    @pl.when(first)
    def _():
        ref[...] = val

    @pl.when(jnp.logical_not(first))
    def _():
        ref[...] += val


def _gelu(z):
    t = jnp.tanh(GELU_C0 * (z + GELU_C1 * z * z * z))
    return 0.5 * z * (1.0 + t), t


def _gelu_grad(z, t):
    return 0.5 * (1.0 + t) + 0.5 * z * (1.0 - t * t) * GELU_C0 * (1.0 + 3.0 * GELU_C1 * z * z)


def _me():
    return lax.axis_index("x"), lax.axis_index("y"), lax.axis_index("c")


def _gather_phase(phase, x_ref, out_ref, send_sems, recv_sems, local_sem):
    x, y, c = _me()
    me, sibling = (x, y, c), (x, y, 1 - c)
    chips = [(1 - x, y), (x, 1 - y), (1 - x, 1 - y)]

    def slot(px, py, pc):
        return out_ref.at[4 * px + 2 * py + pc]

    def copy(k, block, to, src=None):
        return pltpu.make_async_remote_copy(
            src_ref=slot(*block) if src is None else src, dst_ref=slot(*block),
            send_sem=send_sems.at[k], recv_sem=recv_sems.at[k], device_id=to, device_id_type=MESH_T)

    mine = pltpu.make_async_copy(x_ref, slot(*me), local_sem)
    first = [copy(0, me, sibling, src=x_ref)]
    first += [copy(1 + j, me, (*chip, c), src=x_ref) for j, chip in enumerate(chips)]
    passed = [copy(4 + j, (*chip, c), sibling) for j, chip in enumerate(chips)]
    if phase == 0:
        mine.start()
        for cp in first:
            cp.start()
    elif phase == 1:
        for j, chip in enumerate(chips):
            copy(1 + j, (*chip, c), me).wait_recv()
            passed[j].start()
    else:
        copy(0, sibling, me).wait_recv()
        for j, chip in enumerate(chips):
            copy(4 + j, (*chip, 1 - c), me).wait_recv()
        for cp in first + passed:
            cp.wait_send()
        mine.wait()


def _exchange_phase(phase, kind, src_ref, dst_ref, send_sems, recv_sems):
    x, y, c = _me()
    if kind == "core":
        plan = [(2 * k + (1 - c), k, (x, y, 1 - c)) for k in range(N_CHIP)]
    else:
        plan = [(2 * px + py, t, (px, py, c)) for t, (px, py) in enumerate([(1 - x, y), (x, 1 - y), (1 - x, 1 - y)])]
    cps = [pltpu.make_async_remote_copy(
        src_ref=src_ref.at[s], dst_ref=dst_ref.at[d], send_sem=send_sems.at[i], recv_sem=recv_sems.at[i],
        device_id=to, device_id_type=MESH_T) for i, (s, d, to) in enumerate(plan)]
    if phase == 0:
        for cp in cps:
            cp.start()
    elif phase == 2:
        for cp in cps:
            cp.wait()


_N_COPIES = {"gather": 7, "core": N_CHIP, "chip": 3}


class _Carried:
    def __init__(self, items=()):
        self.items = list(items)

    def arrays(self):
        return [a for _, a in self.items]

    def out_shapes(self):
        lead = {"gather": lambda a: (N_DEV,) + a.shape, "core": lambda a: (N_CHIP,) + a.shape[1:],
                "chip": lambda a: (3,) + a.shape[1:]}
        return [jax.ShapeDtypeStruct(lead[k](a), a.dtype) for k, a in self.items]

    def scratch(self):
        res = []
        for k, _ in self.items:
            res += [DMA_SEM((_N_COPIES[k],)), DMA_SEM((_N_COPIES[k],))]
            if k == "gather":
                res.append(DMA_SEM(()))
        return res

    def emit(self, phase, in_refs, out_refs, scr):
        i = 0
        for (kind, _), src, dst in zip(self.items, in_refs, out_refs):
            if kind == "gather":
                _gather_phase(phase, src, dst, scr[i], scr[i + 1], scr[i + 2])
                i += 3
            else:
                _exchange_phase(phase, kind, src, dst, scr[i], scr[i + 1])
                i += 2


def _comm_only(carried, name):
    nc = len(carried.items)

    def body(*refs):
        for phase in range(3):
            carried.emit(phase, refs[:nc], refs[nc:2 * nc], refs[2 * nc:])

    return pl.pallas_call(
        body, name=name, out_shape=carried.out_shapes(), in_specs=[ANY] * nc, out_specs=[ANY] * nc,
        scratch_shapes=carried.scratch(),
    )(*carried.arrays())


def _pcall(main, *, name, grid, in_specs, out_specs, out_shape, args, scratch=(), sem=None, vmem=48,
           carried=None, aliases=None):
    params = pltpu.CompilerParams(dimension_semantics=sem, vmem_limit_bytes=vmem * VMEM_MB)
    n_in, n_out, n_scr = len(in_specs), len(out_specs), len(scratch)
    if carried is None or not carried.items:
        outs = pl.pallas_call(
            main, name=name, grid=grid, in_specs=in_specs, out_specs=out_specs, out_shape=out_shape,
            scratch_shapes=list(scratch), compiler_params=params, input_output_aliases=aliases or {},
        )(*args)
        return list(outs), []
    nc = len(carried.items)
    total = math.prod(grid)

    def body(*refs):
        ins, cin = refs[:n_in], refs[n_in:n_in + nc]
        o0 = n_in + nc
        outs, cout = refs[o0:o0 + n_out], refs[o0 + n_out:o0 + n_out + nc]
        s0 = o0 + n_out + nc
        scr, cscr = refs[s0:s0 + n_scr], refs[s0 + n_scr:]
        step = 0
        for d, n in enumerate(grid):
            step = step * n + pl.program_id(d)

        @pl.when(step == 0)
        def _():
            carried.emit(0, cin, cout, cscr)

        main(*ins, *outs, *scr)

        @pl.when(step == max(total - 2, 0))
        def _():
            carried.emit(1, cin, cout, cscr)

        @pl.when(step == total - 1)
        def _():
            carried.emit(2, cin, cout, cscr)

    outs = pl.pallas_call(
        body, name=name, grid=grid, in_specs=list(in_specs) + [ANY] * nc, out_specs=list(out_specs) + [ANY] * nc,
        out_shape=list(out_shape) + carried.out_shapes(), scratch_shapes=list(scratch) + carried.scratch(),
        compiler_params=params, input_output_aliases=aliases or {},
    )(*args, *carried.arrays())
    return list(outs[:n_out]), list(outs[n_out:])


def _pair_add(p, r1, core):
    _, rows, cols = p.shape
    tr = _tile(rows, 512, 16)
    p4 = p.reshape(N_CHIP, 2, rows, cols)

    def body(core_ref, p_ref, r_ref, q_ref):
        q_ref[...] = (p_ref[...].astype(F32) + r_ref[...].astype(F32)).astype(BF)

    grid_spec = pltpu.PrefetchScalarGridSpec(
        num_scalar_prefetch=1, grid=(N_CHIP, rows // tr),
        in_specs=[pl.BlockSpec((None, None, tr, cols), lambda k, i, cr: (k, cr[0], i, 0)),
                  pl.BlockSpec((None, tr, cols), lambda k, i, cr: (k, i, 0))],
        out_specs=pl.BlockSpec((None, tr, cols), lambda k, i, cr: (k, i, 0)))
    return pl.pallas_call(
        body, name="pair_add", grid_spec=grid_spec, out_shape=jax.ShapeDtypeStruct((N_CHIP, rows, cols), BF),
        compiler_params=pltpu.CompilerParams(dimension_semantics=("parallel", "parallel")),
    )(core, p4, r1)


def _adam_math(g, w, m, v):
    m2 = ADAM_B1 * m + (1.0 - ADAM_B1) * g
    v2 = ADAM_B2 * v + (1.0 - ADAM_B2) * (g * g)
    m_hat = m2 / (1.0 - ADAM_B1 ** ADAM_STEP)
    v_hat = v2 / (1.0 - ADAM_B2 ** ADAM_STEP)
    delta = -ADAM_LR * (m_hat / (jnp.sqrt(v_hat) + ADAM_EPS) + ADAM_WD * w)
    return delta, m2, v2


def _adamw_shard(q, r2, chip, w, m, v, layer, prev):
    n_layers, rows, cols = w.shape
    tr = _tile(rows, 256, 16)

    def body(chip_ref, q_ref, r_ref, w_ref, m_ref, v_ref, *rest):
        g_ref, d_ref, m2_ref, v2_ref = rest[-4:]
        g = q_ref[...].astype(F32) + r_ref[0].astype(F32) + r_ref[1].astype(F32) + r_ref[2].astype(F32)
        d, m2, v2 = _adam_math(g, w_ref[...], m_ref[...], v_ref[...])
        g_ref[...] = g
        d_ref[...] = d
        m2_ref[...] = m2
        v2_ref[...] = v2

    blk = pl.BlockSpec((None, tr, cols), lambda i, cr: (layer, i, 0))
    in_specs = [pl.BlockSpec((None, tr, cols), lambda i, cr: (cr[0], i, 0)),
                pl.BlockSpec((3, tr, cols), lambda i, cr: (0, i, 0)), blk, blk, blk]
    args = [chip, q, r2, w, m, v]
    aliases = {}
    if prev is not None:
        in_specs += [ANY] * 4
        args += list(prev)
        aliases = {6 + i: i for i in range(4)}
    grid_spec = pltpu.PrefetchScalarGridSpec(
        num_scalar_prefetch=1, grid=(rows // tr,), in_specs=in_specs, out_specs=[blk, blk, blk, blk])
    out = jax.ShapeDtypeStruct((n_layers, rows, cols), F32)
    return pl.pallas_call(
        body, name="adamw_shard", grid_spec=grid_spec, out_shape=[out, out, out, out], input_output_aliases=aliases,
        compiler_params=pltpu.CompilerParams(dimension_semantics=("parallel",)),
    )(*args)


def _adamw_replicated(parts, w, m, v):
    rows, cols = w.shape
    tr = _tile(rows, 512, 8)

    def body(p_ref, w_ref, m_ref, v_ref, g_ref, d_ref, m2_ref, v2_ref):
        g = p_ref[0]
        for k in range(1, N_DEV):
            g = g + p_ref[k]
        d, m2, v2 = _adam_math(g, w_ref[...], m_ref[...], v_ref[...])
        g_ref[...] = g
        d_ref[...] = d
        m2_ref[...] = m2
        v2_ref[...] = v2

    blk = pl.BlockSpec((tr, cols), lambda i: (i, 0))
    out = jax.ShapeDtypeStruct((rows, cols), F32)
    outs, _ = _pcall(body, name="adamw_replicated", grid=(rows // tr,),
                     in_specs=[pl.BlockSpec((N_DEV, tr, cols), lambda i: (0, i, 0)), blk, blk, blk],
                     out_specs=[blk, blk, blk, blk], out_shape=[out, out, out, out], sem=("parallel",),
                     args=(parts, w, m, v))
    return outs


def _ffn_fwd(x, gain, wg, wu, wd, carried=None):
    S, D = x.shape
    nsh, fs, _ = wg.shape
    tm = _tile(S, 512, 16)

    def body(x_ref, g_ref, wg_ref, wu_ref, wd_ref, y_ref, hn_ref, act_ref, ga_ref, gb_ref, acc_ref):
        j = pl.program_id(1)

        @pl.when(j == 0)
        def _():
            xhat, _ = _rms_hat(x_ref[...])
            hn_ref[...] = (xhat * g_ref[...]).astype(BF)
            acc_ref[...] = jnp.zeros_like(acc_ref)

        hn = hn_ref[...]
        a = _dot(hn, wg_ref[...], NT)
        b = _dot(hn, wu_ref[...], NT)
        sg = jax.nn.sigmoid(a)
        sil = a * sg
        act = (sil * b).astype(BF)
        act_ref[...] = act
        ga_ref[...] = (b * (sg * (1.0 + a * (1.0 - sg)))).astype(BF)
        gb_ref[...] = sil.astype(BF)
        acc_ref[...] += _dot(act, wd_ref[...])

        @pl.when(j == nsh - 1)
        def _():
            y_ref[...] = x_ref[...] + 0.5 * acc_ref[...]

    row = pl.BlockSpec((tm, D), lambda m, j: (m, 0))
    hid = pl.BlockSpec((None, tm, fs), lambda m, j: (j, m, 0))
    return _pcall(
        body, name="ffn_fwd", grid=(S // tm, nsh),
        in_specs=[row, pl.BlockSpec((1, D), lambda m, j: (0, 0)),
                  pl.BlockSpec((None, fs, D), lambda m, j: (j, 0, 0)),
                  pl.BlockSpec((None, fs, D), lambda m, j: (j, 0, 0)),
                  pl.BlockSpec((None, fs, D), lambda m, j: (j, 0, 0))],
        out_specs=[row, row, hid, hid, hid],
        out_shape=[jax.ShapeDtypeStruct((S, D), F32), jax.ShapeDtypeStruct((S, D), BF)]
        + [jax.ShapeDtypeStruct((nsh, S, fs), BF)] * 3,
        scratch=[pltpu.VMEM((tm, D), F32)], sem=("parallel", "arbitrary"), vmem=58,
        args=(x, gain, wg, wu, wd), carried=carried)


def _ffn_bwd_dx(dy, x, gain, ga, gb, wg, wu, wd, carried=None):
    S, D = x.shape
    nsh, fs, _ = wg.shape
    tm = _tile(S, 512, 16)

    def body(dy_ref, x_ref, g_ref, ga_ref, gb_ref, wg_ref, wu_ref, wd_ref,
             dx_ref, da_ref, db_ref, dyh_ref, dg_ref, acc_ref):
        m, j = pl.program_id(0), pl.program_id(1)

        @pl.when(j == 0)
        def _():
            dyh_ref[...] = (0.5 * dy_ref[...]).astype(BF)
            acc_ref[...] = jnp.zeros_like(acc_ref)

        dact = _dot(dyh_ref[...], wd_ref[...], NT)
        da = (dact * ga_ref[...].astype(F32)).astype(BF)
        db = (dact * gb_ref[...].astype(F32)).astype(BF)
        da_ref[...] = da
        db_ref[...] = db
        acc_ref[...] += _dot(da, wg_ref[...])
        acc_ref[...] += _dot(db, wu_ref[...])

        @pl.when(j == nsh - 1)
        def _():
            dx, dgain = _rms_bwd(acc_ref[...], x_ref[...], g_ref[...], dy_ref[...])
            dx_ref[...] = dx
            _accum_rows(dg_ref, dgain, m == 0)

    row = pl.BlockSpec((tm, D), lambda m, j: (m, 0))
    row1 = pl.BlockSpec((tm, D), lambda m, j: (m, 0), pipeline_mode=pl.Buffered(1))
    hid = pl.BlockSpec((None, tm, fs), lambda m, j: (j, m, 0))
    hshape = jax.ShapeDtypeStruct((nsh, S, fs), BF)
    return _pcall(
        body, name="ffn_bwd_dx", grid=(S // tm, nsh),
        in_specs=[row1, row1, pl.BlockSpec((1, D), lambda m, j: (0, 0)), hid, hid,
                  pl.BlockSpec((None, fs, D), lambda m, j: (j, 0, 0)),
                  pl.BlockSpec((None, fs, D), lambda m, j: (j, 0, 0)),
                  pl.BlockSpec((None, fs, D), lambda m, j: (j, 0, 0))],
        out_specs=[row1, hid, hid, row1, pl.BlockSpec((8, D), lambda m, j: (0, 0))],
        out_shape=[jax.ShapeDtypeStruct((S, D), F32), hshape, hshape,
                   jax.ShapeDtypeStruct((S, D), BF), jax.ShapeDtypeStruct((8, D), F32)],
        scratch=[pltpu.VMEM((tm, D), F32)], sem=("arbitrary", "arbitrary"), vmem=60,
        args=(dy, x, gain, ga, gb, wg, wu, wd), carried=carried)


def _opspec(arr, kind, tm, grid_mj):
    if kind == "full":
        return pl.BlockSpec((tm, arr.shape[1]), lambda *g: (grid_mj(*g)[0], 0)), arr.shape[1]
    if kind == "cols":
        n = arr.shape[1] // N_DEV
        return pl.BlockSpec((tm, n), lambda *g: grid_mj(*g)), n
    n = arr.shape[2]
    return pl.BlockSpec((None, tm, n), lambda *g: (grid_mj(*g)[1], grid_mj(*g)[0], 0)), n


def _wgrad(a, a_kind, b, b_kind, name, carried=None):
    S = a.shape[0] if a_kind != "stack" else a.shape[1]
    tm = _tile(S, 1024, 16)
    mj = lambda j, m: (m, j)
    a_spec, ka = _opspec(a, a_kind, tm, mj)
    b_spec, nb = _opspec(b, b_kind, tm, mj)
    n_m = S // tm

    def body(a_ref, b_ref, o_ref, acc_ref):
        m = pl.program_id(1)

        @pl.when(m == 0)
        def _():
            acc_ref[...] = jnp.zeros_like(acc_ref)

        acc_ref[...] += _dot(a_ref[...].astype(BF), b_ref[...].astype(BF), TN)

        @pl.when(m == n_m - 1)
        def _():
            o_ref[...] = acc_ref[...].astype(BF)

    outs, cout = _pcall(
        body, name=name, grid=(N_DEV, n_m), in_specs=[a_spec, b_spec],
        out_specs=[pl.BlockSpec((None, ka, nb), lambda j, m: (j, 0, 0))],
        out_shape=[jax.ShapeDtypeStruct((N_DEV, ka, nb), BF)],
        scratch=[pltpu.VMEM((ka, nb), F32)], sem=("parallel", "arbitrary"), args=(a, b), carried=carried)
    return outs[0], cout


def _normproj(x, gain, w, name, carried=None):
    S, D = x.shape
    _, n, _ = w.shape
    tm = _tile(S, 512, 16)

    def body(x_ref, g_ref, w_ref, hn_ref, y_ref):
        @pl.when(pl.program_id(1) == 0)
        def _():
            xhat, _ = _rms_hat(x_ref[...])
            hn_ref[...] = (xhat * g_ref[...]).astype(BF)

        y_ref[...] = _dot(hn_ref[...], w_ref[...], NT)

    row = pl.BlockSpec((tm, D), lambda m, j: (m, 0))
    return _pcall(
        body, name=name, grid=(S // tm, N_DEV),
        in_specs=[row, pl.BlockSpec((1, D), lambda m, j: (0, 0)), pl.BlockSpec((None, n, D), lambda m, j: (j, 0, 0))],
        out_specs=[row, pl.BlockSpec((tm, n), lambda m, j: (m, j))],
        out_shape=[jax.ShapeDtypeStruct((S, D), BF), jax.ShapeDtypeStruct((S, N_DEV * n), F32)],
        sem=("parallel", "arbitrary"), args=(x, gain, w), carried=carried)


def _rowproj(x, t, w, name, carried=None):
    S, D = x.shape
    _, k, _ = w.shape
    tm = _tile(S, 512, 16)

    def body(x_ref, t_ref, w_ref, y_ref):
        j = pl.program_id(1)
        part = _dot(t_ref[...], w_ref[...])

        @pl.when(j == 0)
        def _():
            y_ref[...] = x_ref[...] + part

        @pl.when(j > 0)
        def _():
            y_ref[...] += part

    row = pl.BlockSpec((tm, D), lambda m, j: (m, 0))
    outs, cout = _pcall(
        body, name=name, grid=(S // tm, N_DEV),
        in_specs=[row, pl.BlockSpec((tm, k), lambda m, j: (m, j)), pl.BlockSpec((None, k, D), lambda m, j: (j, 0, 0))],
        out_specs=[row], out_shape=[jax.ShapeDtypeStruct((S, D), F32)],
        sem=("parallel", "arbitrary"), args=(x, t, w), carried=carried)
    return outs[0], cout


def _dgrad_row(dy, w, out_dtype, name):
    S, D = dy.shape
    _, k, _ = w.shape
    tm = _tile(S, 512, 16)

    def body(dy_ref, w_ref, dt_ref, dyb_ref):
        @pl.when(pl.program_id(1) == 0)
        def _():
            dyb_ref[...] = dy_ref[...].astype(BF)

        dt_ref[...] = _dot(dyb_ref[...], w_ref[...], NT).astype(out_dtype)

    outs, _ = _pcall(
        body, name=name, grid=(S // tm, N_DEV),
        in_specs=[pl.BlockSpec((tm, D), lambda m, j: (m, 0)), pl.BlockSpec((None, k, D), lambda m, j: (j, 0, 0))],
        out_specs=[pl.BlockSpec((tm, k), lambda m, j: (m, j))],
        out_shape=[jax.ShapeDtypeStruct((S, N_DEV * k), out_dtype)],
        scratch=[pltpu.VMEM((tm, D), BF)], sem=("parallel", "arbitrary"), args=(dy, w))
    return outs[0]


def _dgrad_col_norm(dres, x, gain, dz, w, name, carried=None):
    S, D = x.shape
    _, n, _ = w.shape
    tm = _tile(S, 512, 16)

    def body(dres_ref, x_ref, g_ref, dz_ref, w_ref, dx_ref, dg_ref, acc_ref):
        m, j = pl.program_id(0), pl.program_id(1)

        @pl.when(j == 0)
        def _():
            acc_ref[...] = jnp.zeros_like(acc_ref)

        acc_ref[...] += _dot(dz_ref[...], w_ref[...])

        @pl.when(j == N_DEV - 1)
        def _():
            dx, dgain = _rms_bwd(acc_ref[...], x_ref[...], g_ref[...], dres_ref[...])
            dx_ref[...] = dx
            _accum_rows(dg_ref, dgain, m == 0)

    row = pl.BlockSpec((tm, D), lambda m, j: (m, 0), pipeline_mode=pl.Buffered(1))
    return _pcall(
        body, name=name, grid=(S // tm, N_DEV),
        in_specs=[row, row, pl.BlockSpec((1, D), lambda m, j: (0, 0)), pl.BlockSpec((tm, n), lambda m, j: (m, j)),
                  pl.BlockSpec((None, n, D), lambda m, j: (j, 0, 0))],
        out_specs=[row, pl.BlockSpec((8, D), lambda m, j: (0, 0))],
        out_shape=[jax.ShapeDtypeStruct((S, D), F32), jax.ShapeDtypeStruct((8, D), F32)],
        scratch=[pltpu.VMEM((tm, D), F32)], sem=("arbitrary", "arbitrary"),
        args=(dres, x, gain, dz, w), carried=carried)


def _headnorm_fwd(xa, width, gain_row, scale, name, carried=None):
    S = xa.shape[0]
    cb = _tile(width, 1024, LANE)
    tm = _tile(S, 512, 8)

    def body(x_ref, g_ref, y_ref):
        for c in range(cb // LANE):
            sl = slice(c * LANE, (c + 1) * LANE)
            xhat, _ = _rms_hat(x_ref[:, sl])
            y_ref[:, sl] = xhat * (g_ref[:, sl] * scale)

    outs, couts = _pcall(
        body, name=name, grid=(S // tm, width // cb),
        in_specs=[pl.BlockSpec((tm, cb), lambda m, c: (m, c)), pl.BlockSpec((1, cb), lambda m, c: (0, c))],
        out_specs=[pl.BlockSpec((tm, cb), lambda m, c: (m, c))],
        out_shape=[jax.ShapeDtypeStruct((S, width), F32)], sem=("parallel", "parallel"), args=(xa, gain_row),
        carried=carried)
    return outs[0], couts


def _headnorm_bwd(dy, xa, gain_row, scale, name, tail=None):
    S, width = dy.shape
    cb = _tile(width, 1024, LANE)
    tm = _tile(S, 512, 16)
    ncb = width // cb
    ntail = 0 if tail is None else tail.shape[1] // cb

    def body(dy_ref, x_ref, g_ref, *rest):
        dx_ref, dg_ref = rest[-2:]
        c, m = pl.program_id(0), pl.program_id(1)

        @pl.when(c < ncb)
        def _():
            rows = []
            for i in range(cb // LANE):
                sl = slice(i * LANE, (i + 1) * LANE)
                dx, dgain = _rms_bwd(dy_ref[:, sl] * scale, x_ref[:, sl], g_ref[:, sl], 0.0)
                dx_ref[:, sl] = dx.astype(BF)
                rows.append(dgain)
            _accum_rows(dg_ref, jnp.concatenate(rows, axis=1), m == 0)

        if tail is not None:
            @pl.when(c >= ncb)
            def _():
                dx_ref[...] = rest[0][...].astype(BF)

    head = lambda c: jnp.minimum(c, ncb - 1)
    in_specs = [pl.BlockSpec((tm, cb), lambda c, m: (jnp.where(c < ncb, m, 0), head(c))),
                pl.BlockSpec((tm, cb), lambda c, m: (jnp.where(c < ncb, m, 0), head(c))),
                pl.BlockSpec((1, cb), lambda c, m: (0, head(c)))]
    args = [dy, xa, gain_row]
    if tail is not None:
        in_specs.append(pl.BlockSpec((tm, cb), lambda c, m: (jnp.where(c >= ncb, m, 0), jnp.maximum(c - ncb, 0))))
        args.append(tail)
    outs, _ = _pcall(
        body, name=name, grid=(ncb + ntail, S // tm), in_specs=in_specs,
        out_specs=[pl.BlockSpec((tm, cb), lambda c, m: (m, c)), pl.BlockSpec((8, cb), lambda c, m: (0, head(c)))],
        out_shape=[jax.ShapeDtypeStruct((S, width + ntail * cb), BF), jax.ShapeDtypeStruct((8, width), F32)],
        sem=("arbitrary", "arbitrary"), args=args)
    return outs


def _causal():
    p = lax.broadcasted_iota(jnp.int32, (LANE, LANE), 0)
    q = lax.broadcasted_iota(jnp.int32, (LANE, LANE), 1)
    return p >= q


def _gmlp_fwd(z, v_gain, ws, bias, carried=None):
    S, dg2 = z.shape
    dg = dg2 // 2
    G = dg // LANE

    def body(z_ref, vg_ref, ws_ref, bias_ref, t_ref):
        u, _ = _gelu(z_ref[:, :dg])
        v, _ = _gelu(z_ref[:, dg:])
        vhat, _ = _rms_hat(v)
        vn = (vhat * vg_ref[...]).astype(BF)
        mask = _causal()
        for g in range(G):
            sl = slice(g * LANE, (g + 1) * LANE)
            wm = jnp.where(mask, ws_ref[g], 0.0).astype(BF)
            sv = _dot(wm, vn[:, sl]) + bias_ref[:, sl]
            t_ref[:, sl] = (u[:, sl] * sv).astype(BF)

    outs, couts = _pcall(
        body, name="gmlp_fwd", grid=(S // LANE,),
        in_specs=[pl.BlockSpec((LANE, dg2), lambda n: (n, 0)), pl.BlockSpec((1, dg), lambda n: (0, 0)),
                  pl.BlockSpec((G, LANE, LANE), lambda n: (0, 0, 0)), pl.BlockSpec((LANE, dg), lambda n: (0, 0))],
        out_specs=[pl.BlockSpec((LANE, dg), lambda n: (n, 0))],
        out_shape=[jax.ShapeDtypeStruct((S, dg), BF)], sem=("parallel",), args=(z, v_gain, ws, bias),
        carried=carried)
    return outs[0], couts


def _gmlp_bwd(z, dt, v_gain, ws, bias):
    S, dg2 = z.shape
    dg = dg2 // 2
    G = dg // LANE

    def body(z_ref, dt_ref, vg_ref, ws_ref, bias_ref, dz_ref, dws_ref, db_ref, dvg_ref, dvn_ref):
        n = pl.program_id(0)
        zu, zv = z_ref[:, :dg], z_ref[:, dg:]
        u, tu = _gelu(zu)
        v, tv = _gelu(zv)
        vhat, r = _rms_hat(v)
        vn = (vhat * vg_ref[...]).astype(BF)
        mask = _causal()

        @pl.when(n == 0)
        def _():
            dws_ref[...] = jnp.zeros_like(dws_ref)
            db_ref[...] = jnp.zeros_like(db_ref)

        for g in range(G):
            sl = slice(g * LANE, (g + 1) * LANE)
            wm = jnp.where(mask, ws_ref[g], 0.0).astype(BF)
            sv = _dot(wm, vn[:, sl]) + bias_ref[:, sl]
            dtg = dt_ref[:, sl].astype(F32)
            dz_ref[:, sl] = (dtg * sv * _gelu_grad(zu[:, sl], tu[:, sl])).astype(BF)
            dsv = dtg * u[:, sl]
            dsvb = dsv.astype(BF)
            dvn_ref[:, sl] = _dot(wm, dsvb, TN)
            dws_ref[g] += jnp.where(mask, _dot(dsvb, vn[:, sl], NT), 0.0)
            db_ref[:, sl] += jnp.broadcast_to(jnp.sum(dsv, axis=1, keepdims=True), (LANE, LANE))

        dvn = dvn_ref[...]
        dxhat = dvn * vg_ref[...]
        dv = r * (dxhat - vhat * jnp.mean(dxhat * vhat, axis=-1, keepdims=True))
        dz_ref[:, dg:] = (dv * _gelu_grad(zv, tv)).astype(BF)
        _accum_rows(dvg_ref, jnp.sum(dvn * vhat, axis=0, keepdims=True), n == 0)

    outs, _ = _pcall(
        body, name="gmlp_bwd", grid=(S // LANE,),
        in_specs=[pl.BlockSpec((LANE, dg2), lambda n: (n, 0)), pl.BlockSpec((LANE, dg), lambda n: (n, 0)),
                  pl.BlockSpec((1, dg), lambda n: (0, 0)), pl.BlockSpec((G, LANE, LANE), lambda n: (0, 0, 0)),
                  pl.BlockSpec((LANE, dg), lambda n: (0, 0))],
        out_specs=[pl.BlockSpec((LANE, dg2), lambda n: (n, 0)), pl.BlockSpec((G, LANE, LANE), lambda n: (0, 0, 0)),
                   pl.BlockSpec((LANE, dg), lambda n: (0, 0)), pl.BlockSpec((8, dg), lambda n: (0, 0))],
        out_shape=[jax.ShapeDtypeStruct((S, dg2), BF), jax.ShapeDtypeStruct((G, LANE, LANE), F32),
                   jax.ShapeDtypeStruct((LANE, dg), F32), jax.ShapeDtypeStruct((8, dg), F32)],
        scratch=[pltpu.VMEM((LANE, dg), F32)], sem=("arbitrary",), args=(z, dt, v_gain, ws, bias))
    return outs


N_ENT = ATT_WIN // LANE


def _rows(ref, start, dil):
    return ref[pl.ds(start, LANE), :] if dil == 1 else ref[pl.ds(start, LANE, stride=dil), :]


def _rows_store(ref, start, dil, val):
    if dil == 1:
        ref[pl.ds(start, LANE), :] = val
    else:
        ref[pl.ds(start, LANE, stride=dil), :] = val


def _slope_times_dil(h, n_heads, dil, shape):
    hv = jnp.zeros(shape, F32) + (h + 1).astype(F32)
    return jnp.exp(hv * (-8.0 / n_heads * math.log(2.0))) * float(dil)


def _band_bias(h, n_heads, dil, has_prev):
    qi = lax.broadcasted_iota(jnp.int32, (LANE, 2 * LANE), 0)
    kj = lax.broadcasted_iota(jnp.int32, (LANE, 2 * LANE), 1)
    delta = qi + LANE - kj
    valid = (delta >= 0) & (delta <= LANE) & ((kj >= LANE) | has_prev)
    return jnp.where(valid, -_slope_times_dil(h, n_heads, dil, (LANE, 2 * LANE)) * delta.astype(F32), NEG)


def _attn_geom(S, H, g, dil):
    pb = LANE * dil
    nblk = ATT_WIN // pb
    nw = S // ATT_WIN
    win = lambda c0: pl.BlockSpec((ATT_WIN, LANE), lambda h, w: (w, c0 + h))
    prev = lambda c0: pl.BlockSpec((pb, LANE), lambda h, w: (jnp.maximum(w * nblk - 1, 0), c0 + h))
    nxt = lambda c0: pl.BlockSpec((pb, LANE), lambda h, w: (jnp.minimum((w + 1) * nblk, nw * nblk - 1), c0 + h))
    return pb, nblk, nw, g * H, win, prev, nxt


def _stage_band(dst, cur_ref, prev_ref, dil, pb, nblk):
    for blk in range(nblk):
        for r in range(dil):
            e = blk * dil + r
            dst[e, :LANE] = _rows(prev_ref, r, dil) if blk == 0 else _rows(cur_ref, (blk - 1) * pb + r, dil)
            dst[e, LANE:] = _rows(cur_ref, blk * pb + r, dil)


def _stage(dst, ref, dil, pb, nblk, lead=None):
    for blk in range(nblk):
        for r in range(dil):
            val = _rows(ref, blk * pb + r, dil)
            if lead is None:
                dst[blk * dil + r] = val
            else:
                dst[lead, blk * dil + r] = val


def _unstage(ref, src, dil, pb, nblk):
    for blk in range(nblk):
        for r in range(dil):
            _rows_store(ref, blk * pb + r, dil, src[blk * dil + r])


def _attn_fwd(q, k, kv, g, dil, carried=None):
    S = q.shape[0]
    H = q.shape[1] // (3 * LANE)
    pb, nblk, nw, col, win, prev, _ = _attn_geom(S, H, g, dil)
    vcol = 3 * H + col

    def body(q_ref, kc_ref, kp_ref, vc_ref, vp_ref, o_ref, l_ref, qs, ks, vs, os_, ls):
        h, w = pl.program_id(0), pl.program_id(1)
        _stage(qs, q_ref, dil, pb, nblk)
        _stage_band(ks, kc_ref, kp_ref, dil, pb, nblk)
        _stage_band(vs, vc_ref, vp_ref, dil, pb, nblk)

        def run(lo, hi, bias):
            def step(e, carry):
                s = _dot(qs[e].astype(BF), ks[e].astype(BF), NT) + bias
                mx = jnp.max(s, axis=-1, keepdims=True)
                p = jnp.exp(s - mx)
                l = jnp.sum(p, axis=-1, keepdims=True)
                os_[e] = _dot((p / l).astype(BF), vs[e].astype(BF))
                ls[e] = jnp.broadcast_to(mx + jnp.log(l), (LANE, LANE))
                return carry
            if hi > lo:
                lax.fori_loop(lo, hi, step, 0, unroll=True)

        run(0, dil, _band_bias(h, H, dil, w > 0))
        run(dil, N_ENT, _band_bias(h, H, dil, True))
        _unstage(o_ref, os_, dil, pb, nblk)
        _unstage(l_ref, ls, dil, pb, nblk)

    out = jax.ShapeDtypeStruct((S, H * LANE), F32)
    sq = pltpu.VMEM((N_ENT, LANE, LANE), F32)
    sk = pltpu.VMEM((N_ENT, 2 * LANE, LANE), F32)
    return _pcall(
        body, name="attn_fwd_d%d" % dil, grid=(H, nw),
        in_specs=[win(col), win(col), prev(col), win(vcol), prev(vcol)],
        out_specs=[win(0), win(0)], out_shape=[out, out], scratch=[sq, sk, sk, sq, sq],
        sem=("parallel", "parallel"), args=(q, k, k, kv, kv), carried=carried)


def _attn_combine(os_, ls_, carried=None):
    S, C = os_[0].shape
    tm = _tile(S, 256, 16)

    def body(o0, o1, o2, l0, l1, l2, o_ref, lse_ref):
        a, b, c = l0[...], l1[...], l2[...]
        mx = jnp.maximum(jnp.maximum(a, b), c)
        ea, eb, ec = jnp.exp(a - mx), jnp.exp(b - mx), jnp.exp(c - mx)
        den = ea + eb + ec
        o_ref[...] = ((ea * o0[...] + eb * o1[...] + ec * o2[...]) / den).astype(BF)
        lse_ref[...] = mx + jnp.log(den)

    blk = pl.BlockSpec((tm, C), lambda m: (m, 0))
    return _pcall(
        body, name="attn_combine", grid=(S // tm,), in_specs=[blk] * 6, out_specs=[blk, blk],
        out_shape=[jax.ShapeDtypeStruct((S, C), BF), jax.ShapeDtypeStruct((S, C), F32)],
        sem=("parallel",), args=(*os_, *ls_), carried=carried)


def _attn_delta(do, o):
    S, C = do.shape
    tm = _tile(S, 512, 16)

    def body(do_ref, o_ref, d_ref):
        for c in range(C // LANE):
            sl = slice(c * LANE, (c + 1) * LANE)
            prod = do_ref[:, sl].astype(BF).astype(F32) * o_ref[:, sl].astype(F32)
            d_ref[:, sl] = jnp.broadcast_to(jnp.sum(prod, axis=-1, keepdims=True), (tm, LANE))

    blk = pl.BlockSpec((tm, C), lambda m: (m, 0))
    outs, _ = _pcall(
        body, name="attn_delta", grid=(S // tm,), in_specs=[blk, blk], out_specs=[blk],
        out_shape=[jax.ShapeDtypeStruct((S, C), F32)], sem=("parallel",), args=(do, o))
    return outs[0]


def _attn_bwd_dq(q, k, kv, do, lse, dl, g, dil, dq_prev):
    S = q.shape[0]
    H = q.shape[1] // (3 * LANE)
    pb, nblk, nw, col, win, prev, _ = _attn_geom(S, H, g, dil)
    vcol = 3 * H + col

    def body(q_ref, kc_ref, kp_ref, vc_ref, vp_ref, do_ref, l_ref, d_ref, *rest):
        dq_ref, qs, ks, vs, dos, ls, ds_, dqs = rest[-8:]
        h, w = pl.program_id(0), pl.program_id(1)
        _stage(qs, q_ref, dil, pb, nblk)
        _stage_band(ks, kc_ref, kp_ref, dil, pb, nblk)
        _stage_band(vs, vc_ref, vp_ref, dil, pb, nblk)
        _stage(dos, do_ref, dil, pb, nblk)
        _stage(ls, l_ref, dil, pb, nblk)
        _stage(ds_, d_ref, dil, pb, nblk)

        def run(lo, hi, bias):
            def step(e, carry):
                kb = ks[e].astype(BF)
                s = _dot(qs[e].astype(BF), kb, NT) + bias
                p = jnp.exp(s - ls[e][:, :1])
                dp = _dot(dos[e].astype(BF), vs[e].astype(BF), NT)
                dsc = p * (dp - ds_[e][:, :1])
                dqs[e] = _dot(dsc.astype(BF), kb)
                return carry
            if hi > lo:
                lax.fori_loop(lo, hi, step, 0, unroll=True)

        run(0, dil, _band_bias(h, H, dil, w > 0))
        run(dil, N_ENT, _band_bias(h, H, dil, True))
        _unstage(dq_ref, dqs, dil, pb, nblk)

    sq = pltpu.VMEM((N_ENT, LANE, LANE), F32)
    sk = pltpu.VMEM((N_ENT, 2 * LANE, LANE), F32)
    in_specs = [win(col), win(col), prev(col), win(vcol), prev(vcol), win(0), win(0), win(0)]
    args = [q, k, k, kv, kv, do, lse, dl]
    aliases = {}
    if dq_prev is not None:
        in_specs.append(ANY)
        args.append(dq_prev)
        aliases = {8: 0}
    outs, _ = _pcall(
        body, name="attn_bwd_dq_d%d" % dil, grid=(H, nw), in_specs=in_specs,
        out_specs=[win(col)], out_shape=[jax.ShapeDtypeStruct((S, 3 * H * LANE), F32)],
        scratch=[sq, sk, sk, sq, sq, sq, sq], sem=("parallel", "parallel"), args=args, aliases=aliases)
    return outs[0]


def _attn_bwd_dkv(q, k, kv, do, lse, dl, g, dil, prev_out, accumulate):
    S = q.shape[0]
    H = q.shape[1] // (3 * LANE)
    pb, nblk, nw, col, win, _, nxt = _attn_geom(S, H, g, dil)
    vcol = 3 * H + col
    n_q = N_ENT + dil

    def body(k_ref, v_ref, qc_ref, qn_ref, doc_ref, don_ref, lc_ref, ln_ref, dc_ref, dn_ref, *rest):
        dk_ref, dv_ref, ks, vs, qs, dos, ls, ds_, dks, dvs = rest[-10:]
        h, w = pl.program_id(0), pl.program_id(1)
        _stage(ks, k_ref, dil, pb, nblk)
        _stage(vs, v_ref, dil, pb, nblk)
        for dst, cur, nx in ((qs, qc_ref, qn_ref), (dos, doc_ref, don_ref), (ls, lc_ref, ln_ref), (ds_, dc_ref, dn_ref)):
            _stage(dst, cur, dil, pb, nblk)
            for r in range(dil):
                dst[N_ENT + r] = _rows(nx, r, dil)
        qi = lax.broadcasted_iota(jnp.int32, (LANE, LANE), 0)
        kj = lax.broadcasted_iota(jnp.int32, (LANE, LANE), 1)
        sd = _slope_times_dil(h, H, dil, (LANE, LANE))
        bias_c = jnp.where(qi >= kj, -sd * (qi - kj).astype(F32), NEG)

        def run(lo, hi, has_next):
            bias_n = jnp.where((qi <= kj) & has_next, -sd * (qi + LANE - kj).astype(F32), NEG)

            def step(e, carry):
                kb = ks[e].astype(BF)
                vb = vs[e].astype(BF)
                dk = jnp.zeros((LANE, LANE), F32)
                dv = jnp.zeros((LANE, LANE), F32)
                for eq, bias in ((e, bias_c), (e + dil, bias_n)):
                    qb = qs[eq].astype(BF)
                    dob = dos[eq].astype(BF)
                    s = _dot(qb, kb, NT) + bias
                    p = jnp.exp(s - ls[eq][:, :1])
                    dp = _dot(dob, vb, NT)
                    dsc = p * (dp - ds_[eq][:, :1])
                    dv = dv + _dot(p.astype(BF), dob, TN)
                    dk = dk + _dot(dsc.astype(BF), qb, TN)
                dks[e] = dk
                dvs[e] = dv
                return carry
            if hi > lo:
                lax.fori_loop(lo, hi, step, 0, unroll=True)

        run(0, N_ENT - dil, True)
        run(N_ENT - dil, N_ENT, w < nw - 1)
        if accumulate:
            pk_ref, pv_ref = rest[0], rest[1]
            dk_ref[...] = pk_ref[...]
            dv_ref[...] = pv_ref[...]
            for blk in range(nblk):
                for r in range(dil):
                    e, start = blk * dil + r, blk * pb + r
                    _rows_store(dk_ref, start, dil, _rows(dk_ref, start, dil) + dks[e])
                    _rows_store(dv_ref, start, dil, _rows(dv_ref, start, dil) + dvs[e])
        else:
            _unstage(dk_ref, dks, dil, pb, nblk)
            _unstage(dv_ref, dvs, dil, pb, nblk)

    s1 = pltpu.VMEM((N_ENT, LANE, LANE), F32)
    s2 = pltpu.VMEM((n_q, LANE, LANE), F32)
    out = jax.ShapeDtypeStruct((S, 3 * H * LANE), F32)
    in_specs = [win(col), win(vcol), win(col), nxt(col), win(0), nxt(0), win(0), nxt(0), win(0), nxt(0)]
    args = [k, kv, q, q, do, do, lse, lse, dl, dl]
    aliases = {}
    if prev_out is not None:
        in_specs += [win(col), win(col)] if accumulate else [ANY, ANY]
        args += list(prev_out)
        aliases = {10: 0, 11: 1}
    outs, _ = _pcall(
        body, name="attn_bwd_dkv_d%d%s" % (dil, "_acc" if accumulate else ""), grid=(H, nw), in_specs=in_specs,
        out_specs=[win(col), win(col)], out_shape=[out, out],
        scratch=[s1, s1, s2, s2, s2, s2, s1, s1], sem=("parallel", "parallel"), vmem=56, args=args, aliases=aliases)
    return outs


def _loss_head(y, target):
    S, D = y.shape
    tm = _tile(S, 512, 8)

    def body(y_ref, t_ref, dy_ref, l_ref):
        e = y_ref[...] - t_ref[...]
        dy_ref[...] = e * (1.0 / D)
        part = jnp.broadcast_to(jnp.sum(jnp.sum(e * e, axis=1, keepdims=True), axis=0, keepdims=True) * (0.5 / D), (8, LANE))
        _accum_rows(l_ref, part, pl.program_id(0) == 0)

    blk = pl.BlockSpec((tm, D), lambda m: (m, 0))
    outs, _ = _pcall(
        body, name="loss_head", grid=(S // tm,), in_specs=[blk, blk],
        out_specs=[blk, pl.BlockSpec((8, LANE), lambda m: (0, 0))],
        out_shape=[jax.ShapeDtypeStruct((S, D), F32), jax.ShapeDtypeStruct((8, LANE), F32)],
        sem=("arbitrary",), args=(y, target))
    return outs


class _Queue:
    def __init__(self):
        self.items = []

    def push(self, kind, key, arr):
        self.items.append((kind, key, arr))

    def take(self, budget_us):
        taken, spent = [], 0.0
        while self.items and spent < budget_us:
            item = self.items.pop(0)
            taken.append(item)
            spent += COST_US_PER_ELEM[item[0]] * (item[2].size / item[2].shape[0] if item[0] != "gather" else item[2].size)
        return taken

    def take_keys(self, keys):
        taken = [it for it in self.items if it[1] in keys]
        self.items = [it for it in self.items if it[1] not in keys]
        return taken

    def take_kind(self, kind):
        taken = [it for it in self.items if it[0] == kind]
        self.items = [it for it in self.items if it[0] != kind]
        return taken


def kernel(x, ffn1_norm, ffn1_w_gate, ffn1_w_up, ffn1_w_down, mix_norm, ffn2_norm, ffn2_w_gate, ffn2_w_up, ffn2_w_down, gmlp_w_in, gmlp_v_norm, gmlp_w_s, gmlp_b_s, gmlp_w_out, kv_norm, w_kv, k_norm, attn_w_q, attn_q_norm, attn_w_o, loss_target, m_ffn1_norm, m_ffn1_w_gate, m_ffn1_w_up, m_ffn1_w_down, m_mix_norm, m_ffn2_norm, m_ffn2_w_gate, m_ffn2_w_up, m_ffn2_w_down, m_gmlp_w_in, m_gmlp_v_norm, m_gmlp_w_s, m_gmlp_b_s, m_gmlp_w_out, m_kv_norm, m_w_kv, m_k_norm, m_attn_w_q, m_attn_q_norm, m_attn_w_o, v_ffn1_norm, v_ffn1_w_gate, v_ffn1_w_up, v_ffn1_w_down, v_mix_norm, v_ffn2_norm, v_ffn2_w_gate, v_ffn2_w_up, v_ffn2_w_down, v_gmlp_w_in, v_gmlp_v_norm, v_gmlp_w_s, v_gmlp_b_s, v_gmlp_w_out, v_kv_norm, v_w_kv, v_k_norm, v_attn_w_q, v_attn_q_norm, v_attn_w_o):
    names = ["ffn1_norm", "ffn1_w_gate", "ffn1_w_up", "ffn1_w_down", "mix_norm", "ffn2_norm", "ffn2_w_gate",
             "ffn2_w_up", "ffn2_w_down", "gmlp_w_in", "gmlp_v_norm", "gmlp_w_s", "gmlp_b_s", "gmlp_w_out",
             "kv_norm", "w_kv", "k_norm", "attn_w_q", "attn_q_norm", "attn_w_o"]
    W = dict(zip(names, [ffn1_norm, ffn1_w_gate, ffn1_w_up, ffn1_w_down, mix_norm, ffn2_norm, ffn2_w_gate,
                         ffn2_w_up, ffn2_w_down, gmlp_w_in, gmlp_v_norm, gmlp_w_s, gmlp_b_s, gmlp_w_out,
                         kv_norm, w_kv, k_norm, attn_w_q, attn_q_norm, attn_w_o]))
    M = dict(zip(names, [m_ffn1_norm, m_ffn1_w_gate, m_ffn1_w_up, m_ffn1_w_down, m_mix_norm, m_ffn2_norm, m_ffn2_w_gate,
                         m_ffn2_w_up, m_ffn2_w_down, m_gmlp_w_in, m_gmlp_v_norm, m_gmlp_w_s, m_gmlp_b_s, m_gmlp_w_out,
                         m_kv_norm, m_w_kv, m_k_norm, m_attn_w_q, m_attn_q_norm, m_attn_w_o]))
    V = dict(zip(names, [v_ffn1_norm, v_ffn1_w_gate, v_ffn1_w_up, v_ffn1_w_down, v_mix_norm, v_ffn2_norm, v_ffn2_w_gate,
                         v_ffn2_w_up, v_ffn2_w_down, v_gmlp_w_in, v_gmlp_v_norm, v_gmlp_w_s, v_gmlp_b_s, v_gmlp_w_out,
                         v_kv_norm, v_w_kv, v_k_norm, v_attn_w_q, v_attn_q_norm, v_attn_w_o]))

    depth = ffn1_norm.shape[0]
    n_a = gmlp_w_in.shape[0]
    S, D = x.shape[1], x.shape[2]
    H = D // LANE
    n_grp = len(DILATIONS)
    hw = H * LANE
    xi, yi, ci = _me()
    core = jnp.reshape(ci, (1,)).astype(jnp.int32)
    chip = jnp.reshape(2 * xi + yi, (1,)).astype(jnp.int32)
    dev = 4 * xi + 2 * yi + ci
    q_scale = LANE ** -0.5
    ffn_names = (("f1", ("ffn1_norm", "ffn1_w_gate", "ffn1_w_up", "ffn1_w_down")),
                 ("f2", ("ffn2_norm", "ffn2_w_gate", "ffn2_w_up", "ffn2_w_down")))
    transposed = ("ffn1_w_gate", "ffn1_w_up", "ffn2_w_gate", "ffn2_w_up")

    def gain(v):
        return v.reshape(1, -1)

    def head_gain(g3):
        return jnp.tile(g3[:, None, :], (1, H, 1)).reshape(1, n_grp * hw)

    def bf_t(w):
        return jnp.swapaxes(w, 0, 1).astype(BF)

    gq = _Queue()
    gathered = {}
    gq.push("gather", "v_norm", jnp.pad(gmlp_v_norm, ((0, 8 - n_a), (0, 0))))
    for l in range(depth):
        for tag, (_, wgn, wun, wdn) in ffn_names:
            if tag == "f2":
                if l < n_a:
                    gq.push("gather", ("gmlp_w_in", l), bf_t(gmlp_w_in[l]))
                    gq.push("gather", ("gmlp_w_out", l), gmlp_w_out[l].astype(BF))
                else:
                    gq.push("gather", ("attn_w_q", l - n_a), bf_t(attn_w_q[l - n_a]))
                    gq.push("gather", ("attn_w_o", l - n_a), attn_w_o[l - n_a].astype(BF))
            gq.push("gather", (wgn, l), bf_t(W[wgn][l]))
            gq.push("gather", (wun, l), bf_t(W[wun][l]))
            gq.push("gather", (wdn, l), W[wdn][l].astype(BF))
        if l == n_a - 1:
            gq.push("gather", "w_kv", bf_t(w_kv))

    def land(items, outs):
        for (_, key, _), o in zip(items, outs):
            gathered[key] = o

    def need(*keys):
        items = gq.take_keys([k for k in keys if k not in gathered])
        if items:
            land(items, _comm_only(_Carried([(k, a) for k, _, a in items]), "allgather"))
        return [gathered[k] for k in keys]

    def carry(q, kind):
        items = q.take(BUDGET_US[kind])
        return items, _Carried([(k, a) for k, _, a in items])

    v_all = need("v_norm", *[(n, 0) for n in ffn_names[0][1][1:]])[0]
    v_gain_all = jnp.transpose(v_all[:, :n_a], (1, 0, 2)).reshape(n_a, -1)

    cur = x.reshape(S, D)
    saved = []
    k_sh = kv_raw = kv_hn = kv_x = k_gain = None

    for l in range(depth):
        rec = {}
        for tag, (nn, wgn, wun, wdn) in ffn_names:
            if tag == "f2":
                rec["mix_x"] = cur
                if l < n_a:
                    w_in, w_out = need(("gmlp_w_in", l), ("gmlp_w_out", l))
                    bias = jnp.repeat(gmlp_b_s[l].T, LANE, axis=1)
                    items, car = carry(gq, "normproj")
                    (hm, z), couts = _normproj(cur, gain(mix_norm[l]), w_in, "gmlp_in", car)
                    land(items, couts)
                    items, car = carry(gq, "gmlp_fwd")
                    t, couts = _gmlp_fwd(z, gain(v_gain_all[l]), gmlp_w_s[l], bias, car)
                    land(items, couts)
                    items, car = carry(gq, "rowproj")
                    cur, couts = _rowproj(cur, t, w_out, "proj_out", car)
                    land(items, couts)
                    rec.update(w_in=w_in, w_out=w_out, bias=bias, hm=hm, z=z, t=t)
                else:
                    jj = l - n_a
                    w_q, w_o = need(("attn_w_q", jj), ("attn_w_o", jj))
                    items, car = carry(gq, "normproj")
                    (hm, q_raw), couts = _normproj(cur, gain(mix_norm[l]), w_q, "attn_q", car)
                    land(items, couts)
                    qg = head_gain(attn_q_norm[jj])
                    items, car = carry(gq, "headnorm")
                    q, couts = _headnorm_fwd(q_raw, n_grp * hw, qg, q_scale, "headnorm_q", car)
                    land(items, couts)
                    outs = []
                    for g, dil in enumerate(DILATIONS):
                        items, car = carry(gq, "attn_fwd")
                        og, couts = _attn_fwd(q, k_sh, kv_raw, g, dil, car)
                        land(items, couts)
                        outs.append(og)
                    items, car = carry(gq, "attn_combine")
                    (o, lse), couts = _attn_combine([o_ for o_, _ in outs], [l_ for _, l_ in outs], car)
                    land(items, couts)
                    items, car = carry(gq, "rowproj")
                    cur, couts = _rowproj(cur, o, w_o, "proj_out", car)
                    land(items, couts)
                    rec.update(w_q=w_q, w_o=w_o, hm=hm, q_raw=q_raw, qg=qg, q=q, o=o, lse=lse)
            wg, wu, wd = need((wgn, l), (wun, l), (wdn, l))
            rec[tag + "_x"] = cur
            items, car = carry(gq, "ffn_fwd")
            (cur, hn, act, ga, gb), couts = _ffn_fwd(cur, gain(W[nn][l]), wg, wu, wd, car)
            land(items, couts)
            rec[tag] = (wg, wu, wd, hn, act, ga, gb)
        if l == n_a - 1:
            (w_kv_g,) = need("w_kv")
            kv_x = cur
            items, car = carry(gq, "normproj")
            (kv_hn, kv_raw), couts = _normproj(cur, gain(kv_norm), w_kv_g, "kv_proj", car)
            land(items, couts)
            k_gain = head_gain(k_norm)
            items, car = carry(gq, "headnorm")
            k_sh, couts = _headnorm_fwd(kv_raw, n_grp * hw, k_gain, 1.0, "headnorm_k", car)
            land(items, couts)
        saved.append(rec)

    dcur, loss_part = _loss_head(cur, loss_target.reshape(S, D))
    loss = lax.psum(loss_part[0, 0], ("x", "y", "c"))

    rq = _Queue()
    reduced = {}
    small = {}

    def rs_land(items, outs):
        for (kind, key, arr), o in zip(items, outs):
            if kind == "core":
                rq.push("chip", key, _pair_add(arr, o, core))
            else:
                reduced[key] = (arr, o)

    def rs_carry(kind):
        items = rq.take(BUDGET_US[kind])
        return items, _Carried([(k, a) for k, _, a in items])

    def wgrad(key, a, a_kind, b, b_kind, name):
        items, car = rs_carry("wgrad")
        p, couts = _wgrad(a, a_kind, b, b_kind, name, car)
        rs_land(items, couts)
        rq.push("core", key, p)

    def put(name, l, val, n_layers):
        small.setdefault(name, [None] * n_layers)[l] = val

    dk_dv = None
    for l in reversed(range(depth)):
        rec = saved[l]
        if l == n_a - 1:
            dkv, dkg = _headnorm_bwd(dk_dv[0], kv_raw, k_gain, 1.0, "headnorm_k_bwd", tail=dk_dv[1])
            wgrad(("w_kv", 0), kv_hn, "full", dkv, "cols", "wgrad_kv")
            items, car = rs_carry("dgrad_col_norm")
            (dcur, dg), couts = _dgrad_col_norm(dcur, kv_x, gain(kv_norm), dkv, gathered["w_kv"], "dgrad_kv", car)
            rs_land(items, couts)
            small["kv_norm"] = [dg[0]]
            small["k_norm"] = [dkg[0].reshape(n_grp, H, LANE).sum(axis=1)]
        for tag, (nn, wgn, wun, wdn) in reversed(ffn_names):
            wg, wu, wd, hn, act, ga, gb = rec[tag]
            items, car = rs_carry("ffn_bwd_dx")
            (dcur, da, db, dyh, dg), couts = _ffn_bwd_dx(dcur, rec[tag + "_x"], gain(W[nn][l]), ga, gb, wg, wu, wd, car)
            rs_land(items, couts)
            wgrad((wgn, l), da, "stack", hn, "full", "wgrad_ffn_in")
            wgrad((wun, l), db, "stack", hn, "full", "wgrad_ffn_in")
            wgrad((wdn, l), act, "stack", dyh, "full", "wgrad_ffn_out")
            put(nn, l, dg[0], depth)
            if tag == "f2":
                mix_x = rec["mix_x"]
                if l < n_a:
                    dt = _dgrad_row(dcur, rec["w_out"], BF, "dgrad_gmlp_out")
                    wgrad(("gmlp_w_out", l), rec["t"], "cols", dcur, "full", "wgrad_proj_out")
                    dz, dws, dbias, dvg = _gmlp_bwd(rec["z"], dt, gain(v_gain_all[l]), gmlp_w_s[l], rec["bias"])
                    wgrad(("gmlp_w_in", l), rec["hm"], "full", dz, "cols", "wgrad_gmlp_in")
                    items, car = rs_carry("dgrad_col_norm")
                    (dcur, dg), couts = _dgrad_col_norm(dcur, mix_x, gain(mix_norm[l]), dz, rec["w_in"], "dgrad_gmlp_in", car)
                    rs_land(items, couts)
                    put("gmlp_w_s", l, dws, n_a)
                    put("gmlp_b_s", l, dbias[:, ::LANE].T, n_a)
                    put("gmlp_v_norm", l, dvg[0], n_a)
                else:
                    jj = l - n_a
                    do = _dgrad_row(dcur, rec["w_o"], F32, "dgrad_attn_out")
                    wgrad(("attn_w_o", jj), rec["o"], "cols", dcur, "full", "wgrad_proj_out")
                    dl = _attn_delta(do, rec["o"])
                    dq = None
                    first_layer = dk_dv is None
                    for g, dil in enumerate(DILATIONS):
                        dq = _attn_bwd_dq(rec["q"], k_sh, kv_raw, do, rec["lse"], dl, g, dil, dq)
                        dk_dv = _attn_bwd_dkv(rec["q"], k_sh, kv_raw, do, rec["lse"], dl, g, dil, dk_dv,
                                              accumulate=not first_layer)
                    dq_raw, dqg = _headnorm_bwd(dq, rec["q_raw"], rec["qg"], q_scale, "headnorm_q_bwd")
                    wgrad(("attn_w_q", jj), rec["hm"], "full", dq_raw, "cols", "wgrad_attn_q")
                    items, car = rs_carry("dgrad_col_norm")
                    (dcur, dg), couts = _dgrad_col_norm(dcur, mix_x, gain(mix_norm[l]), dq_raw, rec["w_q"], "dgrad_attn_q", car)
                    rs_land(items, couts)
                    put("attn_q_norm", jj, dqg[0].reshape(n_grp, H, LANE).sum(axis=1), depth - n_a)
                put("mix_norm", l, dg[0], depth)
    grad_x = dcur.reshape(1, S, D)

    for kind in ("core", "chip"):
        items = rq.take_kind(kind)
        if items:
            rs_land(items, _comm_only(_Carried([(k, a) for k, _, a in items]), "rs_%s_exchange" % kind))

    out_g, out_d, out_m, out_v = {}, {}, {}, {}
    for name in names:
        if (name, 0) not in reduced:
            continue
        if name in transposed:
            as3 = lambda t: jnp.swapaxes(t, 1, 2)
        else:
            as3 = (lambda t: t[None]) if W[name].ndim == 2 else (lambda t: t)
        res = None
        for l in range(as3(W[name]).shape[0]):
            qsum, r2 = reduced[(name, l)]
            res = _adamw_shard(qsum, r2, chip, as3(W[name]), as3(M[name]), as3(V[name]), l, res)
        for dct, val in zip((out_g, out_d, out_m, out_v), res):
            dct[name] = jnp.swapaxes(val, 1, 2) if name in transposed else (val[0] if W[name].ndim == 2 else val)

    small_names = [n for n in names if n in small]
    full_shape = {n: (W[n].shape if n != "gmlp_v_norm" else (n_a, v_gain_all.shape[1])) for n in small_names}
    flat = jnp.concatenate([jnp.stack(small[n]).reshape(-1) if W[n].ndim > 1 else small[n][0].reshape(-1)
                            for n in small_names])
    n_flat = flat.shape[0]
    rows = -(-n_flat // (8 * LANE)) * 8

    def pack(parts_list):
        v = jnp.concatenate([p.reshape(-1) for p in parts_list])
        return jnp.pad(v, (0, rows * LANE - n_flat)).reshape(rows, LANE)

    def full_of(dct, n, fill):
        if n != "gmlp_v_norm":
            return dct[n]
        sh = dct[n].shape[1]
        return lax.dynamic_update_slice(jnp.full(full_shape[n], fill, F32), dct[n], (0, dev * sh))

    (g_all,) = _comm_only(_Carried([("gather", pack([flat]))]), "allgather_small_grads")
    w_p = pack([full_of(W, n, 0.0) for n in small_names])
    m_p = pack([full_of(M, n, 0.0) for n in small_names])
    v_p = pack([full_of(V, n, 1.0) for n in small_names])
    packed = _adamw_replicated(g_all, w_p, m_p, v_p)
    offs = 0
    for n in small_names:
        size = math.prod(full_shape[n])
        for dct, arr in zip((out_g, out_d, out_m, out_v), packed):
            val = arr.reshape(-1)[offs:offs + size].reshape(full_shape[n])
            if n == "gmlp_v_norm":
                sh = W[n].shape[1]
                val = lax.dynamic_slice(val, (0, dev * sh), (n_a, sh))
            dct[n] = val
        offs += size

    return (loss, grad_x, *[out_g[n] for n in names], *[out_d[n] for n in names],
            *[out_m[n] for n in names], *[out_v[n] for n in names])
```

```python
import math

import jax
import jax.numpy as jnp
from jax import lax
from jax.experimental import pallas as pl
from jax.experimental.pallas import tpu as pltpu

F32 = jnp.float32
BF = jnp.bfloat16
N_DEV = 8
N_CHIP = 4
LANE = 128
ATT_WIN = 16 * LANE
EPS = 1e-6
NEG = -1e30
DILATIONS = (1, 4, 16)
ADAM_LR, ADAM_B1, ADAM_B2, ADAM_EPS, ADAM_WD, ADAM_STEP = 0.001, 0.9, 0.999, 1e-08, 0.01, 10
GELU_C0, GELU_C1 = 0.7978845608028654, 0.044715
VMEM_MB = 2 ** 20
BUDGET_US = {"ffn_fwd": 380.0, "normproj": 150.0, "rowproj": 100.0, "gmlp_fwd": 50.0, "headnorm": 75.0,
             "attn_fwd": 120.0, "attn_combine": 75.0,
             "ffn_bwd_dx": 480.0, "wgrad": 110.0, "dgrad_col_norm": 250.0}
COST_US_PER_ELEM = {"gather": 0.8e-4, "core": 1.4e-5, "chip": 9.0e-5}

MESH_T = pl.DeviceIdType.MESH
ANY = pl.BlockSpec(memory_space=pl.ANY)
DMA_SEM = pltpu.SemaphoreType.DMA
NT = (((1,), (1,)), ((), ()))
TN = (((0,), (0,)), ((), ()))


def _tile(n, target, mult):
    best = None
    for t in range(mult, min(n, target) + 1, mult):
        if n % t == 0:
            best = t
    if best is None:
        best = n
    return best


def _dot(a, b, dims=None):
    if dims is None:
        return jnp.dot(a, b, preferred_element_type=F32)
    return lax.dot_general(a, b, dims, preferred_element_type=F32)


def _rms_hat(xv):
    r = lax.rsqrt(jnp.mean(xv * xv, axis=-1, keepdims=True) + EPS)
    return xv * r, r


def _rms_bwd(dhn, xv, gain, dres):
    xhat, r = _rms_hat(xv)
    dxhat = dhn * gain
    dx = dres + r * (dxhat - xhat * jnp.mean(dxhat * xhat, axis=-1, keepdims=True))
    return dx, jnp.sum(dhn * xhat, axis=0, keepdims=True)


def _accum_rows(ref, row, first):
    val = jnp.broadcast_to(row, ref.shape)

    @pl.when(first)
    def _():
        ref[...] = val

    @pl.when(jnp.logical_not(first))
    def _():
        ref[...] += val


def _gelu(z):
    t = jnp.tanh(GELU_C0 * (z + GELU_C1 * z * z * z))
    return 0.5 * z * (1.0 + t), t


def _gelu_grad(z, t):
    return 0.5 * (1.0 + t) + 0.5 * z * (1.0 - t * t) * GELU_C0 * (1.0 + 3.0 * GELU_C1 * z * z)


def _me():
    return lax.axis_index("x"), lax.axis_index("y"), lax.axis_index("c")


def _gather_phase(phase, x_ref, out_ref, send_sems, recv_sems, local_sem):
    x, y, c = _me()
    me, sibling = (x, y, c), (x, y, 1 - c)
    x_nbr, y_nbr, diag = (1 - x, y), (x, 1 - y), (1 - x, 1 - y)
    src_chip = (x + (1 - 2 * x) * (1 - c), y + (1 - 2 * y) * c)
    dst_chip = (x + (1 - 2 * x) * c, y + (1 - 2 * y) * (1 - c))

    def slot(px, py, pc):
        return out_ref.at[4 * px + 2 * py + pc]

    def copy(k, block, to, src=None):
        return pltpu.make_async_remote_copy(
            src_ref=slot(*block) if src is None else src, dst_ref=slot(*block),
            send_sem=send_sems.at[k], recv_sem=recv_sems.at[k], device_id=to, device_id_type=MESH_T)

    mine = pltpu.make_async_copy(x_ref, slot(*me), local_sem)
    first = [copy(0, me, sibling, src=x_ref), copy(1, me, (*x_nbr, c), src=x_ref), copy(2, me, (*y_nbr, c), src=x_ref)]
    second = [copy(3, (*x_nbr, c), sibling), copy(4, (*y_nbr, c), sibling), copy(5, (*src_chip, c), (*dst_chip, c))]
    third = [copy(6, (*diag, c), sibling)]
    if phase == 0:
        mine.start()
        for cp in first:
            cp.start()
    elif phase == 1:
        copy(1, (*x_nbr, c), me).wait_recv()
        copy(2, (*y_nbr, c), me).wait_recv()
        for cp in second:
            cp.start()
    elif phase == 2:
        copy(5, (*diag, c), me).wait_recv()
        third[0].start()
    else:
        copy(0, sibling, me).wait_recv()
        copy(3, (*x_nbr, 1 - c), me).wait_recv()
        copy(4, (*y_nbr, 1 - c), me).wait_recv()
        copy(6, (*diag, 1 - c), me).wait_recv()
        for cp in first + second + third:
            cp.wait_send()
        mine.wait()


def _exchange_phase(phase, kind, src_ref, dst_ref, send_sems, recv_sems):
    x, y, c = _me()
    if kind == "core":
        plan = [(2 * k + (1 - c), k, (x, y, 1 - c)) for k in range(N_CHIP)]
    else:
        plan = [(2 * px + py, t, (px, py, c)) for t, (px, py) in enumerate([(1 - x, y), (x, 1 - y), (1 - x, 1 - y)])]
    cps = [pltpu.make_async_remote_copy(
        src_ref=src_ref.at[s], dst_ref=dst_ref.at[d], send_sem=send_sems.at[i], recv_sem=recv_sems.at[i],
        device_id=to, device_id_type=MESH_T) for i, (s, d, to) in enumerate(plan)]
    if phase == 0:
        for cp in cps:
            cp.start()
    elif phase == N_PHASES - 1:
        for cp in cps:
            cp.wait()


_N_COPIES = {"gather": 7, "core": N_CHIP, "chip": 3}
N_PHASES = 4


class _Carried:
    def __init__(self, items=()):
        self.items = list(items)

    def arrays(self):
        return [a for _, a in self.items]

    def out_shapes(self):
        lead = {"gather": lambda a: (N_DEV,) + a.shape, "core": lambda a: (N_CHIP,) + a.shape[1:],
                "chip": lambda a: (3,) + a.shape[1:]}
        return [jax.ShapeDtypeStruct(lead[k](a), a.dtype) for k, a in self.items]

    def scratch(self):
        res = []
        for k, _ in self.items:
            res += [DMA_SEM((_N_COPIES[k],)), DMA_SEM((_N_COPIES[k],))]
            if k == "gather":
                res.append(DMA_SEM(()))
        return res

    def emit(self, phase, in_refs, out_refs, scr):
        i = 0
        for (kind, _), src, dst in zip(self.items, in_refs, out_refs):
            if kind == "gather":
                _gather_phase(phase, src, dst, scr[i], scr[i + 1], scr[i + 2])
                i += 3
            else:
                _exchange_phase(phase, kind, src, dst, scr[i], scr[i + 1])
                i += 2


def _comm_only(carried, name):
    nc = len(carried.items)

    def body(*refs):
        for phase in range(N_PHASES):
            carried.emit(phase, refs[:nc], refs[nc:2 * nc], refs[2 * nc:])

    return pl.pallas_call(
        body, name=name, out_shape=carried.out_shapes(), in_specs=[ANY] * nc, out_specs=[ANY] * nc,
        scratch_shapes=carried.scratch(),
    )(*carried.arrays())


def _pcall(main, *, name, grid, in_specs, out_specs, out_shape, args, scratch=(), sem=None, vmem=48,
           carried=None, aliases=None):
    params = pltpu.CompilerParams(dimension_semantics=sem, vmem_limit_bytes=vmem * VMEM_MB)
    n_in, n_out, n_scr = len(in_specs), len(out_specs), len(scratch)
    if carried is None or not carried.items:
        outs = pl.pallas_call(
            main, name=name, grid=grid, in_specs=in_specs, out_specs=out_specs, out_shape=out_shape,
            scratch_shapes=list(scratch), compiler_params=params, input_output_aliases=aliases or {},
        )(*args)
        return list(outs), []
    nc = len(carried.items)
    total = math.prod(grid)

    def body(*refs):
        ins, cin = refs[:n_in], refs[n_in:n_in + nc]
        o0 = n_in + nc
        outs, cout = refs[o0:o0 + n_out], refs[o0 + n_out:o0 + n_out + nc]
        s0 = o0 + n_out + nc
        scr, cscr = refs[s0:s0 + n_scr], refs[s0 + n_scr:]
        step = 0
        for d, n in enumerate(grid):
            step = step * n + pl.program_id(d)

        @pl.when(step == 0)
        def _():
            carried.emit(0, cin, cout, cscr)

        main(*ins, *outs, *scr)

        for phase, at in ((1, (5 * total) // 8), (2, (7 * total) // 8), (3, total - 1)):
            @pl.when(step == min(at, total - 1))
            def _(phase=phase):
                carried.emit(phase, cin, cout, cscr)

    outs = pl.pallas_call(
        body, name=name, grid=grid, in_specs=list(in_specs) + [ANY] * nc, out_specs=list(out_specs) + [ANY] * nc,
        out_shape=list(out_shape) + carried.out_shapes(), scratch_shapes=list(scratch) + carried.scratch(),
        compiler_params=params, input_output_aliases=aliases or {},
    )(*args, *carried.arrays())
    return list(outs[:n_out]), list(outs[n_out:])


def _pair_add(p, r1, core):
    _, rows, cols = p.shape
    tr = _tile(rows, 512, 16)
    p4 = p.reshape(N_CHIP, 2, rows, cols)

    def body(core_ref, p_ref, r_ref, q_ref):
        q_ref[...] = (p_ref[...].astype(F32) + r_ref[...].astype(F32)).astype(BF)

    grid_spec = pltpu.PrefetchScalarGridSpec(
        num_scalar_prefetch=1, grid=(N_CHIP, rows // tr),
        in_specs=[pl.BlockSpec((None, None, tr, cols), lambda k, i, cr: (k, cr[0], i, 0)),
                  pl.BlockSpec((None, tr, cols), lambda k, i, cr: (k, i, 0))],
        out_specs=pl.BlockSpec((None, tr, cols), lambda k, i, cr: (k, i, 0)))
    return pl.pallas_call(
        body, name="pair_add", grid_spec=grid_spec, out_shape=jax.ShapeDtypeStruct((N_CHIP, rows, cols), BF),
        compiler_params=pltpu.CompilerParams(dimension_semantics=("parallel", "parallel")),
    )(core, p4, r1)


def _adam_math(g, w, m, v):
    m2 = ADAM_B1 * m + (1.0 - ADAM_B1) * g
    v2 = ADAM_B2 * v + (1.0 - ADAM_B2) * (g * g)
    m_hat = m2 / (1.0 - ADAM_B1 ** ADAM_STEP)
    v_hat = v2 / (1.0 - ADAM_B2 ** ADAM_STEP)
    delta = -ADAM_LR * (m_hat / (jnp.sqrt(v_hat) + ADAM_EPS) + ADAM_WD * w)
    return delta, m2, v2


def _adamw_shard(q, r2, chip, w, m, v, layer, prev):
    n_layers, rows, cols = w.shape
    tr = _tile(rows, 256, 16)

    def body(chip_ref, q_ref, r_ref, w_ref, m_ref, v_ref, *rest):
        g_ref, d_ref, m2_ref, v2_ref = rest[-4:]
        g = q_ref[...].astype(F32) + r_ref[0].astype(F32) + r_ref[1].astype(F32) + r_ref[2].astype(F32)
        d, m2, v2 = _adam_math(g, w_ref[...], m_ref[...], v_ref[...])
        g_ref[...] = g
        d_ref[...] = d
        m2_ref[...] = m2
        v2_ref[...] = v2

    blk = pl.BlockSpec((None, tr, cols), lambda i, cr: (layer, i, 0))
    in_specs = [pl.BlockSpec((None, tr, cols), lambda i, cr: (cr[0], i, 0)),
                pl.BlockSpec((3, tr, cols), lambda i, cr: (0, i, 0)), blk, blk, blk]
    args = [chip, q, r2, w, m, v]
    aliases = {}
    if prev is not None:
        in_specs += [ANY] * 4
        args += list(prev)
        aliases = {6 + i: i for i in range(4)}
    grid_spec = pltpu.PrefetchScalarGridSpec(
        num_scalar_prefetch=1, grid=(rows // tr,), in_specs=in_specs, out_specs=[blk, blk, blk, blk])
    out = jax.ShapeDtypeStruct((n_layers, rows, cols), F32)
    return pl.pallas_call(
        body, name="adamw_shard", grid_spec=grid_spec, out_shape=[out, out, out, out], input_output_aliases=aliases,
        compiler_params=pltpu.CompilerParams(dimension_semantics=("parallel",)),
    )(*args)


def _adamw_replicated(parts, w, m, v):
    rows, cols = w.shape
    tr = _tile(rows, 512, 8)

    def body(p_ref, w_ref, m_ref, v_ref, g_ref, d_ref, m2_ref, v2_ref):
        g = p_ref[0]
        for k in range(1, N_DEV):
            g = g + p_ref[k]
        d, m2, v2 = _adam_math(g, w_ref[...], m_ref[...], v_ref[...])
        g_ref[...] = g
        d_ref[...] = d
        m2_ref[...] = m2
        v2_ref[...] = v2

    blk = pl.BlockSpec((tr, cols), lambda i: (i, 0))
    out = jax.ShapeDtypeStruct((rows, cols), F32)
    outs, _ = _pcall(body, name="adamw_replicated", grid=(rows // tr,),
                     in_specs=[pl.BlockSpec((N_DEV, tr, cols), lambda i: (0, i, 0)), blk, blk, blk],
                     out_specs=[blk, blk, blk, blk], out_shape=[out, out, out, out], sem=("parallel",),
                     args=(parts, w, m, v))
    return outs


def _ffn_fwd(x, gain, wg, wu, wd, carried=None):
    S, D = x.shape
    nsh, fs, _ = wg.shape
    tm = _tile(S, 512, 16)

    def body(x_ref, g_ref, wg_ref, wu_ref, wd_ref, y_ref, hn_ref, act_ref, ga_ref, gb_ref, acc_ref):
        j = pl.program_id(1)

        @pl.when(j == 0)
        def _():
            xhat, _ = _rms_hat(x_ref[...])
            hn_ref[...] = (xhat * g_ref[...]).astype(BF)
            acc_ref[...] = jnp.zeros_like(acc_ref)

        hn = hn_ref[...]
        a = _dot(hn, wg_ref[...], NT)
        b = _dot(hn, wu_ref[...], NT)
        sg = jax.nn.sigmoid(a)
        sil = a * sg
        act = (sil * b).astype(BF)
        act_ref[...] = act
        ga_ref[...] = (b * (sg * (1.0 + a * (1.0 - sg)))).astype(BF)
        gb_ref[...] = sil.astype(BF)
        acc_ref[...] += _dot(act, wd_ref[...])

        @pl.when(j == nsh - 1)
        def _():
            y_ref[...] = x_ref[...] + 0.5 * acc_ref[...]

    row = pl.BlockSpec((tm, D), lambda m, j: (m, 0))
    hid = pl.BlockSpec((None, tm, fs), lambda m, j: (j, m, 0))
    return _pcall(
        body, name="ffn_fwd", grid=(S // tm, nsh),
        in_specs=[row, pl.BlockSpec((1, D), lambda m, j: (0, 0)),
                  pl.BlockSpec((None, fs, D), lambda m, j: (j, 0, 0)),
                  pl.BlockSpec((None, fs, D), lambda m, j: (j, 0, 0)),
                  pl.BlockSpec((None, fs, D), lambda m, j: (j, 0, 0))],
        out_specs=[row, row, hid, hid, hid],
        out_shape=[jax.ShapeDtypeStruct((S, D), F32), jax.ShapeDtypeStruct((S, D), BF)]
        + [jax.ShapeDtypeStruct((nsh, S, fs), BF)] * 3,
        scratch=[pltpu.VMEM((tm, D), F32)], sem=("parallel", "arbitrary"), vmem=58,
        args=(x, gain, wg, wu, wd), carried=carried)


def _ffn_bwd_dx(dy, x, gain, ga, gb, wg, wu, wd, carried=None):
    S, D = x.shape
    nsh, fs, _ = wg.shape
    tm = _tile(S, 512, 16)

    def body(dy_ref, x_ref, g_ref, ga_ref, gb_ref, wg_ref, wu_ref, wd_ref,
             dx_ref, da_ref, db_ref, dyh_ref, dg_ref, acc_ref):
        m, j = pl.program_id(0), pl.program_id(1)

        @pl.when(j == 0)
        def _():
            dyh_ref[...] = (0.5 * dy_ref[...]).astype(BF)
            acc_ref[...] = jnp.zeros_like(acc_ref)

        dact = _dot(dyh_ref[...], wd_ref[...], NT)
        da = (dact * ga_ref[...].astype(F32)).astype(BF)
        db = (dact * gb_ref[...].astype(F32)).astype(BF)
        da_ref[...] = da
        db_ref[...] = db
        acc_ref[...] += _dot(da, wg_ref[...])
        acc_ref[...] += _dot(db, wu_ref[...])

        @pl.when(j == nsh - 1)
        def _():
            dx, dgain = _rms_bwd(acc_ref[...], x_ref[...], g_ref[...], dy_ref[...])
            dx_ref[...] = dx
            _accum_rows(dg_ref, dgain, m == 0)

    row = pl.BlockSpec((tm, D), lambda m, j: (m, 0))
    row1 = pl.BlockSpec((tm, D), lambda m, j: (m, 0), pipeline_mode=pl.Buffered(1))
    hid = pl.BlockSpec((None, tm, fs), lambda m, j: (j, m, 0))
    hshape = jax.ShapeDtypeStruct((nsh, S, fs), BF)
    return _pcall(
        body, name="ffn_bwd_dx", grid=(S // tm, nsh),
        in_specs=[row1, row1, pl.BlockSpec((1, D), lambda m, j: (0, 0)), hid, hid,
                  pl.BlockSpec((None, fs, D), lambda m, j: (j, 0, 0)),
                  pl.BlockSpec((None, fs, D), lambda m, j: (j, 0, 0)),
                  pl.BlockSpec((None, fs, D), lambda m, j: (j, 0, 0))],
        out_specs=[row1, hid, hid, row1, pl.BlockSpec((8, D), lambda m, j: (0, 0))],
        out_shape=[jax.ShapeDtypeStruct((S, D), F32), hshape, hshape,
                   jax.ShapeDtypeStruct((S, D), BF), jax.ShapeDtypeStruct((8, D), F32)],
        scratch=[pltpu.VMEM((tm, D), F32)], sem=("arbitrary", "arbitrary"), vmem=60,
        args=(dy, x, gain, ga, gb, wg, wu, wd), carried=carried)


def _opspec(arr, kind, tm, grid_mj):
    if kind == "full":
        return pl.BlockSpec((tm, arr.shape[1]), lambda *g: (grid_mj(*g)[0], 0)), arr.shape[1]
    if kind == "cols":
        n = arr.shape[1] // N_DEV
        return pl.BlockSpec((tm, n), lambda *g: grid_mj(*g)), n
    n = arr.shape[2]
    return pl.BlockSpec((None, tm, n), lambda *g: (grid_mj(*g)[1], grid_mj(*g)[0], 0)), n


def _wgrad(a, a_kind, b, b_kind, name, carried=None):
    S = a.shape[0] if a_kind != "stack" else a.shape[1]
    tm = _tile(S, 1024, 16)
    mj = lambda j, m: (m, j)
    a_spec, ka = _opspec(a, a_kind, tm, mj)
    b_spec, nb = _opspec(b, b_kind, tm, mj)
    n_m = S // tm

    def body(a_ref, b_ref, o_ref, acc_ref):
        m = pl.program_id(1)

        @pl.when(m == 0)
        def _():
            acc_ref[...] = jnp.zeros_like(acc_ref)

        acc_ref[...] += _dot(a_ref[...].astype(BF), b_ref[...].astype(BF), TN)

        @pl.when(m == n_m - 1)
        def _():
            o_ref[...] = acc_ref[...].astype(BF)

    outs, cout = _pcall(
        body, name=name, grid=(N_DEV, n_m), in_specs=[a_spec, b_spec],
        out_specs=[pl.BlockSpec((None, ka, nb), lambda j, m: (j, 0, 0))],
        out_shape=[jax.ShapeDtypeStruct((N_DEV, ka, nb), BF)],
        scratch=[pltpu.VMEM((ka, nb), F32)], sem=("parallel", "arbitrary"), args=(a, b), carried=carried)
    return outs[0], cout


def _normproj(x, gain, w, name, carried=None):
    S, D = x.shape
    _, n, _ = w.shape
    tm = _tile(S, 512, 16)

    def body(x_ref, g_ref, w_ref, hn_ref, y_ref):
        @pl.when(pl.program_id(1) == 0)
        def _():
            xhat, _ = _rms_hat(x_ref[...])
            hn_ref[...] = (xhat * g_ref[...]).astype(BF)

        y_ref[...] = _dot(hn_ref[...], w_ref[...], NT)

    row = pl.BlockSpec((tm, D), lambda m, j: (m, 0))
    return _pcall(
        body, name=name, grid=(S // tm, N_DEV),
        in_specs=[row, pl.BlockSpec((1, D), lambda m, j: (0, 0)), pl.BlockSpec((None, n, D), lambda m, j: (j, 0, 0))],
        out_specs=[row, pl.BlockSpec((tm, n), lambda m, j: (m, j))],
        out_shape=[jax.ShapeDtypeStruct((S, D), BF), jax.ShapeDtypeStruct((S, N_DEV * n), F32)],
        sem=("parallel", "arbitrary"), args=(x, gain, w), carried=carried)


def _rowproj(x, t, w, name, carried=None):
    S, D = x.shape
    _, k, _ = w.shape
    tm = _tile(S, 512, 16)

    def body(x_ref, t_ref, w_ref, y_ref):
        j = pl.program_id(1)
        part = _dot(t_ref[...], w_ref[...])

        @pl.when(j == 0)
        def _():
            y_ref[...] = x_ref[...] + part

        @pl.when(j > 0)
        def _():
            y_ref[...] += part

    row = pl.BlockSpec((tm, D), lambda m, j: (m, 0))
    outs, cout = _pcall(
        body, name=name, grid=(S // tm, N_DEV),
        in_specs=[row, pl.BlockSpec((tm, k), lambda m, j: (m, j)), pl.BlockSpec((None, k, D), lambda m, j: (j, 0, 0))],
        out_specs=[row], out_shape=[jax.ShapeDtypeStruct((S, D), F32)],
        sem=("parallel", "arbitrary"), args=(x, t, w), carried=carried)
    return outs[0], cout


def _dgrad_row(dy, w, out_dtype, name):
    S, D = dy.shape
    _, k, _ = w.shape
    tm = _tile(S, 512, 16)

    def body(dy_ref, w_ref, dt_ref, dyb_ref):
        @pl.when(pl.program_id(1) == 0)
        def _():
            dyb_ref[...] = dy_ref[...].astype(BF)

        dt_ref[...] = _dot(dyb_ref[...], w_ref[...], NT).astype(out_dtype)

    outs, _ = _pcall(
        body, name=name, grid=(S // tm, N_DEV),
        in_specs=[pl.BlockSpec((tm, D), lambda m, j: (m, 0)), pl.BlockSpec((None, k, D), lambda m, j: (j, 0, 0))],
        out_specs=[pl.BlockSpec((tm, k), lambda m, j: (m, j))],
        out_shape=[jax.ShapeDtypeStruct((S, N_DEV * k), out_dtype)],
        scratch=[pltpu.VMEM((tm, D), BF)], sem=("parallel", "arbitrary"), args=(dy, w))
    return outs[0]


def _dgrad_col_norm(dres, x, gain, dz, w, name, carried=None):
    S, D = x.shape
    _, n, _ = w.shape
    tm = _tile(S, 512, 16)

    def body(dres_ref, x_ref, g_ref, dz_ref, w_ref, dx_ref, dg_ref, acc_ref):
        m, j = pl.program_id(0), pl.program_id(1)

        @pl.when(j == 0)
        def _():
            acc_ref[...] = jnp.zeros_like(acc_ref)

        acc_ref[...] += _dot(dz_ref[...], w_ref[...])

        @pl.when(j == N_DEV - 1)
        def _():
            dx, dgain = _rms_bwd(acc_ref[...], x_ref[...], g_ref[...], dres_ref[...])
            dx_ref[...] = dx
            _accum_rows(dg_ref, dgain, m == 0)

    row = pl.BlockSpec((tm, D), lambda m, j: (m, 0), pipeline_mode=pl.Buffered(1))
    return _pcall(
        body, name=name, grid=(S // tm, N_DEV),
        in_specs=[row, row, pl.BlockSpec((1, D), lambda m, j: (0, 0)), pl.BlockSpec((tm, n), lambda m, j: (m, j)),
                  pl.BlockSpec((None, n, D), lambda m, j: (j, 0, 0))],
        out_specs=[row, pl.BlockSpec((8, D), lambda m, j: (0, 0))],
        out_shape=[jax.ShapeDtypeStruct((S, D), F32), jax.ShapeDtypeStruct((8, D), F32)],
        scratch=[pltpu.VMEM((tm, D), F32)], sem=("arbitrary", "arbitrary"),
        args=(dres, x, gain, dz, w), carried=carried)


def _headnorm_fwd(xa, width, gain_row, scale, name, carried=None):
    S = xa.shape[0]
    cb = _tile(width, 1024, LANE)
    tm = _tile(S, 512, 8)

    def body(x_ref, g_ref, y_ref):
        for c in range(cb // LANE):
            sl = slice(c * LANE, (c + 1) * LANE)
            xhat, _ = _rms_hat(x_ref[:, sl])
            y_ref[:, sl] = xhat * (g_ref[:, sl] * scale)

    outs, couts = _pcall(
        body, name=name, grid=(S // tm, width // cb),
        in_specs=[pl.BlockSpec((tm, cb), lambda m, c: (m, c)), pl.BlockSpec((1, cb), lambda m, c: (0, c))],
        out_specs=[pl.BlockSpec((tm, cb), lambda m, c: (m, c))],
        out_shape=[jax.ShapeDtypeStruct((S, width), F32)], sem=("parallel", "parallel"), args=(xa, gain_row),
        carried=carried)
    return outs[0], couts


def _headnorm_bwd(dy, xa, gain_row, scale, name, tail=None):
    S, width = dy.shape
    cb = _tile(width, 1024, LANE)
    tm = _tile(S, 512, 16)
    ncb = width // cb
    ntail = 0 if tail is None else tail.shape[1] // cb

    def body(dy_ref, x_ref, g_ref, *rest):
        dx_ref, dg_ref = rest[-2:]
        c, m = pl.program_id(0), pl.program_id(1)

        @pl.when(c < ncb)
        def _():
            rows = []
            for i in range(cb // LANE):
                sl = slice(i * LANE, (i + 1) * LANE)
                dx, dgain = _rms_bwd(dy_ref[:, sl] * scale, x_ref[:, sl], g_ref[:, sl], 0.0)
                dx_ref[:, sl] = dx.astype(BF)
                rows.append(dgain)
            _accum_rows(dg_ref, jnp.concatenate(rows, axis=1), m == 0)

        if tail is not None:
            @pl.when(c >= ncb)
            def _():
                dx_ref[...] = rest[0][...].astype(BF)

    head = lambda c: jnp.minimum(c, ncb - 1)
    in_specs = [pl.BlockSpec((tm, cb), lambda c, m: (jnp.where(c < ncb, m, 0), head(c))),
                pl.BlockSpec((tm, cb), lambda c, m: (jnp.where(c < ncb, m, 0), head(c))),
                pl.BlockSpec((1, cb), lambda c, m: (0, head(c)))]
    args = [dy, xa, gain_row]
    if tail is not None:
        in_specs.append(pl.BlockSpec((tm, cb), lambda c, m: (jnp.where(c >= ncb, m, 0), jnp.maximum(c - ncb, 0))))
        args.append(tail)
    outs, _ = _pcall(
        body, name=name, grid=(ncb + ntail, S // tm), in_specs=in_specs,
        out_specs=[pl.BlockSpec((tm, cb), lambda c, m: (m, c)), pl.BlockSpec((8, cb), lambda c, m: (0, head(c)))],
        out_shape=[jax.ShapeDtypeStruct((S, width + ntail * cb), BF), jax.ShapeDtypeStruct((8, width), F32)],
        sem=("arbitrary", "arbitrary"), args=args)
    return outs


def _causal():
    p = lax.broadcasted_iota(jnp.int32, (LANE, LANE), 0)
    q = lax.broadcasted_iota(jnp.int32, (LANE, LANE), 1)
    return p >= q


def _gmlp_fwd(z, v_gain, ws, bias, carried=None):
    S, dg2 = z.shape
    dg = dg2 // 2
    G = dg // LANE

    def body(z_ref, vg_ref, ws_ref, bias_ref, t_ref):
        u, _ = _gelu(z_ref[:, :dg])
        v, _ = _gelu(z_ref[:, dg:])
        vhat, _ = _rms_hat(v)
        vn = (vhat * vg_ref[...]).astype(BF)
        mask = _causal()
        for g in range(G):
            sl = slice(g * LANE, (g + 1) * LANE)
            wm = jnp.where(mask, ws_ref[g], 0.0).astype(BF)
            sv = _dot(wm, vn[:, sl]) + bias_ref[:, sl]
            t_ref[:, sl] = (u[:, sl] * sv).astype(BF)

    outs, couts = _pcall(
        body, name="gmlp_fwd", grid=(S // LANE,),
        in_specs=[pl.BlockSpec((LANE, dg2), lambda n: (n, 0)), pl.BlockSpec((1, dg), lambda n: (0, 0)),
                  pl.BlockSpec((G, LANE, LANE), lambda n: (0, 0, 0)), pl.BlockSpec((LANE, dg), lambda n: (0, 0))],
        out_specs=[pl.BlockSpec((LANE, dg), lambda n: (n, 0))],
        out_shape=[jax.ShapeDtypeStruct((S, dg), BF)], sem=("parallel",), args=(z, v_gain, ws, bias),
        carried=carried)
    return outs[0], couts


def _gmlp_bwd(z, dt, v_gain, ws, bias):
    S, dg2 = z.shape
    dg = dg2 // 2
    G = dg // LANE

    def body(z_ref, dt_ref, vg_ref, ws_ref, bias_ref, dz_ref, dws_ref, db_ref, dvg_ref, dvn_ref):
        n = pl.program_id(0)
        zu, zv = z_ref[:, :dg], z_ref[:, dg:]
        u, tu = _gelu(zu)
        v, tv = _gelu(zv)
        vhat, r = _rms_hat(v)
        vn = (vhat * vg_ref[...]).astype(BF)
        mask = _causal()

        @pl.when(n == 0)
        def _():
            dws_ref[...] = jnp.zeros_like(dws_ref)
            db_ref[...] = jnp.zeros_like(db_ref)

        for g in range(G):
            sl = slice(g * LANE, (g + 1) * LANE)
            wm = jnp.where(mask, ws_ref[g], 0.0).astype(BF)
            sv = _dot(wm, vn[:, sl]) + bias_ref[:, sl]
            dtg = dt_ref[:, sl].astype(F32)
            dz_ref[:, sl] = (dtg * sv * _gelu_grad(zu[:, sl], tu[:, sl])).astype(BF)
            dsv = dtg * u[:, sl]
            dsvb = dsv.astype(BF)
            dvn_ref[:, sl] = _dot(wm, dsvb, TN)
            dws_ref[g] += jnp.where(mask, _dot(dsvb, vn[:, sl], NT), 0.0)
            db_ref[:, sl] += jnp.broadcast_to(jnp.sum(dsv, axis=1, keepdims=True), (LANE, LANE))

        dvn = dvn_ref[...]
        dxhat = dvn * vg_ref[...]
        dv = r * (dxhat - vhat * jnp.mean(dxhat * vhat, axis=-1, keepdims=True))
        dz_ref[:, dg:] = (dv * _gelu_grad(zv, tv)).astype(BF)
        _accum_rows(dvg_ref, jnp.sum(dvn * vhat, axis=0, keepdims=True), n == 0)

    outs, _ = _pcall(
        body, name="gmlp_bwd", grid=(S // LANE,),
        in_specs=[pl.BlockSpec((LANE, dg2), lambda n: (n, 0)), pl.BlockSpec((LANE, dg), lambda n: (n, 0)),
                  pl.BlockSpec((1, dg), lambda n: (0, 0)), pl.BlockSpec((G, LANE, LANE), lambda n: (0, 0, 0)),
                  pl.BlockSpec((LANE, dg), lambda n: (0, 0))],
        out_specs=[pl.BlockSpec((LANE, dg2), lambda n: (n, 0)), pl.BlockSpec((G, LANE, LANE), lambda n: (0, 0, 0)),
                   pl.BlockSpec((LANE, dg), lambda n: (0, 0)), pl.BlockSpec((8, dg), lambda n: (0, 0))],
        out_shape=[jax.ShapeDtypeStruct((S, dg2), BF), jax.ShapeDtypeStruct((G, LANE, LANE), F32),
                   jax.ShapeDtypeStruct((LANE, dg), F32), jax.ShapeDtypeStruct((8, dg), F32)],
        scratch=[pltpu.VMEM((LANE, dg), F32)], sem=("arbitrary",), args=(z, dt, v_gain, ws, bias))
    return outs


N_ENT = ATT_WIN // LANE


def _rows(ref, start, dil):
    return ref[pl.ds(start, LANE), :] if dil == 1 else ref[pl.ds(start, LANE, stride=dil), :]


def _rows_store(ref, start, dil, val):
    if dil == 1:
        ref[pl.ds(start, LANE), :] = val
    else:
        ref[pl.ds(start, LANE, stride=dil), :] = val


def _slope_times_dil(h, n_heads, dil, shape):
    hv = jnp.zeros(shape, F32) + (h + 1).astype(F32)
    return jnp.exp(hv * (-8.0 / n_heads * math.log(2.0))) * float(dil)


def _band_bias(h, n_heads, dil, has_prev):
    qi = lax.broadcasted_iota(jnp.int32, (LANE, 2 * LANE), 0)
    kj = lax.broadcasted_iota(jnp.int32, (LANE, 2 * LANE), 1)
    delta = qi + LANE - kj
    valid = (delta >= 0) & (delta <= LANE) & ((kj >= LANE) | has_prev)
    return jnp.where(valid, -_slope_times_dil(h, n_heads, dil, (LANE, 2 * LANE)) * delta.astype(F32), NEG)


def _attn_geom(S, H, g, dil):
    pb = LANE * dil
    nblk = ATT_WIN // pb
    nw = S // ATT_WIN
    win = lambda c0: pl.BlockSpec((ATT_WIN, LANE), lambda h, w: (w, c0 + h))
    prev = lambda c0: pl.BlockSpec((pb, LANE), lambda h, w: (jnp.maximum(w * nblk - 1, 0), c0 + h))
    nxt = lambda c0: pl.BlockSpec((pb, LANE), lambda h, w: (jnp.minimum((w + 1) * nblk, nw * nblk - 1), c0 + h))
    return pb, nblk, nw, g * H, win, prev, nxt


def _stage_band(dst, cur_ref, prev_ref, dil, pb, nblk):
    for blk in range(nblk):
        for r in range(dil):
            e = blk * dil + r
            dst[e, :LANE] = _rows(prev_ref, r, dil) if blk == 0 else _rows(cur_ref, (blk - 1) * pb + r, dil)
            dst[e, LANE:] = _rows(cur_ref, blk * pb + r, dil)


def _stage(dst, ref, dil, pb, nblk, lead=None):
    for blk in range(nblk):
        for r in range(dil):
            val = _rows(ref, blk * pb + r, dil)
            if lead is None:
                dst[blk * dil + r] = val
            else:
                dst[lead, blk * dil + r] = val


def _unstage(ref, src, dil, pb, nblk):
    for blk in range(nblk):
        for r in range(dil):
            _rows_store(ref, blk * pb + r, dil, src[blk * dil + r])


def _attn_fwd(q, k, kv, g, dil, carried=None):
    S = q.shape[0]
    H = q.shape[1] // (3 * LANE)
    pb, nblk, nw, col, win, prev, _ = _attn_geom(S, H, g, dil)
    vcol = 3 * H + col

    def body(q_ref, kc_ref, kp_ref, vc_ref, vp_ref, o_ref, l_ref, qs, ks, vs, os_, ls):
        h, w = pl.program_id(0), pl.program_id(1)
        _stage(qs, q_ref, dil, pb, nblk)
        _stage_band(ks, kc_ref, kp_ref, dil, pb, nblk)
        _stage_band(vs, vc_ref, vp_ref, dil, pb, nblk)

        def run(lo, hi, bias):
            def step(e, carry):
                s = _dot(qs[e].astype(BF), ks[e].astype(BF), NT) + bias
                mx = jnp.max(s, axis=-1, keepdims=True)
                p = jnp.exp(s - mx)
                l = jnp.sum(p, axis=-1, keepdims=True)
                os_[e] = _dot((p / l).astype(BF), vs[e].astype(BF))
                ls[e] = jnp.broadcast_to(mx + jnp.log(l), (LANE, LANE))
                return carry
            if hi > lo:
                lax.fori_loop(lo, hi, step, 0, unroll=True)

        run(0, dil, _band_bias(h, H, dil, w > 0))
        run(dil, N_ENT, _band_bias(h, H, dil, True))
        _unstage(o_ref, os_, dil, pb, nblk)
        _unstage(l_ref, ls, dil, pb, nblk)

    out = jax.ShapeDtypeStruct((S, H * LANE), F32)
    sq = pltpu.VMEM((N_ENT, LANE, LANE), F32)
    sk = pltpu.VMEM((N_ENT, 2 * LANE, LANE), F32)
    return _pcall(
        body, name="attn_fwd_d%d" % dil, grid=(H, nw),
        in_specs=[win(col), win(col), prev(col), win(vcol), prev(vcol)],
        out_specs=[win(0), win(0)], out_shape=[out, out], scratch=[sq, sk, sk, sq, sq],
        sem=("parallel", "parallel"), args=(q, k, k, kv, kv), carried=carried)


def _attn_combine(os_, ls_, carried=None):
    S, C = os_[0].shape
    tm = _tile(S, 256, 16)

    def body(o0, o1, o2, l0, l1, l2, o_ref, lse_ref):
        a, b, c = l0[...], l1[...], l2[...]
        mx = jnp.maximum(jnp.maximum(a, b), c)
        ea, eb, ec = jnp.exp(a - mx), jnp.exp(b - mx), jnp.exp(c - mx)
        den = ea + eb + ec
        o_ref[...] = ((ea * o0[...] + eb * o1[...] + ec * o2[...]) / den).astype(BF)
        lse_ref[...] = mx + jnp.log(den)

    blk = pl.BlockSpec((tm, C), lambda m: (m, 0))
    return _pcall(
        body, name="attn_combine", grid=(S // tm,), in_specs=[blk] * 6, out_specs=[blk, blk],
        out_shape=[jax.ShapeDtypeStruct((S, C), BF), jax.ShapeDtypeStruct((S, C), F32)],
        sem=("parallel",), args=(*os_, *ls_), carried=carried)


def _attn_delta(do, o):
    S, C = do.shape
    tm = _tile(S, 512, 16)

    def body(do_ref, o_ref, d_ref):
        for c in range(C // LANE):
            sl = slice(c * LANE, (c + 1) * LANE)
            prod = do_ref[:, sl].astype(BF).astype(F32) * o_ref[:, sl].astype(F32)
            d_ref[:, sl] = jnp.broadcast_to(jnp.sum(prod, axis=-1, keepdims=True), (tm, LANE))

    blk = pl.BlockSpec((tm, C), lambda m: (m, 0))
    outs, _ = _pcall(
        body, name="attn_delta", grid=(S // tm,), in_specs=[blk, blk], out_specs=[blk],
        out_shape=[jax.ShapeDtypeStruct((S, C), F32)], sem=("parallel",), args=(do, o))
    return outs[0]


def _attn_bwd_dq(q, k, kv, do, lse, dl, g, dil, dq_prev):
    S = q.shape[0]
    H = q.shape[1] // (3 * LANE)
    pb, nblk, nw, col, win, prev, _ = _attn_geom(S, H, g, dil)
    vcol = 3 * H + col

    def body(q_ref, kc_ref, kp_ref, vc_ref, vp_ref, do_ref, l_ref, d_ref, *rest):
        dq_ref, qs, ks, vs, dos, ls, ds_, dqs = rest[-8:]
        h, w = pl.program_id(0), pl.program_id(1)
        _stage(qs, q_ref, dil, pb, nblk)
        _stage_band(ks, kc_ref, kp_ref, dil, pb, nblk)
        _stage_band(vs, vc_ref, vp_ref, dil, pb, nblk)
        _stage(dos, do_ref, dil, pb, nblk)
        _stage(ls, l_ref, dil, pb, nblk)
        _stage(ds_, d_ref, dil, pb, nblk)

        def run(lo, hi, bias):
            def step(e, carry):
                kb = ks[e].astype(BF)
                s = _dot(qs[e].astype(BF), kb, NT) + bias
                p = jnp.exp(s - ls[e][:, :1])
                dp = _dot(dos[e].astype(BF), vs[e].astype(BF), NT)
                dsc = p * (dp - ds_[e][:, :1])
                dqs[e] = _dot(dsc.astype(BF), kb)
                return carry
            if hi > lo:
                lax.fori_loop(lo, hi, step, 0, unroll=True)

        run(0, dil, _band_bias(h, H, dil, w > 0))
        run(dil, N_ENT, _band_bias(h, H, dil, True))
        _unstage(dq_ref, dqs, dil, pb, nblk)

    sq = pltpu.VMEM((N_ENT, LANE, LANE), F32)
    sk = pltpu.VMEM((N_ENT, 2 * LANE, LANE), F32)
    in_specs = [win(col), win(col), prev(col), win(vcol), prev(vcol), win(0), win(0), win(0)]
    args = [q, k, k, kv, kv, do, lse, dl]
    aliases = {}
    if dq_prev is not None:
        in_specs.append(ANY)
        args.append(dq_prev)
        aliases = {8: 0}
    outs, _ = _pcall(
        body, name="attn_bwd_dq_d%d" % dil, grid=(H, nw), in_specs=in_specs,
        out_specs=[win(col)], out_shape=[jax.ShapeDtypeStruct((S, 3 * H * LANE), F32)],
        scratch=[sq, sk, sk, sq, sq, sq, sq], sem=("parallel", "parallel"), args=args, aliases=aliases)
    return outs[0]


def _attn_bwd_dkv(q, k, kv, do, lse, dl, g, dil, prev_out, accumulate):
    S = q.shape[0]
    H = q.shape[1] // (3 * LANE)
    pb, nblk, nw, col, win, _, nxt = _attn_geom(S, H, g, dil)
    vcol = 3 * H + col
    n_q = N_ENT + dil

    def body(k_ref, v_ref, qc_ref, qn_ref, doc_ref, don_ref, lc_ref, ln_ref, dc_ref, dn_ref, *rest):
        dk_ref, dv_ref, ks, vs, qs, dos, ls, ds_, dks, dvs = rest[-10:]
        h, w = pl.program_id(0), pl.program_id(1)
        _stage(ks, k_ref, dil, pb, nblk)
        _stage(vs, v_ref, dil, pb, nblk)
        for dst, cur, nx in ((qs, qc_ref, qn_ref), (dos, doc_ref, don_ref), (ls, lc_ref, ln_ref), (ds_, dc_ref, dn_ref)):
            _stage(dst, cur, dil, pb, nblk)
            for r in range(dil):
                dst[N_ENT + r] = _rows(nx, r, dil)
        qi = lax.broadcasted_iota(jnp.int32, (LANE, LANE), 0)
        kj = lax.broadcasted_iota(jnp.int32, (LANE, LANE), 1)
        sd = _slope_times_dil(h, H, dil, (LANE, LANE))
        bias_c = jnp.where(qi >= kj, -sd * (qi - kj).astype(F32), NEG)

        def run(lo, hi, has_next):
            bias_n = jnp.where((qi <= kj) & has_next, -sd * (qi + LANE - kj).astype(F32), NEG)

            def step(e, carry):
                kb = ks[e].astype(BF)
                vb = vs[e].astype(BF)
                dk = jnp.zeros((LANE, LANE), F32)
                dv = jnp.zeros((LANE, LANE), F32)
                for eq, bias in ((e, bias_c), (e + dil, bias_n)):
                    qb = qs[eq].astype(BF)
                    dob = dos[eq].astype(BF)
                    s = _dot(qb, kb, NT) + bias
                    p = jnp.exp(s - ls[eq][:, :1])
                    dp = _dot(dob, vb, NT)
                    dsc = p * (dp - ds_[eq][:, :1])
                    dv = dv + _dot(p.astype(BF), dob, TN)
                    dk = dk + _dot(dsc.astype(BF), qb, TN)
                dks[e] = dk
                dvs[e] = dv
                return carry
            if hi > lo:
                lax.fori_loop(lo, hi, step, 0, unroll=True)

        run(0, N_ENT - dil, True)
        run(N_ENT - dil, N_ENT, w < nw - 1)
        if accumulate:
            pk_ref, pv_ref = rest[0], rest[1]
            dk_ref[...] = pk_ref[...]
            dv_ref[...] = pv_ref[...]
            for blk in range(nblk):
                for r in range(dil):
                    e, start = blk * dil + r, blk * pb + r
                    _rows_store(dk_ref, start, dil, _rows(dk_ref, start, dil) + dks[e])
                    _rows_store(dv_ref, start, dil, _rows(dv_ref, start, dil) + dvs[e])
        else:
            _unstage(dk_ref, dks, dil, pb, nblk)
            _unstage(dv_ref, dvs, dil, pb, nblk)

    s1 = pltpu.VMEM((N_ENT, LANE, LANE), F32)
    s2 = pltpu.VMEM((n_q, LANE, LANE), F32)
    out = jax.ShapeDtypeStruct((S, 3 * H * LANE), F32)
    in_specs = [win(col), win(vcol), win(col), nxt(col), win(0), nxt(0), win(0), nxt(0), win(0), nxt(0)]
    args = [k, kv, q, q, do, do, lse, lse, dl, dl]
    aliases = {}
    if prev_out is not None:
        in_specs += [win(col), win(col)] if accumulate else [ANY, ANY]
        args += list(prev_out)
        aliases = {10: 0, 11: 1}
    outs, _ = _pcall(
        body, name="attn_bwd_dkv_d%d%s" % (dil, "_acc" if accumulate else ""), grid=(H, nw), in_specs=in_specs,
        out_specs=[win(col), win(col)], out_shape=[out, out],
        scratch=[s1, s1, s2, s2, s2, s2, s1, s1], sem=("parallel", "parallel"), vmem=56, args=args, aliases=aliases)
    return outs


def _loss_head(y, target):
    S, D = y.shape
    tm = _tile(S, 512, 8)

    def body(y_ref, t_ref, dy_ref, l_ref):
        e = y_ref[...] - t_ref[...]
        dy_ref[...] = e * (1.0 / D)
        part = jnp.broadcast_to(jnp.sum(jnp.sum(e * e, axis=1, keepdims=True), axis=0, keepdims=True) * (0.5 / D), (8, LANE))
        _accum_rows(l_ref, part, pl.program_id(0) == 0)

    blk = pl.BlockSpec((tm, D), lambda m: (m, 0))
    outs, _ = _pcall(
        body, name="loss_head", grid=(S // tm,), in_specs=[blk, blk],
        out_specs=[blk, pl.BlockSpec((8, LANE), lambda m: (0, 0))],
        out_shape=[jax.ShapeDtypeStruct((S, D), F32), jax.ShapeDtypeStruct((8, LANE), F32)],
        sem=("arbitrary",), args=(y, target))
    return outs


class _Queue:
    def __init__(self):
        self.items = []

    def push(self, kind, key, arr):
        self.items.append((kind, key, arr))

    def take(self, budget_us):
        taken, spent = [], 0.0
        while self.items and spent < budget_us:
            item = self.items.pop(0)
            taken.append(item)
            spent += COST_US_PER_ELEM[item[0]] * (item[2].size / item[2].shape[0] if item[0] != "gather" else item[2].size)
        return taken

    def take_keys(self, keys):
        taken = [it for it in self.items if it[1] in keys]
        self.items = [it for it in self.items if it[1] not in keys]
        return taken

    def take_kind(self, kind):
        taken = [it for it in self.items if it[0] == kind]
        self.items = [it for it in self.items if it[0] != kind]
        return taken


def kernel(x, ffn1_norm, ffn1_w_gate, ffn1_w_up, ffn1_w_down, mix_norm, ffn2_norm, ffn2_w_gate, ffn2_w_up, ffn2_w_down, gmlp_w_in, gmlp_v_norm, gmlp_w_s, gmlp_b_s, gmlp_w_out, kv_norm, w_kv, k_norm, attn_w_q, attn_q_norm, attn_w_o, loss_target, m_ffn1_norm, m_ffn1_w_gate, m_ffn1_w_up, m_ffn1_w_down, m_mix_norm, m_ffn2_norm, m_ffn2_w_gate, m_ffn2_w_up, m_ffn2_w_down, m_gmlp_w_in, m_gmlp_v_norm, m_gmlp_w_s, m_gmlp_b_s, m_gmlp_w_out, m_kv_norm, m_w_kv, m_k_norm, m_attn_w_q, m_attn_q_norm, m_attn_w_o, v_ffn1_norm, v_ffn1_w_gate, v_ffn1_w_up, v_ffn1_w_down, v_mix_norm, v_ffn2_norm, v_ffn2_w_gate, v_ffn2_w_up, v_ffn2_w_down, v_gmlp_w_in, v_gmlp_v_norm, v_gmlp_w_s, v_gmlp_b_s, v_gmlp_w_out, v_kv_norm, v_w_kv, v_k_norm, v_attn_w_q, v_attn_q_norm, v_attn_w_o):
    names = ["ffn1_norm", "ffn1_w_gate", "ffn1_w_up", "ffn1_w_down", "mix_norm", "ffn2_norm", "ffn2_w_gate",
             "ffn2_w_up", "ffn2_w_down", "gmlp_w_in", "gmlp_v_norm", "gmlp_w_s", "gmlp_b_s", "gmlp_w_out",
             "kv_norm", "w_kv", "k_norm", "attn_w_q", "attn_q_norm", "attn_w_o"]
    W = dict(zip(names, [ffn1_norm, ffn1_w_gate, ffn1_w_up, ffn1_w_down, mix_norm, ffn2_norm, ffn2_w_gate,
                         ffn2_w_up, ffn2_w_down, gmlp_w_in, gmlp_v_norm, gmlp_w_s, gmlp_b_s, gmlp_w_out,
                         kv_norm, w_kv, k_norm, attn_w_q, attn_q_norm, attn_w_o]))
    M = dict(zip(names, [m_ffn1_norm, m_ffn1_w_gate, m_ffn1_w_up, m_ffn1_w_down, m_mix_norm, m_ffn2_norm, m_ffn2_w_gate,
                         m_ffn2_w_up, m_ffn2_w_down, m_gmlp_w_in, m_gmlp_v_norm, m_gmlp_w_s, m_gmlp_b_s, m_gmlp_w_out,
                         m_kv_norm, m_w_kv, m_k_norm, m_attn_w_q, m_attn_q_norm, m_attn_w_o]))
    V = dict(zip(names, [v_ffn1_norm, v_ffn1_w_gate, v_ffn1_w_up, v_ffn1_w_down, v_mix_norm, v_ffn2_norm, v_ffn2_w_gate,
                         v_ffn2_w_up, v_ffn2_w_down, v_gmlp_w_in, v_gmlp_v_norm, v_gmlp_w_s, v_gmlp_b_s, v_gmlp_w_out,
                         v_kv_norm, v_w_kv, v_k_norm, v_attn_w_q, v_attn_q_norm, v_attn_w_o]))

    depth = ffn1_norm.shape[0]
    n_a = gmlp_w_in.shape[0]
    S, D = x.shape[1], x.shape[2]
    H = D // LANE
    n_grp = len(DILATIONS)
    hw = H * LANE
    xi, yi, ci = _me()
    core = jnp.reshape(ci, (1,)).astype(jnp.int32)
    chip = jnp.reshape(2 * xi + yi, (1,)).astype(jnp.int32)
    dev = 4 * xi + 2 * yi + ci
    q_scale = LANE ** -0.5
    ffn_names = (("f1", ("ffn1_norm", "ffn1_w_gate", "ffn1_w_up", "ffn1_w_down")),
                 ("f2", ("ffn2_norm", "ffn2_w_gate", "ffn2_w_up", "ffn2_w_down")))
    transposed = ("ffn1_w_gate", "ffn1_w_up", "ffn2_w_gate", "ffn2_w_up")

    def gain(v):
        return v.reshape(1, -1)

    def head_gain(g3):
        return jnp.tile(g3[:, None, :], (1, H, 1)).reshape(1, n_grp * hw)

    def bf_t(w):
        return jnp.swapaxes(w, 0, 1).astype(BF)

    gq = _Queue()
    gathered = {}
    gq.push("gather", "v_norm", jnp.pad(gmlp_v_norm, ((0, 8 - n_a), (0, 0))))
    for l in range(depth):
        for tag, (_, wgn, wun, wdn) in ffn_names:
            if tag == "f2":
                if l < n_a:
                    gq.push("gather", ("gmlp_w_in", l), bf_t(gmlp_w_in[l]))
                    gq.push("gather", ("gmlp_w_out", l), gmlp_w_out[l].astype(BF))
                else:
                    gq.push("gather", ("attn_w_q", l - n_a), bf_t(attn_w_q[l - n_a]))
                    gq.push("gather", ("attn_w_o", l - n_a), attn_w_o[l - n_a].astype(BF))
            gq.push("gather", (wgn, l), bf_t(W[wgn][l]))
            gq.push("gather", (wun, l), bf_t(W[wun][l]))
            gq.push("gather", (wdn, l), W[wdn][l].astype(BF))
        if l == n_a - 1:
            gq.push("gather", "w_kv", bf_t(w_kv))

    def land(items, outs):
        for (_, key, _), o in zip(items, outs):
            gathered[key] = o

    def need(*keys):
        items = gq.take_keys([k for k in keys if k not in gathered])
        if items:
            land(items, _comm_only(_Carried([(k, a) for k, _, a in items]), "allgather"))
        return [gathered[k] for k in keys]

    def carry(q, kind):
        items = q.take(BUDGET_US[kind])
        return items, _Carried([(k, a) for k, _, a in items])

    v_all = need("v_norm", *[(n, 0) for n in ffn_names[0][1][1:]])[0]
    v_gain_all = jnp.transpose(v_all[:, :n_a], (1, 0, 2)).reshape(n_a, -1)

    cur = x.reshape(S, D)
    saved = []
    k_sh = kv_raw = kv_hn = kv_x = k_gain = None

    for l in range(depth):
        rec = {}
        for tag, (nn, wgn, wun, wdn) in ffn_names:
            if tag == "f2":
                rec["mix_x"] = cur
                if l < n_a:
                    w_in, w_out = need(("gmlp_w_in", l), ("gmlp_w_out", l))
                    bias = jnp.repeat(gmlp_b_s[l].T, LANE, axis=1)
                    items, car = carry(gq, "normproj")
                    (hm, z), couts = _normproj(cur, gain(mix_norm[l]), w_in, "gmlp_in", car)
                    land(items, couts)
                    items, car = carry(gq, "gmlp_fwd")
                    t, couts = _gmlp_fwd(z, gain(v_gain_all[l]), gmlp_w_s[l], bias, car)
                    land(items, couts)
                    items, car = carry(gq, "rowproj")
                    cur, couts = _rowproj(cur, t, w_out, "proj_out", car)
                    land(items, couts)
                    rec.update(w_in=w_in, w_out=w_out, bias=bias, hm=hm, z=z, t=t)
                else:
                    jj = l - n_a
                    w_q, w_o = need(("attn_w_q", jj), ("attn_w_o", jj))
                    items, car = carry(gq, "normproj")
                    (hm, q_raw), couts = _normproj(cur, gain(mix_norm[l]), w_q, "attn_q", car)
                    land(items, couts)
                    qg = head_gain(attn_q_norm[jj])
                    items, car = carry(gq, "headnorm")
                    q, couts = _headnorm_fwd(q_raw, n_grp * hw, qg, q_scale, "headnorm_q", car)
                    land(items, couts)
                    outs = []
                    for g, dil in enumerate(DILATIONS):
                        items, car = carry(gq, "attn_fwd")
                        og, couts = _attn_fwd(q, k_sh, kv_raw, g, dil, car)
                        land(items, couts)
                        outs.append(og)
                    items, car = carry(gq, "attn_combine")
                    (o, lse), couts = _attn_combine([o_ for o_, _ in outs], [l_ for _, l_ in outs], car)
                    land(items, couts)
                    items, car = carry(gq, "rowproj")
                    cur, couts = _rowproj(cur, o, w_o, "proj_out", car)
                    land(items, couts)
                    rec.update(w_q=w_q, w_o=w_o, hm=hm, q_raw=q_raw, qg=qg, q=q, o=o, lse=lse)
            wg, wu, wd = need((wgn, l), (wun, l), (wdn, l))
            rec[tag + "_x"] = cur
            items, car = carry(gq, "ffn_fwd")
            (cur, hn, act, ga, gb), couts = _ffn_fwd(cur, gain(W[nn][l]), wg, wu, wd, car)
            land(items, couts)
            rec[tag] = (wg, wu, wd, hn, act, ga, gb)
        if l == n_a - 1:
            (w_kv_g,) = need("w_kv")
            kv_x = cur
            items, car = carry(gq, "normproj")
            (kv_hn, kv_raw), couts = _normproj(cur, gain(kv_norm), w_kv_g, "kv_proj", car)
            land(items, couts)
            k_gain = head_gain(k_norm)
            items, car = carry(gq, "headnorm")
            k_sh, couts = _headnorm_fwd(kv_raw, n_grp * hw, k_gain, 1.0, "headnorm_k", car)
            land(items, couts)
        saved.append(rec)

    dcur, loss_part = _loss_head(cur, loss_target.reshape(S, D))
    loss = lax.psum(loss_part[0, 0], ("x", "y", "c"))

    rq = _Queue()
    reduced = {}
    small = {}

    def rs_land(items, outs):
        for (kind, key, arr), o in zip(items, outs):
            if kind == "core":
                rq.push("chip", key, _pair_add(arr, o, core))
            else:
                reduced[key] = (arr, o)

    def rs_carry(kind):
        items = rq.take(BUDGET_US[kind])
        return items, _Carried([(k, a) for k, _, a in items])

    def wgrad(key, a, a_kind, b, b_kind, name):
        items, car = rs_carry("wgrad")
        p, couts = _wgrad(a, a_kind, b, b_kind, name, car)
        rs_land(items, couts)
        rq.push("core", key, p)

    def put(name, l, val, n_layers):
        small.setdefault(name, [None] * n_layers)[l] = val

    dk_dv = None
    for l in reversed(range(depth)):
        rec = saved[l]
        if l == n_a - 1:
            dkv, dkg = _headnorm_bwd(dk_dv[0], kv_raw, k_gain, 1.0, "headnorm_k_bwd", tail=dk_dv[1])
            wgrad(("w_kv", 0), kv_hn, "full", dkv, "cols", "wgrad_kv")
            items, car = rs_carry("dgrad_col_norm")
            (dcur, dg), couts = _dgrad_col_norm(dcur, kv_x, gain(kv_norm), dkv, gathered["w_kv"], "dgrad_kv", car)
            rs_land(items, couts)
            small["kv_norm"] = [dg[0]]
            small["k_norm"] = [dkg[0].reshape(n_grp, H, LANE).sum(axis=1)]
        for tag, (nn, wgn, wun, wdn) in reversed(ffn_names):
            wg, wu, wd, hn, act, ga, gb = rec[tag]
            items, car = rs_carry("ffn_bwd_dx")
            (dcur, da, db, dyh, dg), couts = _ffn_bwd_dx(dcur, rec[tag + "_x"], gain(W[nn][l]), ga, gb, wg, wu, wd, car)
            rs_land(items, couts)
            wgrad((wgn, l), da, "stack", hn, "full", "wgrad_ffn_in")
            wgrad((wun, l), db, "stack", hn, "full", "wgrad_ffn_in")
            wgrad((wdn, l), act, "stack", dyh, "full", "wgrad_ffn_out")
            put(nn, l, dg[0], depth)
            if tag == "f2":
                mix_x = rec["mix_x"]
                if l < n_a:
                    dt = _dgrad_row(dcur, rec["w_out"], BF, "dgrad_gmlp_out")
                    wgrad(("gmlp_w_out", l), rec["t"], "cols", dcur, "full", "wgrad_proj_out")
                    dz, dws, dbias, dvg = _gmlp_bwd(rec["z"], dt, gain(v_gain_all[l]), gmlp_w_s[l], rec["bias"])
                    wgrad(("gmlp_w_in", l), rec["hm"], "full", dz, "cols", "wgrad_gmlp_in")
                    items, car = rs_carry("dgrad_col_norm")
                    (dcur, dg), couts = _dgrad_col_norm(dcur, mix_x, gain(mix_norm[l]), dz, rec["w_in"], "dgrad_gmlp_in", car)
                    rs_land(items, couts)
                    put("gmlp_w_s", l, dws, n_a)
                    put("gmlp_b_s", l, dbias[:, ::LANE].T, n_a)
                    put("gmlp_v_norm", l, dvg[0], n_a)
                else:
                    jj = l - n_a
                    do = _dgrad_row(dcur, rec["w_o"], F32, "dgrad_attn_out")
                    wgrad(("attn_w_o", jj), rec["o"], "cols", dcur, "full", "wgrad_proj_out")
                    dl = _attn_delta(do, rec["o"])
                    dq = None
                    first_layer = dk_dv is None
                    for g, dil in enumerate(DILATIONS):
                        dq = _attn_bwd_dq(rec["q"], k_sh, kv_raw, do, rec["lse"], dl, g, dil, dq)
                        dk_dv = _attn_bwd_dkv(rec["q"], k_sh, kv_raw, do, rec["lse"], dl, g, dil, dk_dv,
                                              accumulate=not first_layer)
                    dq_raw, dqg = _headnorm_bwd(dq, rec["q_raw"], rec["qg"], q_scale, "headnorm_q_bwd")
                    wgrad(("attn_w_q", jj), rec["hm"], "full", dq_raw, "cols", "wgrad_attn_q")
                    items, car = rs_carry("dgrad_col_norm")
                    (dcur, dg), couts = _dgrad_col_norm(dcur, mix_x, gain(mix_norm[l]), dq_raw, rec["w_q"], "dgrad_attn_q", car)
                    rs_land(items, couts)
                    put("attn_q_norm", jj, dqg[0].reshape(n_grp, H, LANE).sum(axis=1), depth - n_a)
                put("mix_norm", l, dg[0], depth)
    grad_x = dcur.reshape(1, S, D)

    for kind in ("core", "chip"):
        items = rq.take_kind(kind)
        if items:
            rs_land(items, _comm_only(_Carried([(k, a) for k, _, a in items]), "rs_%s_exchange" % kind))

    out_g, out_d, out_m, out_v = {}, {}, {}, {}
    for name in names:
        if (name, 0) not in reduced:
            continue
        if name in transposed:
            as3 = lambda t: jnp.swapaxes(t, 1, 2)
        else:
            as3 = (lambda t: t[None]) if W[name].ndim == 2 else (lambda t: t)
        res = None
        for l in range(as3(W[name]).shape[0]):
            qsum, r2 = reduced[(name, l)]
            res = _adamw_shard(qsum, r2, chip, as3(W[name]), as3(M[name]), as3(V[name]), l, res)
        for dct, val in zip((out_g, out_d, out_m, out_v), res):
            dct[name] = jnp.swapaxes(val, 1, 2) if name in transposed else (val[0] if W[name].ndim == 2 else val)

    small_names = [n for n in names if n in small]
    full_shape = {n: (W[n].shape if n != "gmlp_v_norm" else (n_a, v_gain_all.shape[1])) for n in small_names}
    flat = jnp.concatenate([jnp.stack(small[n]).reshape(-1) if W[n].ndim > 1 else small[n][0].reshape(-1)
                            for n in small_names])
    n_flat = flat.shape[0]
    rows = -(-n_flat // (8 * LANE)) * 8

    def pack(parts_list):
        v = jnp.concatenate([p.reshape(-1) for p in parts_list])
        return jnp.pad(v, (0, rows * LANE - n_flat)).reshape(rows, LANE)

    def full_of(dct, n, fill):
        if n != "gmlp_v_norm":
            return dct[n]
        sh = dct[n].shape[1]
        return lax.dynamic_update_slice(jnp.full(full_shape[n], fill, F32), dct[n], (0, dev * sh))

    (g_all,) = _comm_only(_Carried([("gather", pack([flat]))]), "allgather_small_grads")
    w_p = pack([full_of(W, n, 0.0) for n in small_names])
    m_p = pack([full_of(M, n, 0.0) for n in small_names])
    v_p = pack([full_of(V, n, 1.0) for n in small_names])
    packed = _adamw_replicated(g_all, w_p, m_p, v_p)
    offs = 0
    for n in small_names:
        size = math.prod(full_shape[n])
        for dct, arr in zip((out_g, out_d, out_m, out_v), packed):
            val = arr.reshape(-1)[offs:offs + size].reshape(full_shape[n])
            if n == "gmlp_v_norm":
                sh = W[n].shape[1]
                val = lax.dynamic_slice(val, (0, dev * sh), (n_a, sh))
            dct[n] = val
        offs += size

    return (loss, grad_x, *[out_g[n] for n in names], *[out_d[n] for n in names],
            *[out_m[n] for n in names], *[out_v[n] for n in names])
```

```python
import math

import jax
import jax.numpy as jnp
from jax import lax
from jax.experimental import pallas as pl
from jax.experimental.pallas import tpu as pltpu

F32 = jnp.float32
BF = jnp.bfloat16
N_DEV = 8
N_CHIP = 4
LANE = 128
ATT_WIN = 16 * LANE
EPS = 1e-6
NEG = -1e30
DILATIONS = (1, 4, 16)
ADAM_LR, ADAM_B1, ADAM_B2, ADAM_EPS, ADAM_WD, ADAM_STEP = 0.001, 0.9, 0.999, 1e-08, 0.01, 10
GELU_C0, GELU_C1 = 0.7978845608028654, 0.044715
VMEM_MB = 2 ** 20
BUDGET_US = {"ffn_fwd": 380.0, "normproj": 150.0, "rowproj": 100.0, "gmlp_fwd": 50.0, "headnorm": 75.0,
             "attn_fwd": 120.0, "attn_combine": 75.0,
             "ffn_bwd_dx": 520.0, "wgrad": 60.0, "wgrad_last": 110.0, "dgrad_col_norm": 280.0}
COST_US_PER_ELEM = {"gather": 0.8e-4, "core": 1.4e-5, "chip": 9.0e-5}

MESH_T = pl.DeviceIdType.MESH
ANY = pl.BlockSpec(memory_space=pl.ANY)
DMA_SEM = pltpu.SemaphoreType.DMA
NT = (((1,), (1,)), ((), ()))
TN = (((0,), (0,)), ((), ()))


def _tile(n, target, mult):
    best = None
    for t in range(mult, min(n, target) + 1, mult):
        if n % t == 0:
            best = t
    if best is None:
        best = n
    return best


def _dot(a, b, dims=None):
    if dims is None:
        return jnp.dot(a, b, preferred_element_type=F32)
    return lax.dot_general(a, b, dims, preferred_element_type=F32)


def _rms_hat(xv):
    r = lax.rsqrt(jnp.mean(xv * xv, axis=-1, keepdims=True) + EPS)
    return xv * r, r


def _rms_bwd(dhn, xv, gain, dres):
    xhat, r = _rms_hat(xv)
    dxhat = dhn * gain
    dx = dres + r * (dxhat - xhat * jnp.mean(dxhat * xhat, axis=-1, keepdims=True))
    return dx, jnp.sum(dhn * xhat, axis=0, keepdims=True)


def _accum_rows(ref, row, first):
    val = jnp.broadcast_to(row, ref.shape)

    @pl.when(first)
    def _():
        ref[...] = val

    @pl.when(jnp.logical_not(first))
    def _():
        ref[...] += val


def _gelu(z):
    t = jnp.tanh(GELU_C0 * (z + GELU_C1 * z * z * z))
    return 0.5 * z * (1.0 + t), t


def _gelu_grad(z, t):
    return 0.5 * (1.0 + t) + 0.5 * z * (1.0 - t * t) * GELU_C0 * (1.0 + 3.0 * GELU_C1 * z * z)


def _me():
    return lax.axis_index("x"), lax.axis_index("y"), lax.axis_index("c")


def _gather_phase(phase, x_ref, out_ref, send_sems, recv_sems, local_sem):
    x, y, c = _me()
    me, sibling = (x, y, c), (x, y, 1 - c)
    x_nbr, y_nbr, diag = (1 - x, y), (x, 1 - y), (1 - x, 1 - y)
    src_chip = (x + (1 - 2 * x) * (1 - c), y + (1 - 2 * y) * c)
    dst_chip = (x + (1 - 2 * x) * c, y + (1 - 2 * y) * (1 - c))

    def slot(px, py, pc):
        return out_ref.at[4 * px + 2 * py + pc]

    def copy(k, block, to, src=None):
        return pltpu.make_async_remote_copy(
            src_ref=slot(*block) if src is None else src, dst_ref=slot(*block),
            send_sem=send_sems.at[k], recv_sem=recv_sems.at[k], device_id=to, device_id_type=MESH_T)

    mine = pltpu.make_async_copy(x_ref, slot(*me), local_sem)
    first = [copy(0, me, sibling, src=x_ref), copy(1, me, (*x_nbr, c), src=x_ref), copy(2, me, (*y_nbr, c), src=x_ref)]
    second = [copy(3, (*x_nbr, c), sibling), copy(4, (*y_nbr, c), sibling), copy(5, (*src_chip, c), (*dst_chip, c))]
    third = [copy(6, (*diag, c), sibling)]
    if phase == 0:
        mine.start()
        for cp in first:
            cp.start()
    elif phase == 1:
        copy(1, (*x_nbr, c), me).wait_recv()
        copy(2, (*y_nbr, c), me).wait_recv()
        for cp in second:
            cp.start()
    elif phase == 2:
        copy(5, (*diag, c), me).wait_recv()
        third[0].start()
    else:
        copy(0, sibling, me).wait_recv()
        copy(3, (*x_nbr, 1 - c), me).wait_recv()
        copy(4, (*y_nbr, 1 - c), me).wait_recv()
        copy(6, (*diag, 1 - c), me).wait_recv()
        for cp in first + second + third:
            cp.wait_send()
        mine.wait()


def _exchange_phase(phase, kind, src_ref, dst_ref, send_sems, recv_sems):
    x, y, c = _me()
    if kind == "core":
        plan = [(2 * k + (1 - c), k, (x, y, 1 - c)) for k in range(N_CHIP)]
    else:
        plan = [(2 * px + py, t, (px, py, c)) for t, (px, py) in enumerate([(1 - x, y), (x, 1 - y), (1 - x, 1 - y)])]
    cps = [pltpu.make_async_remote_copy(
        src_ref=src_ref.at[s], dst_ref=dst_ref.at[d], send_sem=send_sems.at[i], recv_sem=recv_sems.at[i],
        device_id=to, device_id_type=MESH_T) for i, (s, d, to) in enumerate(plan)]
    if phase == 0:
        for cp in cps:
            cp.start()
    elif phase == N_PHASES - 1:
        for cp in cps:
            cp.wait()


_N_COPIES = {"gather": 7, "core": N_CHIP, "chip": 3}
N_PHASES = 4


class _Carried:
    def __init__(self, items=()):
        self.items = list(items)

    def arrays(self):
        return [a for _, a in self.items]

    def out_shapes(self):
        lead = {"gather": lambda a: (N_DEV,) + a.shape, "core": lambda a: (N_CHIP,) + a.shape[1:],
                "chip": lambda a: (3,) + a.shape[1:]}
        return [jax.ShapeDtypeStruct(lead[k](a), a.dtype) for k, a in self.items]

    def scratch(self):
        res = []
        for k, _ in self.items:
            res += [DMA_SEM((_N_COPIES[k],)), DMA_SEM((_N_COPIES[k],))]
            if k == "gather":
                res.append(DMA_SEM(()))
        return res

    def emit(self, phase, in_refs, out_refs, scr):
        i = 0
        for (kind, _), src, dst in zip(self.items, in_refs, out_refs):
            if kind == "gather":
                _gather_phase(phase, src, dst, scr[i], scr[i + 1], scr[i + 2])
                i += 3
            else:
                _exchange_phase(phase, kind, src, dst, scr[i], scr[i + 1])
                i += 2


def _comm_only(carried, name):
    nc = len(carried.items)

    def body(*refs):
        for phase in range(N_PHASES):
            carried.emit(phase, refs[:nc], refs[nc:2 * nc], refs[2 * nc:])

    return pl.pallas_call(
        body, name=name, out_shape=carried.out_shapes(), in_specs=[ANY] * nc, out_specs=[ANY] * nc,
        scratch_shapes=carried.scratch(),
    )(*carried.arrays())


def _pcall(main, *, name, grid, in_specs, out_specs, out_shape, args, scratch=(), sem=None, vmem=48,
           carried=None, aliases=None):
    params = pltpu.CompilerParams(dimension_semantics=sem, vmem_limit_bytes=vmem * VMEM_MB)
    n_in, n_out, n_scr = len(in_specs), len(out_specs), len(scratch)
    if carried is None or not carried.items:
        outs = pl.pallas_call(
            main, name=name, grid=grid, in_specs=in_specs, out_specs=out_specs, out_shape=out_shape,
            scratch_shapes=list(scratch), compiler_params=params, input_output_aliases=aliases or {},
        )(*args)
        return list(outs), []
    nc = len(carried.items)
    total = math.prod(grid)

    def body(*refs):
        ins, cin = refs[:n_in], refs[n_in:n_in + nc]
        o0 = n_in + nc
        outs, cout = refs[o0:o0 + n_out], refs[o0 + n_out:o0 + n_out + nc]
        s0 = o0 + n_out + nc
        scr, cscr = refs[s0:s0 + n_scr], refs[s0 + n_scr:]
        step = 0
        for d, n in enumerate(grid):
            step = step * n + pl.program_id(d)

        @pl.when(step == 0)
        def _():
            carried.emit(0, cin, cout, cscr)

        main(*ins, *outs, *scr)

        for phase, at in ((1, (5 * total) // 8), (2, (7 * total) // 8), (3, total - 1)):
            @pl.when(step == min(at, total - 1))
            def _(phase=phase):
                carried.emit(phase, cin, cout, cscr)

    outs = pl.pallas_call(
        body, name=name, grid=grid, in_specs=list(in_specs) + [ANY] * nc, out_specs=list(out_specs) + [ANY] * nc,
        out_shape=list(out_shape) + carried.out_shapes(), scratch_shapes=list(scratch) + carried.scratch(),
        compiler_params=params, input_output_aliases=aliases or {},
    )(*args, *carried.arrays())
    return list(outs[:n_out]), list(outs[n_out:])


def _pair_add(p, r1, core):
    _, rows, cols = p.shape
    tr = _tile(rows, 512, 16)
    p4 = p.reshape(N_CHIP, 2, rows, cols)

    def body(core_ref, p_ref, r_ref, q_ref):
        q_ref[...] = (p_ref[...].astype(F32) + r_ref[...].astype(F32)).astype(BF)

    grid_spec = pltpu.PrefetchScalarGridSpec(
        num_scalar_prefetch=1, grid=(N_CHIP, rows // tr),
        in_specs=[pl.BlockSpec((None, None, tr, cols), lambda k, i, cr: (k, cr[0], i, 0)),
                  pl.BlockSpec((None, tr, cols), lambda k, i, cr: (k, i, 0))],
        out_specs=pl.BlockSpec((None, tr, cols), lambda k, i, cr: (k, i, 0)))
    return pl.pallas_call(
        body, name="pair_add", grid_spec=grid_spec, out_shape=jax.ShapeDtypeStruct((N_CHIP, rows, cols), BF),
        compiler_params=pltpu.CompilerParams(dimension_semantics=("parallel", "parallel")),
    )(core, p4, r1)


def _adam_math(g, w, m, v):
    m2 = ADAM_B1 * m + (1.0 - ADAM_B1) * g
    v2 = ADAM_B2 * v + (1.0 - ADAM_B2) * (g * g)
    m_hat = m2 / (1.0 - ADAM_B1 ** ADAM_STEP)
    v_hat = v2 / (1.0 - ADAM_B2 ** ADAM_STEP)
    delta = -ADAM_LR * (m_hat / (jnp.sqrt(v_hat) + ADAM_EPS) + ADAM_WD * w)
    return delta, m2, v2


def _adamw_shard(q, r2, chip, w, m, v, layer, prev):
    n_layers, rows, cols = w.shape
    tr = _tile(rows, 256, 16)

    def body(chip_ref, q_ref, r_ref, w_ref, m_ref, v_ref, *rest):
        g_ref, d_ref, m2_ref, v2_ref = rest[-4:]
        g = q_ref[...].astype(F32) + r_ref[0].astype(F32) + r_ref[1].astype(F32) + r_ref[2].astype(F32)
        d, m2, v2 = _adam_math(g, w_ref[...], m_ref[...], v_ref[...])
        g_ref[...] = g
        d_ref[...] = d
        m2_ref[...] = m2
        v2_ref[...] = v2

    blk = pl.BlockSpec((None, tr, cols), lambda i, cr: (layer, i, 0))
    in_specs = [pl.BlockSpec((None, tr, cols), lambda i, cr: (cr[0], i, 0)),
                pl.BlockSpec((3, tr, cols), lambda i, cr: (0, i, 0)), blk, blk, blk]
    args = [chip, q, r2, w, m, v]
    aliases = {}
    if prev is not None:
        in_specs += [ANY] * 4
        args += list(prev)
        aliases = {6 + i: i for i in range(4)}
    grid_spec = pltpu.PrefetchScalarGridSpec(
        num_scalar_prefetch=1, grid=(rows // tr,), in_specs=in_specs, out_specs=[blk, blk, blk, blk])
    out = jax.ShapeDtypeStruct((n_layers, rows, cols), F32)
    return pl.pallas_call(
        body, name="adamw_shard", grid_spec=grid_spec, out_shape=[out, out, out, out], input_output_aliases=aliases,
        compiler_params=pltpu.CompilerParams(dimension_semantics=("parallel",)),
    )(*args)


def _adamw_replicated(parts, w, m, v):
    rows, cols = w.shape
    tr = _tile(rows, 512, 8)

    def body(p_ref, w_ref, m_ref, v_ref, g_ref, d_ref, m2_ref, v2_ref):
        g = p_ref[0]
        for k in range(1, N_DEV):
            g = g + p_ref[k]
        d, m2, v2 = _adam_math(g, w_ref[...], m_ref[...], v_ref[...])
        g_ref[...] = g
        d_ref[...] = d
        m2_ref[...] = m2
        v2_ref[...] = v2

    blk = pl.BlockSpec((tr, cols), lambda i: (i, 0))
    out = jax.ShapeDtypeStruct((rows, cols), F32)
    outs, _ = _pcall(body, name="adamw_replicated", grid=(rows // tr,),
                     in_specs=[pl.BlockSpec((N_DEV, tr, cols), lambda i: (0, i, 0)), blk, blk, blk],
                     out_specs=[blk, blk, blk, blk], out_shape=[out, out, out, out], sem=("parallel",),
                     args=(parts, w, m, v))
    return outs


def _ffn_fwd(x, gain, wg, wu, wd, carried=None):
    S, D = x.shape
    nsh, fs, _ = wg.shape
    tm = _tile(S, 512, 16)

    def body(x_ref, g_ref, wg_ref, wu_ref, wd_ref, y_ref, hn_ref, act_ref, ga_ref, gb_ref, acc_ref):
        j = pl.program_id(1)

        @pl.when(j == 0)
        def _():
            xhat, _ = _rms_hat(x_ref[...])
            hn_ref[...] = (xhat * g_ref[...]).astype(BF)
            acc_ref[...] = jnp.zeros_like(acc_ref)

        hn = hn_ref[...]
        a = _dot(hn, wg_ref[...], NT)
        b = _dot(hn, wu_ref[...], NT)
        sg = jax.nn.sigmoid(a)
        sil = a * sg
        act = (sil * b).astype(BF)
        act_ref[...] = act
        ga_ref[...] = (b * (sg * (1.0 + a * (1.0 - sg)))).astype(BF)
        gb_ref[...] = sil.astype(BF)
        acc_ref[...] += _dot(act, wd_ref[...])

        @pl.when(j == nsh - 1)
        def _():
            y_ref[...] = x_ref[...] + 0.5 * acc_ref[...]

    row = pl.BlockSpec((tm, D), lambda m, j: (m, 0))
    hid = pl.BlockSpec((None, tm, fs), lambda m, j: (j, m, 0))
    return _pcall(
        body, name="ffn_fwd", grid=(S // tm, nsh),
        in_specs=[row, pl.BlockSpec((1, D), lambda m, j: (0, 0)),
                  pl.BlockSpec((None, fs, D), lambda m, j: (j, 0, 0)),
                  pl.BlockSpec((None, fs, D), lambda m, j: (j, 0, 0)),
                  pl.BlockSpec((None, fs, D), lambda m, j: (j, 0, 0))],
        out_specs=[row, row, hid, hid, hid],
        out_shape=[jax.ShapeDtypeStruct((S, D), F32), jax.ShapeDtypeStruct((S, D), BF)]
        + [jax.ShapeDtypeStruct((nsh, S, fs), BF)] * 3,
        scratch=[pltpu.VMEM((tm, D), F32)], sem=("parallel", "arbitrary"), vmem=58,
        args=(x, gain, wg, wu, wd), carried=carried)


def _ffn_bwd_dx(dy, x, gain, ga, gb, wg, wu, wd, carried=None):
    S, D = x.shape
    nsh, fs, _ = wg.shape
    tm = _tile(S, 512, 16)

    def body(dy_ref, x_ref, g_ref, ga_ref, gb_ref, wg_ref, wu_ref, wd_ref,
             dx_ref, da_ref, db_ref, dyh_ref, dg_ref, acc_ref):
        m, j = pl.program_id(0), pl.program_id(1)

        @pl.when(j == 0)
        def _():
            dyh_ref[...] = (0.5 * dy_ref[...]).astype(BF)
            acc_ref[...] = jnp.zeros_like(acc_ref)

        dact = _dot(dyh_ref[...], wd_ref[...], NT)
        da = (dact * ga_ref[...].astype(F32)).astype(BF)
        db = (dact * gb_ref[...].astype(F32)).astype(BF)
        da_ref[...] = da
        db_ref[...] = db
        acc_ref[...] += _dot(da, wg_ref[...])
        acc_ref[...] += _dot(db, wu_ref[...])

        @pl.when(j == nsh - 1)
        def _():
            dx, dgain = _rms_bwd(acc_ref[...], x_ref[...], g_ref[...], dy_ref[...])
            dx_ref[...] = dx
            _accum_rows(dg_ref, dgain, m == 0)

    row = pl.BlockSpec((tm, D), lambda m, j: (m, 0))
    row1 = pl.BlockSpec((tm, D), lambda m, j: (m, 0), pipeline_mode=pl.Buffered(1))
    hid = pl.BlockSpec((None, tm, fs), lambda m, j: (j, m, 0))
    hshape = jax.ShapeDtypeStruct((nsh, S, fs), BF)
    return _pcall(
        body, name="ffn_bwd_dx", grid=(S // tm, nsh),
        in_specs=[row1, row1, pl.BlockSpec((1, D), lambda m, j: (0, 0)), hid, hid,
                  pl.BlockSpec((None, fs, D), lambda m, j: (j, 0, 0)),
                  pl.BlockSpec((None, fs, D), lambda m, j: (j, 0, 0)),
                  pl.BlockSpec((None, fs, D), lambda m, j: (j, 0, 0))],
        out_specs=[row1, hid, hid, row1, pl.BlockSpec((8, D), lambda m, j: (0, 0))],
        out_shape=[jax.ShapeDtypeStruct((S, D), F32), hshape, hshape,
                   jax.ShapeDtypeStruct((S, D), BF), jax.ShapeDtypeStruct((8, D), F32)],
        scratch=[pltpu.VMEM((tm, D), F32)], sem=("arbitrary", "arbitrary"), vmem=60,
        args=(dy, x, gain, ga, gb, wg, wu, wd), carried=carried)


def _opspec(arr, kind, tm, grid_mj):
    if kind == "full":
        return pl.BlockSpec((tm, arr.shape[1]), lambda *g: (grid_mj(*g)[0], 0)), arr.shape[1]
    if kind == "cols":
        n = arr.shape[1] // N_DEV
        return pl.BlockSpec((tm, n), lambda *g: grid_mj(*g)), n
    n = arr.shape[2]
    return pl.BlockSpec((None, tm, n), lambda *g: (grid_mj(*g)[1], grid_mj(*g)[0], 0)), n


def _wgrad(a, a_kind, b, b_kind, name, carried=None):
    S = a.shape[0] if a_kind != "stack" else a.shape[1]
    tm = _tile(S, 1024, 16)
    mj = lambda j, m: (m, j)
    a_spec, ka = _opspec(a, a_kind, tm, mj)
    b_spec, nb = _opspec(b, b_kind, tm, mj)
    n_m = S // tm

    def body(a_ref, b_ref, o_ref, acc_ref):
        m = pl.program_id(1)

        @pl.when(m == 0)
        def _():
            acc_ref[...] = jnp.zeros_like(acc_ref)

        acc_ref[...] += _dot(a_ref[...].astype(BF), b_ref[...].astype(BF), TN)

        @pl.when(m == n_m - 1)
        def _():
            o_ref[...] = acc_ref[...].astype(BF)

    outs, cout = _pcall(
        body, name=name, grid=(N_DEV, n_m), in_specs=[a_spec, b_spec],
        out_specs=[pl.BlockSpec((None, ka, nb), lambda j, m: (j, 0, 0))],
        out_shape=[jax.ShapeDtypeStruct((N_DEV, ka, nb), BF)],
        scratch=[pltpu.VMEM((ka, nb), F32)], sem=("parallel", "arbitrary"), args=(a, b), carried=carried)
    return outs[0], cout


def _normproj(x, gain, w, name, carried=None):
    S, D = x.shape
    _, n, _ = w.shape
    tm = _tile(S, 512, 16)

    def body(x_ref, g_ref, w_ref, hn_ref, y_ref):
        @pl.when(pl.program_id(1) == 0)
        def _():
            xhat, _ = _rms_hat(x_ref[...])
            hn_ref[...] = (xhat * g_ref[...]).astype(BF)

        y_ref[...] = _dot(hn_ref[...], w_ref[...], NT)

    row = pl.BlockSpec((tm, D), lambda m, j: (m, 0))
    return _pcall(
        body, name=name, grid=(S // tm, N_DEV),
        in_specs=[row, pl.BlockSpec((1, D), lambda m, j: (0, 0)), pl.BlockSpec((None, n, D), lambda m, j: (j, 0, 0))],
        out_specs=[row, pl.BlockSpec((tm, n), lambda m, j: (m, j))],
        out_shape=[jax.ShapeDtypeStruct((S, D), BF), jax.ShapeDtypeStruct((S, N_DEV * n), F32)],
        sem=("parallel", "arbitrary"), args=(x, gain, w), carried=carried)


def _rowproj(x, t, w, name, carried=None):
    S, D = x.shape
    _, k, _ = w.shape
    tm = _tile(S, 512, 16)

    def body(x_ref, t_ref, w_ref, y_ref):
        j = pl.program_id(1)
        part = _dot(t_ref[...], w_ref[...])

        @pl.when(j == 0)
        def _():
            y_ref[...] = x_ref[...] + part

        @pl.when(j > 0)
        def _():
            y_ref[...] += part

    row = pl.BlockSpec((tm, D), lambda m, j: (m, 0))
    outs, cout = _pcall(
        body, name=name, grid=(S // tm, N_DEV),
        in_specs=[row, pl.BlockSpec((tm, k), lambda m, j: (m, j)), pl.BlockSpec((None, k, D), lambda m, j: (j, 0, 0))],
        out_specs=[row], out_shape=[jax.ShapeDtypeStruct((S, D), F32)],
        sem=("parallel", "arbitrary"), args=(x, t, w), carried=carried)
    return outs[0], cout


def _dgrad_row(dy, w, out_dtype, name):
    S, D = dy.shape
    _, k, _ = w.shape
    tm = _tile(S, 512, 16)

    def body(dy_ref, w_ref, dt_ref, dyb_ref):
        @pl.when(pl.program_id(1) == 0)
        def _():
            dyb_ref[...] = dy_ref[...].astype(BF)

        dt_ref[...] = _dot(dyb_ref[...], w_ref[...], NT).astype(out_dtype)

    outs, _ = _pcall(
        body, name=name, grid=(S // tm, N_DEV),
        in_specs=[pl.BlockSpec((tm, D), lambda m, j: (m, 0)), pl.BlockSpec((None, k, D), lambda m, j: (j, 0, 0))],
        out_specs=[pl.BlockSpec((tm, k), lambda m, j: (m, j))],
        out_shape=[jax.ShapeDtypeStruct((S, N_DEV * k), out_dtype)],
        scratch=[pltpu.VMEM((tm, D), BF)], sem=("parallel", "arbitrary"), args=(dy, w))
    return outs[0]


def _dgrad_col_norm(dres, x, gain, dz, w, name, carried=None):
    S, D = x.shape
    _, n, _ = w.shape
    tm = _tile(S, 512, 16)

    def body(dres_ref, x_ref, g_ref, dz_ref, w_ref, dx_ref, dg_ref, acc_ref):
        m, j = pl.program_id(0), pl.program_id(1)

        @pl.when(j == 0)
        def _():
            acc_ref[...] = jnp.zeros_like(acc_ref)

        acc_ref[...] += _dot(dz_ref[...], w_ref[...])

        @pl.when(j == N_DEV - 1)
        def _():
            dx, dgain = _rms_bwd(acc_ref[...], x_ref[...], g_ref[...], dres_ref[...])
            dx_ref[...] = dx
            _accum_rows(dg_ref, dgain, m == 0)

    row = pl.BlockSpec((tm, D), lambda m, j: (m, 0), pipeline_mode=pl.Buffered(1))
    return _pcall(
        body, name=name, grid=(S // tm, N_DEV),
        in_specs=[row, row, pl.BlockSpec((1, D), lambda m, j: (0, 0)), pl.BlockSpec((tm, n), lambda m, j: (m, j)),
                  pl.BlockSpec((None, n, D), lambda m, j: (j, 0, 0))],
        out_specs=[row, pl.BlockSpec((8, D), lambda m, j: (0, 0))],
        out_shape=[jax.ShapeDtypeStruct((S, D), F32), jax.ShapeDtypeStruct((8, D), F32)],
        scratch=[pltpu.VMEM((tm, D), F32)], sem=("arbitrary", "arbitrary"),
        args=(dres, x, gain, dz, w), carried=carried)


def _headnorm_fwd(xa, width, gain_row, scale, name, carried=None):
    S = xa.shape[0]
    cb = _tile(width, 1024, LANE)
    tm = _tile(S, 512, 8)

    def body(x_ref, g_ref, y_ref):
        for c in range(cb // LANE):
            sl = slice(c * LANE, (c + 1) * LANE)
            xhat, _ = _rms_hat(x_ref[:, sl])
            y_ref[:, sl] = xhat * (g_ref[:, sl] * scale)

    outs, couts = _pcall(
        body, name=name, grid=(S // tm, width // cb),
        in_specs=[pl.BlockSpec((tm, cb), lambda m, c: (m, c)), pl.BlockSpec((1, cb), lambda m, c: (0, c))],
        out_specs=[pl.BlockSpec((tm, cb), lambda m, c: (m, c))],
        out_shape=[jax.ShapeDtypeStruct((S, width), F32)], sem=("parallel", "parallel"), args=(xa, gain_row),
        carried=carried)
    return outs[0], couts


def _headnorm_bwd(dy, xa, gain_row, scale, name, tail=None):
    S, width = dy.shape
    cb = _tile(width, 1024, LANE)
    tm = _tile(S, 512, 16)
    ncb = width // cb
    ntail = 0 if tail is None else tail.shape[1] // cb

    def body(dy_ref, x_ref, g_ref, *rest):
        dx_ref, dg_ref = rest[-2:]
        c, m = pl.program_id(0), pl.program_id(1)

        @pl.when(c < ncb)
        def _():
            rows = []
            for i in range(cb // LANE):
                sl = slice(i * LANE, (i + 1) * LANE)
                dx, dgain = _rms_bwd(dy_ref[:, sl] * scale, x_ref[:, sl], g_ref[:, sl], 0.0)
                dx_ref[:, sl] = dx.astype(BF)
                rows.append(dgain)
            _accum_rows(dg_ref, jnp.concatenate(rows, axis=1), m == 0)

        if tail is not None:
            @pl.when(c >= ncb)
            def _():
                dx_ref[...] = rest[0][...].astype(BF)

    head = lambda c: jnp.minimum(c, ncb - 1)
    in_specs = [pl.BlockSpec((tm, cb), lambda c, m: (jnp.where(c < ncb, m, 0), head(c))),
                pl.BlockSpec((tm, cb), lambda c, m: (jnp.where(c < ncb, m, 0), head(c))),
                pl.BlockSpec((1, cb), lambda c, m: (0, head(c)))]
    args = [dy, xa, gain_row]
    if tail is not None:
        in_specs.append(pl.BlockSpec((tm, cb), lambda c, m: (jnp.where(c >= ncb, m, 0), jnp.maximum(c - ncb, 0))))
        args.append(tail)
    outs, _ = _pcall(
        body, name=name, grid=(ncb + ntail, S // tm), in_specs=in_specs,
        out_specs=[pl.BlockSpec((tm, cb), lambda c, m: (m, c)), pl.BlockSpec((8, cb), lambda c, m: (0, head(c)))],
        out_shape=[jax.ShapeDtypeStruct((S, width + ntail * cb), BF), jax.ShapeDtypeStruct((8, width), F32)],
        sem=("arbitrary", "arbitrary"), args=args)
    return outs


def _causal():
    p = lax.broadcasted_iota(jnp.int32, (LANE, LANE), 0)
    q = lax.broadcasted_iota(jnp.int32, (LANE, LANE), 1)
    return p >= q


def _gmlp_fwd(z, v_gain, ws, bias, carried=None):
    S, dg2 = z.shape
    dg = dg2 // 2
    G = dg // LANE

    def body(z_ref, vg_ref, ws_ref, bias_ref, t_ref):
        u, _ = _gelu(z_ref[:, :dg])
        v, _ = _gelu(z_ref[:, dg:])
        vhat, _ = _rms_hat(v)
        vn = (vhat * vg_ref[...]).astype(BF)
        mask = _causal()
        for g in range(G):
            sl = slice(g * LANE, (g + 1) * LANE)
            wm = jnp.where(mask, ws_ref[g], 0.0).astype(BF)
            sv = _dot(wm, vn[:, sl]) + bias_ref[:, sl]
            t_ref[:, sl] = (u[:, sl] * sv).astype(BF)

    outs, couts = _pcall(
        body, name="gmlp_fwd", grid=(S // LANE,),
        in_specs=[pl.BlockSpec((LANE, dg2), lambda n: (n, 0)), pl.BlockSpec((1, dg), lambda n: (0, 0)),
                  pl.BlockSpec((G, LANE, LANE), lambda n: (0, 0, 0)), pl.BlockSpec((LANE, dg), lambda n: (0, 0))],
        out_specs=[pl.BlockSpec((LANE, dg), lambda n: (n, 0))],
        out_shape=[jax.ShapeDtypeStruct((S, dg), BF)], sem=("parallel",), args=(z, v_gain, ws, bias),
        carried=carried)
    return outs[0], couts


def _gmlp_bwd(z, dt, v_gain, ws, bias):
    S, dg2 = z.shape
    dg = dg2 // 2
    G = dg // LANE

    def body(z_ref, dt_ref, vg_ref, ws_ref, bias_ref, dz_ref, dws_ref, db_ref, dvg_ref, dvn_ref):
        n = pl.program_id(0)
        zu, zv = z_ref[:, :dg], z_ref[:, dg:]
        u, tu = _gelu(zu)
        v, tv = _gelu(zv)
        vhat, r = _rms_hat(v)
        vn = (vhat * vg_ref[...]).astype(BF)
        mask = _causal()

        @pl.when(n == 0)
        def _():
            dws_ref[...] = jnp.zeros_like(dws_ref)
            db_ref[...] = jnp.zeros_like(db_ref)

        for g in range(G):
            sl = slice(g * LANE, (g + 1) * LANE)
            wm = jnp.where(mask, ws_ref[g], 0.0).astype(BF)
            sv = _dot(wm, vn[:, sl]) + bias_ref[:, sl]
            dtg = dt_ref[:, sl].astype(F32)
            dz_ref[:, sl] = (dtg * sv * _gelu_grad(zu[:, sl], tu[:, sl])).astype(BF)
            dsv = dtg * u[:, sl]
            dsvb = dsv.astype(BF)
            dvn_ref[:, sl] = _dot(wm, dsvb, TN)
            dws_ref[g] += jnp.where(mask, _dot(dsvb, vn[:, sl], NT), 0.0)
            db_ref[:, sl] += jnp.broadcast_to(jnp.sum(dsv, axis=1, keepdims=True), (LANE, LANE))

        dvn = dvn_ref[...]
        dxhat = dvn * vg_ref[...]
        dv = r * (dxhat - vhat * jnp.mean(dxhat * vhat, axis=-1, keepdims=True))
        dz_ref[:, dg:] = (dv * _gelu_grad(zv, tv)).astype(BF)
        _accum_rows(dvg_ref, jnp.sum(dvn * vhat, axis=0, keepdims=True), n == 0)

    outs, _ = _pcall(
        body, name="gmlp_bwd", grid=(S // LANE,),
        in_specs=[pl.BlockSpec((LANE, dg2), lambda n: (n, 0)), pl.BlockSpec((LANE, dg), lambda n: (n, 0)),
                  pl.BlockSpec((1, dg), lambda n: (0, 0)), pl.BlockSpec((G, LANE, LANE), lambda n: (0, 0, 0)),
                  pl.BlockSpec((LANE, dg), lambda n: (0, 0))],
        out_specs=[pl.BlockSpec((LANE, dg2), lambda n: (n, 0)), pl.BlockSpec((G, LANE, LANE), lambda n: (0, 0, 0)),
                   pl.BlockSpec((LANE, dg), lambda n: (0, 0)), pl.BlockSpec((8, dg), lambda n: (0, 0))],
        out_shape=[jax.ShapeDtypeStruct((S, dg2), BF), jax.ShapeDtypeStruct((G, LANE, LANE), F32),
                   jax.ShapeDtypeStruct((LANE, dg), F32), jax.ShapeDtypeStruct((8, dg), F32)],
        scratch=[pltpu.VMEM((LANE, dg), F32)], sem=("arbitrary",), args=(z, dt, v_gain, ws, bias))
    return outs


N_ENT = ATT_WIN // LANE


def _rows(ref, start, dil):
    return ref[pl.ds(start, LANE), :] if dil == 1 else ref[pl.ds(start, LANE, stride=dil), :]


def _rows_store(ref, start, dil, val):
    if dil == 1:
        ref[pl.ds(start, LANE), :] = val
    else:
        ref[pl.ds(start, LANE, stride=dil), :] = val


def _slope_times_dil(h, n_heads, dil, shape):
    hv = jnp.zeros(shape, F32) + (h + 1).astype(F32)
    return jnp.exp(hv * (-8.0 / n_heads * math.log(2.0))) * float(dil)


def _band_bias(h, n_heads, dil, has_prev):
    qi = lax.broadcasted_iota(jnp.int32, (LANE, 2 * LANE), 0)
    kj = lax.broadcasted_iota(jnp.int32, (LANE, 2 * LANE), 1)
    delta = qi + LANE - kj
    valid = (delta >= 0) & (delta <= LANE) & ((kj >= LANE) | has_prev)
    return jnp.where(valid, -_slope_times_dil(h, n_heads, dil, (LANE, 2 * LANE)) * delta.astype(F32), NEG)


def _attn_geom(S, H, g, dil):
    pb = LANE * dil
    nblk = ATT_WIN // pb
    nw = S // ATT_WIN
    win = lambda c0: pl.BlockSpec((ATT_WIN, LANE), lambda h, w: (w, c0 + h))
    prev = lambda c0: pl.BlockSpec((pb, LANE), lambda h, w: (jnp.maximum(w * nblk - 1, 0), c0 + h))
    nxt = lambda c0: pl.BlockSpec((pb, LANE), lambda h, w: (jnp.minimum((w + 1) * nblk, nw * nblk - 1), c0 + h))
    return pb, nblk, nw, g * H, win, prev, nxt


def _stage_band(dst, cur_ref, prev_ref, dil, pb, nblk):
    for blk in range(nblk):
        for r in range(dil):
            e = blk * dil + r
            dst[e, :LANE] = _rows(prev_ref, r, dil) if blk == 0 else _rows(cur_ref, (blk - 1) * pb + r, dil)
            dst[e, LANE:] = _rows(cur_ref, blk * pb + r, dil)


def _stage(dst, ref, dil, pb, nblk, lead=None):
    for blk in range(nblk):
        for r in range(dil):
            val = _rows(ref, blk * pb + r, dil)
            if lead is None:
                dst[blk * dil + r] = val
            else:
                dst[lead, blk * dil + r] = val


def _unstage(ref, src, dil, pb, nblk):
    for blk in range(nblk):
        for r in range(dil):
            _rows_store(ref, blk * pb + r, dil, src[blk * dil + r])


def _attn_fwd(q, k, kv, g, dil, carried=None):
    S = q.shape[0]
    H = q.shape[1] // (3 * LANE)
    pb, nblk, nw, col, win, prev, _ = _attn_geom(S, H, g, dil)
    vcol = 3 * H + col

    def body(q_ref, kc_ref, kp_ref, vc_ref, vp_ref, o_ref, l_ref, qs, ks, vs, os_, ls):
        h, w = pl.program_id(0), pl.program_id(1)
        _stage(qs, q_ref, dil, pb, nblk)
        _stage_band(ks, kc_ref, kp_ref, dil, pb, nblk)
        _stage_band(vs, vc_ref, vp_ref, dil, pb, nblk)

        def run(lo, hi, bias):
            def step(e, carry):
                s = _dot(qs[e].astype(BF), ks[e].astype(BF), NT) + bias
                mx = jnp.max(s, axis=-1, keepdims=True)
                p = jnp.exp(s - mx)
                l = jnp.sum(p, axis=-1, keepdims=True)
                os_[e] = _dot((p / l).astype(BF), vs[e].astype(BF))
                ls[e] = jnp.broadcast_to(mx + jnp.log(l), (LANE, LANE))
                return carry
            if hi > lo:
                lax.fori_loop(lo, hi, step, 0, unroll=True)

        run(0, dil, _band_bias(h, H, dil, w > 0))
        run(dil, N_ENT, _band_bias(h, H, dil, True))
        _unstage(o_ref, os_, dil, pb, nblk)
        _unstage(l_ref, ls, dil, pb, nblk)

    out = jax.ShapeDtypeStruct((S, H * LANE), F32)
    sq = pltpu.VMEM((N_ENT, LANE, LANE), F32)
    sk = pltpu.VMEM((N_ENT, 2 * LANE, LANE), F32)
    return _pcall(
        body, name="attn_fwd_d%d" % dil, grid=(H, nw),
        in_specs=[win(col), win(col), prev(col), win(vcol), prev(vcol)],
        out_specs=[win(0), win(0)], out_shape=[out, out], scratch=[sq, sk, sk, sq, sq],
        sem=("parallel", "parallel"), args=(q, k, k, kv, kv), carried=carried)


def _attn_combine(os_, ls_, carried=None):
    S, C = os_[0].shape
    tm = _tile(S, 256, 16)

    def body(o0, o1, o2, l0, l1, l2, o_ref, lse_ref):
        a, b, c = l0[...], l1[...], l2[...]
        mx = jnp.maximum(jnp.maximum(a, b), c)
        ea, eb, ec = jnp.exp(a - mx), jnp.exp(b - mx), jnp.exp(c - mx)
        den = ea + eb + ec
        o_ref[...] = ((ea * o0[...] + eb * o1[...] + ec * o2[...]) / den).astype(BF)
        lse_ref[...] = mx + jnp.log(den)

    blk = pl.BlockSpec((tm, C), lambda m: (m, 0))
    return _pcall(
        body, name="attn_combine", grid=(S // tm,), in_specs=[blk] * 6, out_specs=[blk, blk],
        out_shape=[jax.ShapeDtypeStruct((S, C), BF), jax.ShapeDtypeStruct((S, C), F32)],
        sem=("parallel",), args=(*os_, *ls_), carried=carried)


def _attn_delta(do, o):
    S, C = do.shape
    tm = _tile(S, 512, 16)

    def body(do_ref, o_ref, d_ref):
        for c in range(C // LANE):
            sl = slice(c * LANE, (c + 1) * LANE)
            prod = do_ref[:, sl].astype(BF).astype(F32) * o_ref[:, sl].astype(F32)
            d_ref[:, sl] = jnp.broadcast_to(jnp.sum(prod, axis=-1, keepdims=True), (tm, LANE))

    blk = pl.BlockSpec((tm, C), lambda m: (m, 0))
    outs, _ = _pcall(
        body, name="attn_delta", grid=(S // tm,), in_specs=[blk, blk], out_specs=[blk],
        out_shape=[jax.ShapeDtypeStruct((S, C), F32)], sem=("parallel",), args=(do, o))
    return outs[0]


def _attn_bwd_dq(q, k, kv, do, lse, dl, g, dil, dq_prev):
    S = q.shape[0]
    H = q.shape[1] // (3 * LANE)
    pb, nblk, nw, col, win, prev, _ = _attn_geom(S, H, g, dil)
    vcol = 3 * H + col

    def body(q_ref, kc_ref, kp_ref, vc_ref, vp_ref, do_ref, l_ref, d_ref, *rest):
        dq_ref, qs, ks, vs, dos, ls, ds_, dqs = rest[-8:]
        h, w = pl.program_id(0), pl.program_id(1)
        _stage(qs, q_ref, dil, pb, nblk)
        _stage_band(ks, kc_ref, kp_ref, dil, pb, nblk)
        _stage_band(vs, vc_ref, vp_ref, dil, pb, nblk)
        _stage(dos, do_ref, dil, pb, nblk)
        _stage(ls, l_ref, dil, pb, nblk)
        _stage(ds_, d_ref, dil, pb, nblk)

        def run(lo, hi, bias):
            def step(e, carry):
                kb = ks[e].astype(BF)
                s = _dot(qs[e].astype(BF), kb, NT) + bias
                p = jnp.exp(s - ls[e][:, :1])
                dp = _dot(dos[e].astype(BF), vs[e].astype(BF), NT)
                dsc = p * (dp - ds_[e][:, :1])
                dqs[e] = _dot(dsc.astype(BF), kb)
                return carry
            if hi > lo:
                lax.fori_loop(lo, hi, step, 0, unroll=True)

        run(0, dil, _band_bias(h, H, dil, w > 0))
        run(dil, N_ENT, _band_bias(h, H, dil, True))
        _unstage(dq_ref, dqs, dil, pb, nblk)

    sq = pltpu.VMEM((N_ENT, LANE, LANE), F32)
    sk = pltpu.VMEM((N_ENT, 2 * LANE, LANE), F32)
    in_specs = [win(col), win(col), prev(col), win(vcol), prev(vcol), win(0), win(0), win(0)]
    args = [q, k, k, kv, kv, do, lse, dl]
    aliases = {}
    if dq_prev is not None:
        in_specs.append(ANY)
        args.append(dq_prev)
        aliases = {8: 0}
    outs, _ = _pcall(
        body, name="attn_bwd_dq_d%d" % dil, grid=(H, nw), in_specs=in_specs,
        out_specs=[win(col)], out_shape=[jax.ShapeDtypeStruct((S, 3 * H * LANE), F32)],
        scratch=[sq, sk, sk, sq, sq, sq, sq], sem=("parallel", "parallel"), args=args, aliases=aliases)
    return outs[0]


def _attn_bwd_dkv(q, k, kv, do, lse, dl, g, dil, prev_out, accumulate):
    S = q.shape[0]
    H = q.shape[1] // (3 * LANE)
    pb, nblk, nw, col, win, _, nxt = _attn_geom(S, H, g, dil)
    vcol = 3 * H + col
    n_q = N_ENT + dil

    def body(k_ref, v_ref, qc_ref, qn_ref, doc_ref, don_ref, lc_ref, ln_ref, dc_ref, dn_ref, *rest):
        dk_ref, dv_ref, ks, vs, qs, dos, ls, ds_, dks, dvs = rest[-10:]
        h, w = pl.program_id(0), pl.program_id(1)
        _stage(ks, k_ref, dil, pb, nblk)
        _stage(vs, v_ref, dil, pb, nblk)
        for dst, cur, nx in ((qs, qc_ref, qn_ref), (dos, doc_ref, don_ref), (ls, lc_ref, ln_ref), (ds_, dc_ref, dn_ref)):
            _stage(dst, cur, dil, pb, nblk)
            for r in range(dil):
                dst[N_ENT + r] = _rows(nx, r, dil)
        qi = lax.broadcasted_iota(jnp.int32, (LANE, LANE), 0)
        kj = lax.broadcasted_iota(jnp.int32, (LANE, LANE), 1)
        sd = _slope_times_dil(h, H, dil, (LANE, LANE))
        bias_c = jnp.where(qi >= kj, -sd * (qi - kj).astype(F32), NEG)

        def run(lo, hi, has_next):
            bias_n = jnp.where((qi <= kj) & has_next, -sd * (qi + LANE - kj).astype(F32), NEG)

            def step(e, carry):
                kb = ks[e].astype(BF)
                vb = vs[e].astype(BF)
                dk = jnp.zeros((LANE, LANE), F32)
                dv = jnp.zeros((LANE, LANE), F32)
                for eq, bias in ((e, bias_c), (e + dil, bias_n)):
                    qb = qs[eq].astype(BF)
                    dob = dos[eq].astype(BF)
                    s = _dot(qb, kb, NT) + bias
                    p = jnp.exp(s - ls[eq][:, :1])
                    dp = _dot(dob, vb, NT)
                    dsc = p * (dp - ds_[eq][:, :1])
                    dv = dv + _dot(p.astype(BF), dob, TN)
                    dk = dk + _dot(dsc.astype(BF), qb, TN)
                dks[e] = dk
                dvs[e] = dv
                return carry
            if hi > lo:
                lax.fori_loop(lo, hi, step, 0, unroll=True)

        run(0, N_ENT - dil, True)
        run(N_ENT - dil, N_ENT, w < nw - 1)
        if accumulate:
            pk_ref, pv_ref = rest[0], rest[1]
            dk_ref[...] = pk_ref[...]
            dv_ref[...] = pv_ref[...]
            for blk in range(nblk):
                for r in range(dil):
                    e, start = blk * dil + r, blk * pb + r
                    _rows_store(dk_ref, start, dil, _rows(dk_ref, start, dil) + dks[e])
                    _rows_store(dv_ref, start, dil, _rows(dv_ref, start, dil) + dvs[e])
        else:
            _unstage(dk_ref, dks, dil, pb, nblk)
            _unstage(dv_ref, dvs, dil, pb, nblk)

    s1 = pltpu.VMEM((N_ENT, LANE, LANE), F32)
    s2 = pltpu.VMEM((n_q, LANE, LANE), F32)
    out = jax.ShapeDtypeStruct((S, 3 * H * LANE), F32)
    in_specs = [win(col), win(vcol), win(col), nxt(col), win(0), nxt(0), win(0), nxt(0), win(0), nxt(0)]
    args = [k, kv, q, q, do, do, lse, lse, dl, dl]
    aliases = {}
    if prev_out is not None:
        in_specs += [win(col), win(col)] if accumulate else [ANY, ANY]
        args += list(prev_out)
        aliases = {10: 0, 11: 1}
    outs, _ = _pcall(
        body, name="attn_bwd_dkv_d%d%s" % (dil, "_acc" if accumulate else ""), grid=(H, nw), in_specs=in_specs,
        out_specs=[win(col), win(col)], out_shape=[out, out],
        scratch=[s1, s1, s2, s2, s2, s2, s1, s1], sem=("parallel", "parallel"), vmem=56, args=args, aliases=aliases)
    return outs


def _loss_head(y, target):
    S, D = y.shape
    tm = _tile(S, 512, 8)

    def body(y_ref, t_ref, dy_ref, l_ref):
        e = y_ref[...] - t_ref[...]
        dy_ref[...] = e * (1.0 / D)
        part = jnp.broadcast_to(jnp.sum(jnp.sum(e * e, axis=1, keepdims=True), axis=0, keepdims=True) * (0.5 / D), (8, LANE))
        _accum_rows(l_ref, part, pl.program_id(0) == 0)

    blk = pl.BlockSpec((tm, D), lambda m: (m, 0))
    outs, _ = _pcall(
        body, name="loss_head", grid=(S // tm,), in_specs=[blk, blk],
        out_specs=[blk, pl.BlockSpec((8, LANE), lambda m: (0, 0))],
        out_shape=[jax.ShapeDtypeStruct((S, D), F32), jax.ShapeDtypeStruct((8, LANE), F32)],
        sem=("arbitrary",), args=(y, target))
    return outs


class _Queue:
    def __init__(self):
        self.items = []

    def push(self, kind, key, arr):
        self.items.append((kind, key, arr))

    def take(self, budget_us, kinds=None):
        taken, left, spent = [], [], 0.0
        for item in self.items:
            if spent >= budget_us or (kinds is not None and item[0] not in kinds):
                left.append(item)
                continue
            taken.append(item)
            spent += COST_US_PER_ELEM[item[0]] * (item[2].size / item[2].shape[0] if item[0] != "gather" else item[2].size)
        self.items = left
        return taken

    def take_keys(self, keys):
        taken = [it for it in self.items if it[1] in keys]
        self.items = [it for it in self.items if it[1] not in keys]
        return taken

    def take_kind(self, kind):
        taken = [it for it in self.items if it[0] == kind]
        self.items = [it for it in self.items if it[0] != kind]
        return taken


def kernel(x, ffn1_norm, ffn1_w_gate, ffn1_w_up, ffn1_w_down, mix_norm, ffn2_norm, ffn2_w_gate, ffn2_w_up, ffn2_w_down, gmlp_w_in, gmlp_v_norm, gmlp_w_s, gmlp_b_s, gmlp_w_out, kv_norm, w_kv, k_norm, attn_w_q, attn_q_norm, attn_w_o, loss_target, m_ffn1_norm, m_ffn1_w_gate, m_ffn1_w_up, m_ffn1_w_down, m_mix_norm, m_ffn2_norm, m_ffn2_w_gate, m_ffn2_w_up, m_ffn2_w_down, m_gmlp_w_in, m_gmlp_v_norm, m_gmlp_w_s, m_gmlp_b_s, m_gmlp_w_out, m_kv_norm, m_w_kv, m_k_norm, m_attn_w_q, m_attn_q_norm, m_attn_w_o, v_ffn1_norm, v_ffn1_w_gate, v_ffn1_w_up, v_ffn1_w_down, v_mix_norm, v_ffn2_norm, v_ffn2_w_gate, v_ffn2_w_up, v_ffn2_w_down, v_gmlp_w_in, v_gmlp_v_norm, v_gmlp_w_s, v_gmlp_b_s, v_gmlp_w_out, v_kv_norm, v_w_kv, v_k_norm, v_attn_w_q, v_attn_q_norm, v_attn_w_o):
    names = ["ffn1_norm", "ffn1_w_gate", "ffn1_w_up", "ffn1_w_down", "mix_norm", "ffn2_norm", "ffn2_w_gate",
             "ffn2_w_up", "ffn2_w_down", "gmlp_w_in", "gmlp_v_norm", "gmlp_w_s", "gmlp_b_s", "gmlp_w_out",
             "kv_norm", "w_kv", "k_norm", "attn_w_q", "attn_q_norm", "attn_w_o"]
    W = dict(zip(names, [ffn1_norm, ffn1_w_gate, ffn1_w_up, ffn1_w_down, mix_norm, ffn2_norm, ffn2_w_gate,
                         ffn2_w_up, ffn2_w_down, gmlp_w_in, gmlp_v_norm, gmlp_w_s, gmlp_b_s, gmlp_w_out,
                         kv_norm, w_kv, k_norm, attn_w_q, attn_q_norm, attn_w_o]))
    M = dict(zip(names, [m_ffn1_norm, m_ffn1_w_gate, m_ffn1_w_up, m_ffn1_w_down, m_mix_norm, m_ffn2_norm, m_ffn2_w_gate,
                         m_ffn2_w_up, m_ffn2_w_down, m_gmlp_w_in, m_gmlp_v_norm, m_gmlp_w_s, m_gmlp_b_s, m_gmlp_w_out,
                         m_kv_norm, m_w_kv, m_k_norm, m_attn_w_q, m_attn_q_norm, m_attn_w_o]))
    V = dict(zip(names, [v_ffn1_norm, v_ffn1_w_gate, v_ffn1_w_up, v_ffn1_w_down, v_mix_norm, v_ffn2_norm, v_ffn2_w_gate,
                         v_ffn2_w_up, v_ffn2_w_down, v_gmlp_w_in, v_gmlp_v_norm, v_gmlp_w_s, v_gmlp_b_s, v_gmlp_w_out,
                         v_kv_norm, v_w_kv, v_k_norm, v_attn_w_q, v_attn_q_norm, v_attn_w_o]))

    depth = ffn1_norm.shape[0]
    n_a = gmlp_w_in.shape[0]
    S, D = x.shape[1], x.shape[2]
    H = D // LANE
    n_grp = len(DILATIONS)
    hw = H * LANE
    xi, yi, ci = _me()
    core = jnp.reshape(ci, (1,)).astype(jnp.int32)
    chip = jnp.reshape(2 * xi + yi, (1,)).astype(jnp.int32)
    dev = 4 * xi + 2 * yi + ci
    q_scale = LANE ** -0.5
    ffn_names = (("f1", ("ffn1_norm", "ffn1_w_gate", "ffn1_w_up", "ffn1_w_down")),
                 ("f2", ("ffn2_norm", "ffn2_w_gate", "ffn2_w_up", "ffn2_w_down")))
    transposed = ("ffn1_w_gate", "ffn1_w_up", "ffn2_w_gate", "ffn2_w_up")

    def gain(v):
        return v.reshape(1, -1)

    def head_gain(g3):
        return jnp.tile(g3[:, None, :], (1, H, 1)).reshape(1, n_grp * hw)

    def bf_t(w):
        return jnp.swapaxes(w, 0, 1).astype(BF)

    gq = _Queue()
    gathered = {}
    gq.push("gather", "v_norm", jnp.pad(gmlp_v_norm, ((0, 8 - n_a), (0, 0))))
    for l in range(depth):
        for tag, (_, wgn, wun, wdn) in ffn_names:
            if tag == "f2":
                if l < n_a:
                    gq.push("gather", ("gmlp_w_in", l), bf_t(gmlp_w_in[l]))
                    gq.push("gather", ("gmlp_w_out", l), gmlp_w_out[l].astype(BF))
                else:
                    gq.push("gather", ("attn_w_q", l - n_a), bf_t(attn_w_q[l - n_a]))
                    gq.push("gather", ("attn_w_o", l - n_a), attn_w_o[l - n_a].astype(BF))
            gq.push("gather", (wgn, l), bf_t(W[wgn][l]))
            gq.push("gather", (wun, l), bf_t(W[wun][l]))
            gq.push("gather", (wdn, l), W[wdn][l].astype(BF))
        if l == n_a - 1:
            gq.push("gather", "w_kv", bf_t(w_kv))

    def land(items, outs):
        for (_, key, _), o in zip(items, outs):
            gathered[key] = o

    def need(*keys):
        items = gq.take_keys([k for k in keys if k not in gathered])
        if items:
            land(items, _comm_only(_Carried([(k, a) for k, _, a in items]), "allgather"))
        return [gathered[k] for k in keys]

    def carry(q, kind):
        items = q.take(BUDGET_US[kind])
        return items, _Carried([(k, a) for k, _, a in items])

    v_all = need("v_norm", *[(n, 0) for n in ffn_names[0][1][1:]])[0]
    v_gain_all = jnp.transpose(v_all[:, :n_a], (1, 0, 2)).reshape(n_a, -1)

    cur = x.reshape(S, D)
    saved = []
    k_sh = kv_raw = kv_hn = kv_x = k_gain = None

    for l in range(depth):
        rec = {}
        for tag, (nn, wgn, wun, wdn) in ffn_names:
            if tag == "f2":
                rec["mix_x"] = cur
                if l < n_a:
                    w_in, w_out = need(("gmlp_w_in", l), ("gmlp_w_out", l))
                    bias = jnp.repeat(gmlp_b_s[l].T, LANE, axis=1)
                    items, car = carry(gq, "normproj")
                    (hm, z), couts = _normproj(cur, gain(mix_norm[l]), w_in, "gmlp_in", car)
                    land(items, couts)
                    items, car = carry(gq, "gmlp_fwd")
                    t, couts = _gmlp_fwd(z, gain(v_gain_all[l]), gmlp_w_s[l], bias, car)
                    land(items, couts)
                    items, car = carry(gq, "rowproj")
                    cur, couts = _rowproj(cur, t, w_out, "proj_out", car)
                    land(items, couts)
                    rec.update(w_in=w_in, w_out=w_out, bias=bias, hm=hm, z=z, t=t)
                else:
                    jj = l - n_a
                    w_q, w_o = need(("attn_w_q", jj), ("attn_w_o", jj))
                    items, car = carry(gq, "normproj")
                    (hm, q_raw), couts = _normproj(cur, gain(mix_norm[l]), w_q, "attn_q", car)
                    land(items, couts)
                    qg = head_gain(attn_q_norm[jj])
                    items, car = carry(gq, "headnorm")
                    q, couts = _headnorm_fwd(q_raw, n_grp * hw, qg, q_scale, "headnorm_q", car)
                    land(items, couts)
                    outs = []
                    for g, dil in enumerate(DILATIONS):
                        items, car = carry(gq, "attn_fwd")
                        og, couts = _attn_fwd(q, k_sh, kv_raw, g, dil, car)
                        land(items, couts)
                        outs.append(og)
                    items, car = carry(gq, "attn_combine")
                    (o, lse), couts = _attn_combine([o_ for o_, _ in outs], [l_ for _, l_ in outs], car)
                    land(items, couts)
                    items, car = carry(gq, "rowproj")
                    cur, couts = _rowproj(cur, o, w_o, "proj_out", car)
                    land(items, couts)
                    rec.update(w_q=w_q, w_o=w_o, hm=hm, q_raw=q_raw, qg=qg, q=q, o=o, lse=lse)
            wg, wu, wd = need((wgn, l), (wun, l), (wdn, l))
            rec[tag + "_x"] = cur
            items, car = carry(gq, "ffn_fwd")
            (cur, hn, act, ga, gb), couts = _ffn_fwd(cur, gain(W[nn][l]), wg, wu, wd, car)
            land(items, couts)
            rec[tag] = (wg, wu, wd, hn, act, ga, gb)
        if l == n_a - 1:
            (w_kv_g,) = need("w_kv")
            kv_x = cur
            items, car = carry(gq, "normproj")
            (kv_hn, kv_raw), couts = _normproj(cur, gain(kv_norm), w_kv_g, "kv_proj", car)
            land(items, couts)
            k_gain = head_gain(k_norm)
            items, car = carry(gq, "headnorm")
            k_sh, couts = _headnorm_fwd(kv_raw, n_grp * hw, k_gain, 1.0, "headnorm_k", car)
            land(items, couts)
        saved.append(rec)

    dcur, loss_part = _loss_head(cur, loss_target.reshape(S, D))
    loss = lax.psum(loss_part[0, 0], ("x", "y", "c"))

    rq = _Queue()
    reduced = {}
    small = {}

    def rs_land(items, outs):
        for (kind, key, arr), o in zip(items, outs):
            if kind == "core":
                rq.push("chip", key, _pair_add(arr, o, core))
            else:
                reduced[key] = (arr, o)

    def rs_carry(kind, kinds=None):
        items = rq.take(BUDGET_US[kind], kinds)
        return items, _Carried([(k, a) for k, _, a in items])

    def wgrad(key, a, a_kind, b, b_kind, name):
        last_block = key[1] == 0 and key[0].startswith("ffn1")
        items, car = rs_carry("wgrad_last" if last_block else "wgrad", None if last_block else ("core",))
        p, couts = _wgrad(a, a_kind, b, b_kind, name, car)
        rs_land(items, couts)
        rq.push("core", key, p)

    def put(name, l, val, n_layers):
        small.setdefault(name, [None] * n_layers)[l] = val

    dk_dv = None
    for l in reversed(range(depth)):
        rec = saved[l]
        if l == n_a - 1:
            dkv, dkg = _headnorm_bwd(dk_dv[0], kv_raw, k_gain, 1.0, "headnorm_k_bwd", tail=dk_dv[1])
            wgrad(("w_kv", 0), kv_hn, "full", dkv, "cols", "wgrad_kv")
            items, car = rs_carry("dgrad_col_norm")
            (dcur, dg), couts = _dgrad_col_norm(dcur, kv_x, gain(kv_norm), dkv, gathered["w_kv"], "dgrad_kv", car)
            rs_land(items, couts)
            small["kv_norm"] = [dg[0]]
            small["k_norm"] = [dkg[0].reshape(n_grp, H, LANE).sum(axis=1)]
        for tag, (nn, wgn, wun, wdn) in reversed(ffn_names):
            wg, wu, wd, hn, act, ga, gb = rec[tag]
            items, car = rs_carry("ffn_bwd_dx")
            (dcur, da, db, dyh, dg), couts = _ffn_bwd_dx(dcur, rec[tag + "_x"], gain(W[nn][l]), ga, gb, wg, wu, wd, car)
            rs_land(items, couts)
            wgrad((wgn, l), da, "stack", hn, "full", "wgrad_ffn_in")
            wgrad((wun, l), db, "stack", hn, "full", "wgrad_ffn_in")
            wgrad((wdn, l), act, "stack", dyh, "full", "wgrad_ffn_out")
            put(nn, l, dg[0], depth)
            if tag == "f2":
                mix_x = rec["mix_x"]
                if l < n_a:
                    dt = _dgrad_row(dcur, rec["w_out"], BF, "dgrad_gmlp_out")
                    wgrad(("gmlp_w_out", l), rec["t"], "cols", dcur, "full", "wgrad_proj_out")
                    dz, dws, dbias, dvg = _gmlp_bwd(rec["z"], dt, gain(v_gain_all[l]), gmlp_w_s[l], rec["bias"])
                    wgrad(("gmlp_w_in", l), rec["hm"], "full", dz, "cols", "wgrad_gmlp_in")
                    items, car = rs_carry("dgrad_col_norm")
                    (dcur, dg), couts = _dgrad_col_norm(dcur, mix_x, gain(mix_norm[l]), dz, rec["w_in"], "dgrad_gmlp_in", car)
                    rs_land(items, couts)
                    put("gmlp_w_s", l, dws, n_a)
                    put("gmlp_b_s", l, dbias[:, ::LANE].T, n_a)
                    put("gmlp_v_norm", l, dvg[0], n_a)
                else:
                    jj = l - n_a
                    do = _dgrad_row(dcur, rec["w_o"], F32, "dgrad_attn_out")
                    wgrad(("attn_w_o", jj), rec["o"], "cols", dcur, "full", "wgrad_proj_out")
                    dl = _attn_delta(do, rec["o"])
                    dq = None
                    first_layer = dk_dv is None
                    for g, dil in enumerate(DILATIONS):
                        dq = _attn_bwd_dq(rec["q"], k_sh, kv_raw, do, rec["lse"], dl, g, dil, dq)
                        dk_dv = _attn_bwd_dkv(rec["q"], k_sh, kv_raw, do, rec["lse"], dl, g, dil, dk_dv,
                                              accumulate=not first_layer)
                    dq_raw, dqg = _headnorm_bwd(dq, rec["q_raw"], rec["qg"], q_scale, "headnorm_q_bwd")
                    wgrad(("attn_w_q", jj), rec["hm"], "full", dq_raw, "cols", "wgrad_attn_q")
                    items, car = rs_carry("dgrad_col_norm")
                    (dcur, dg), couts = _dgrad_col_norm(dcur, mix_x, gain(mix_norm[l]), dq_raw, rec["w_q"], "dgrad_attn_q", car)
                    rs_land(items, couts)
                    put("attn_q_norm", jj, dqg[0].reshape(n_grp, H, LANE).sum(axis=1), depth - n_a)
                put("mix_norm", l, dg[0], depth)
    grad_x = dcur.reshape(1, S, D)

    for kind in ("core", "chip"):
        items = rq.take_kind(kind)
        if items:
            rs_land(items, _comm_only(_Carried([(k, a) for k, _, a in items]), "rs_%s_exchange" % kind))

    out_g, out_d, out_m, out_v = {}, {}, {}, {}
    for name in names:
        if (name, 0) not in reduced:
            continue
        if name in transposed:
            as3 = lambda t: jnp.swapaxes(t, 1, 2)
        else:
            as3 = (lambda t: t[None]) if W[name].ndim == 2 else (lambda t: t)
        res = None
        for l in range(as3(W[name]).shape[0]):
            qsum, r2 = reduced[(name, l)]
            res = _adamw_shard(qsum, r2, chip, as3(W[name]), as3(M[name]), as3(V[name]), l, res)
        for dct, val in zip((out_g, out_d, out_m, out_v), res):
            dct[name] = jnp.swapaxes(val, 1, 2) if name in transposed else (val[0] if W[name].ndim == 2 else val)

    small_names = [n for n in names if n in small]
    full_shape = {n: (W[n].shape if n != "gmlp_v_norm" else (n_a, v_gain_all.shape[1])) for n in small_names}
    flat = jnp.concatenate([jnp.stack(small[n]).reshape(-1) if W[n].ndim > 1 else small[n][0].reshape(-1)
                            for n in small_names])
    n_flat = flat.shape[0]
    rows = -(-n_flat // (8 * LANE)) * 8

    def pack(parts_list):
        v = jnp.concatenate([p.reshape(-1) for p in parts_list])
        return jnp.pad(v, (0, rows * LANE - n_flat)).reshape(rows, LANE)

    def full_of(dct, n, fill):
        if n != "gmlp_v_norm":
            return dct[n]
        sh = dct[n].shape[1]
        return lax.dynamic_update_slice(jnp.full(full_shape[n], fill, F32), dct[n], (0, dev * sh))

    (g_all,) = _comm_only(_Carried([("gather", pack([flat]))]), "allgather_small_grads")
    w_p = pack([full_of(W, n, 0.0) for n in small_names])
    m_p = pack([full_of(M, n, 0.0) for n in small_names])
    v_p = pack([full_of(V, n, 1.0) for n in small_names])
    packed = _adamw_replicated(g_all, w_p, m_p, v_p)
    offs = 0
    for n in small_names:
        size = math.prod(full_shape[n])
        for dct, arr in zip((out_g, out_d, out_m, out_v), packed):
            val = arr.reshape(-1)[offs:offs + size].reshape(full_shape[n])
            if n == "gmlp_v_norm":
                sh = W[n].shape[1]
                val = lax.dynamic_slice(val, (0, dev * sh), (n_a, sh))
            dct[n] = val
        offs += size

    return (loss, grad_x, *[out_g[n] for n in names], *[out_d[n] for n in names],
            *[out_m[n] for n in names], *[out_v[n] for n in names])
```

```python
import math

import jax
import jax.numpy as jnp
from jax import lax
from jax.experimental import pallas as pl
from jax.experimental.pallas import tpu as pltpu

F32 = jnp.float32
BF = jnp.bfloat16
N_DEV = 8
N_CHIP = 4
LANE = 128
ATT_WIN = 16 * LANE
EPS = 1e-6
NEG = -1e30
DILATIONS = (1, 4, 16)
ADAM_LR, ADAM_B1, ADAM_B2, ADAM_EPS, ADAM_WD, ADAM_STEP = 0.001, 0.9, 0.999, 1e-08, 0.01, 10
GELU_C0, GELU_C1 = 0.7978845608028654, 0.044715
VMEM_MB = 2 ** 20
BUDGET_US = {"ffn_fwd": 380.0, "normproj": 150.0, "rowproj": 100.0, "gmlp_fwd": 50.0, "headnorm": 75.0,
             "attn_fwd": 120.0, "attn_combine": 75.0,
             "ffn_bwd_dx": 500.0, "wgrad": 60.0, "wgrad_last": 110.0, "dgrad_col_norm": 110.0}
COST_US_PER_ELEM = {"gather": 0.8e-4, "core": 1.4e-5, "chip": 9.0e-5}

MESH_T = pl.DeviceIdType.MESH
ANY = pl.BlockSpec(memory_space=pl.ANY)
DMA_SEM = pltpu.SemaphoreType.DMA
NT = (((1,), (1,)), ((), ()))
TN = (((0,), (0,)), ((), ()))


def _tile(n, target, mult):
    best = None
    for t in range(mult, min(n, target) + 1, mult):
        if n % t == 0:
            best = t
    if best is None:
        best = n
    return best


def _dot(a, b, dims=None):
    if dims is None:
        return jnp.dot(a, b, preferred_element_type=F32)
    return lax.dot_general(a, b, dims, preferred_element_type=F32)


def _rms_hat(xv):
    r = lax.rsqrt(jnp.mean(xv * xv, axis=-1, keepdims=True) + EPS)
    return xv * r, r


def _rms_bwd(dhn, xv, gain, dres):
    xhat, r = _rms_hat(xv)
    dxhat = dhn * gain
    dx = dres + r * (dxhat - xhat * jnp.mean(dxhat * xhat, axis=-1, keepdims=True))
    return dx, jnp.sum(dhn * xhat, axis=0, keepdims=True)


def _accum_rows(ref, row, first):
    val = jnp.broadcast_to(row, ref.shape)

    @pl.when(first)
    def _():
        ref[...] = val

    @pl.when(jnp.logical_not(first))
    def _():
        ref[...] += val


def _gelu(z):
    t = jnp.tanh(GELU_C0 * (z + GELU_C1 * z * z * z))
    return 0.5 * z * (1.0 + t), t


def _gelu_grad(z, t):
    return 0.5 * (1.0 + t) + 0.5 * z * (1.0 - t * t) * GELU_C0 * (1.0 + 3.0 * GELU_C1 * z * z)


def _me():
    return lax.axis_index("x"), lax.axis_index("y"), lax.axis_index("c")


def _gather_phase(phase, x_ref, out_ref, send_sems, recv_sems, local_sem):
    x, y, c = _me()
    me, sibling = (x, y, c), (x, y, 1 - c)
    x_nbr, y_nbr, diag = (1 - x, y), (x, 1 - y), (1 - x, 1 - y)
    src_chip = (x + (1 - 2 * x) * (1 - c), y + (1 - 2 * y) * c)
    dst_chip = (x + (1 - 2 * x) * c, y + (1 - 2 * y) * (1 - c))

    def slot(px, py, pc):
        return out_ref.at[4 * px + 2 * py + pc]

    def copy(k, block, to, src=None):
        return pltpu.make_async_remote_copy(
            src_ref=slot(*block) if src is None else src, dst_ref=slot(*block),
            send_sem=send_sems.at[k], recv_sem=recv_sems.at[k], device_id=to, device_id_type=MESH_T)

    mine = pltpu.make_async_copy(x_ref, slot(*me), local_sem)
    first = [copy(0, me, sibling, src=x_ref), copy(1, me, (*x_nbr, c), src=x_ref), copy(2, me, (*y_nbr, c), src=x_ref)]
    second = [copy(3, (*x_nbr, c), sibling), copy(4, (*y_nbr, c), sibling), copy(5, (*src_chip, c), (*dst_chip, c))]
    third = [copy(6, (*diag, c), sibling)]
    if phase == 0:
        mine.start()
        for cp in first:
            cp.start()
    elif phase == 1:
        copy(1, (*x_nbr, c), me).wait_recv()
        copy(2, (*y_nbr, c), me).wait_recv()
        for cp in second:
            cp.start()
    elif phase == 2:
        copy(5, (*diag, c), me).wait_recv()
        third[0].start()
    else:
        copy(0, sibling, me).wait_recv()
        copy(3, (*x_nbr, 1 - c), me).wait_recv()
        copy(4, (*y_nbr, 1 - c), me).wait_recv()
        copy(6, (*diag, 1 - c), me).wait_recv()
        for cp in first + second + third:
            cp.wait_send()
        mine.wait()


def _exchange_phase(phase, kind, src_ref, dst_ref, send_sems, recv_sems):
    x, y, c = _me()
    if kind == "core":
        plan = [(2 * k + (1 - c), k, (x, y, 1 - c)) for k in range(N_CHIP)]
    else:
        plan = [(2 * px + py, t, (px, py, c)) for t, (px, py) in enumerate([(1 - x, y), (x, 1 - y), (1 - x, 1 - y)])]
    cps = [pltpu.make_async_remote_copy(
        src_ref=src_ref.at[s], dst_ref=dst_ref.at[d], send_sem=send_sems.at[i], recv_sem=recv_sems.at[i],
        device_id=to, device_id_type=MESH_T) for i, (s, d, to) in enumerate(plan)]
    if phase == 0:
        for cp in cps:
            cp.start()
    elif phase == N_PHASES - 1:
        for cp in cps:
            cp.wait()


_N_COPIES = {"gather": 7, "core": N_CHIP, "chip": 3}
N_PHASES = 4


class _Carried:
    def __init__(self, items=()):
        self.items = list(items)

    def arrays(self):
        return [a for _, a in self.items]

    def out_shapes(self):
        lead = {"gather": lambda a: (N_DEV,) + a.shape, "core": lambda a: (N_CHIP,) + a.shape[1:],
                "chip": lambda a: (3,) + a.shape[1:]}
        return [jax.ShapeDtypeStruct(lead[k](a), a.dtype) for k, a in self.items]

    def scratch(self):
        res = []
        for k, _ in self.items:
            res += [DMA_SEM((_N_COPIES[k],)), DMA_SEM((_N_COPIES[k],))]
            if k == "gather":
                res.append(DMA_SEM(()))
        return res

    def emit(self, phase, in_refs, out_refs, scr):
        i = 0
        for (kind, _), src, dst in zip(self.items, in_refs, out_refs):
            if kind == "gather":
                _gather_phase(phase, src, dst, scr[i], scr[i + 1], scr[i + 2])
                i += 3
            else:
                _exchange_phase(phase, kind, src, dst, scr[i], scr[i + 1])
                i += 2


def _comm_only(carried, name):
    nc = len(carried.items)

    def body(*refs):
        for phase in range(N_PHASES):
            carried.emit(phase, refs[:nc], refs[nc:2 * nc], refs[2 * nc:])

    return pl.pallas_call(
        body, name=name, out_shape=carried.out_shapes(), in_specs=[ANY] * nc, out_specs=[ANY] * nc,
        scratch_shapes=carried.scratch(),
    )(*carried.arrays())


def _pcall(main, *, name, grid, in_specs, out_specs, out_shape, args, scratch=(), sem=None, vmem=48,
           carried=None, aliases=None):
    params = pltpu.CompilerParams(dimension_semantics=sem, vmem_limit_bytes=vmem * VMEM_MB)
    n_in, n_out, n_scr = len(in_specs), len(out_specs), len(scratch)
    if carried is None or not carried.items:
        outs = pl.pallas_call(
            main, name=name, grid=grid, in_specs=in_specs, out_specs=out_specs, out_shape=out_shape,
            scratch_shapes=list(scratch), compiler_params=params, input_output_aliases=aliases or {},
        )(*args)
        return list(outs), []
    nc = len(carried.items)
    total = math.prod(grid)

    def body(*refs):
        ins, cin = refs[:n_in], refs[n_in:n_in + nc]
        o0 = n_in + nc
        outs, cout = refs[o0:o0 + n_out], refs[o0 + n_out:o0 + n_out + nc]
        s0 = o0 + n_out + nc
        scr, cscr = refs[s0:s0 + n_scr], refs[s0 + n_scr:]
        step = 0
        for d, n in enumerate(grid):
            step = step * n + pl.program_id(d)

        @pl.when(step == 0)
        def _():
            carried.emit(0, cin, cout, cscr)

        main(*ins, *outs, *scr)

        for phase, at in ((1, (5 * total) // 8), (2, (7 * total) // 8), (3, total - 1)):
            @pl.when(step == min(at, total - 1))
            def _(phase=phase):
                carried.emit(phase, cin, cout, cscr)

    outs = pl.pallas_call(
        body, name=name, grid=grid, in_specs=list(in_specs) + [ANY] * nc, out_specs=list(out_specs) + [ANY] * nc,
        out_shape=list(out_shape) + carried.out_shapes(), scratch_shapes=list(scratch) + carried.scratch(),
        compiler_params=params, input_output_aliases=aliases or {},
    )(*args, *carried.arrays())
    return list(outs[:n_out]), list(outs[n_out:])


def _pair_add(p, r1, core):
    _, rows, cols = p.shape
    tr = _tile(rows, 512, 16)
    p4 = p.reshape(N_CHIP, 2, rows, cols)

    def body(core_ref, p_ref, r_ref, q_ref):
        q_ref[...] = (p_ref[...].astype(F32) + r_ref[...].astype(F32)).astype(BF)

    grid_spec = pltpu.PrefetchScalarGridSpec(
        num_scalar_prefetch=1, grid=(N_CHIP, rows // tr),
        in_specs=[pl.BlockSpec((None, None, tr, cols), lambda k, i, cr: (k, cr[0], i, 0)),
                  pl.BlockSpec((None, tr, cols), lambda k, i, cr: (k, i, 0))],
        out_specs=pl.BlockSpec((None, tr, cols), lambda k, i, cr: (k, i, 0)))
    return pl.pallas_call(
        body, name="pair_add", grid_spec=grid_spec, out_shape=jax.ShapeDtypeStruct((N_CHIP, rows, cols), BF),
        compiler_params=pltpu.CompilerParams(dimension_semantics=("parallel", "parallel")),
    )(core, p4, r1)


def _adam_math(g, w, m, v):
    m2 = ADAM_B1 * m + (1.0 - ADAM_B1) * g
    v2 = ADAM_B2 * v + (1.0 - ADAM_B2) * (g * g)
    m_hat = m2 / (1.0 - ADAM_B1 ** ADAM_STEP)
    v_hat = v2 / (1.0 - ADAM_B2 ** ADAM_STEP)
    delta = -ADAM_LR * (m_hat / (jnp.sqrt(v_hat) + ADAM_EPS) + ADAM_WD * w)
    return delta, m2, v2


def _adamw_shard(q, r2, chip, w, m, v, layer, prev):
    n_layers, rows, cols = w.shape
    tr = _tile(rows, 256, 16)

    def body(chip_ref, q_ref, r_ref, w_ref, m_ref, v_ref, *rest):
        g_ref, d_ref, m2_ref, v2_ref = rest[-4:]
        g = q_ref[...].astype(F32) + r_ref[0].astype(F32) + r_ref[1].astype(F32) + r_ref[2].astype(F32)
        d, m2, v2 = _adam_math(g, w_ref[...], m_ref[...], v_ref[...])
        g_ref[...] = g
        d_ref[...] = d
        m2_ref[...] = m2
        v2_ref[...] = v2

    blk = pl.BlockSpec((None, tr, cols), lambda i, cr: (layer, i, 0))
    in_specs = [pl.BlockSpec((None, tr, cols), lambda i, cr: (cr[0], i, 0)),
                pl.BlockSpec((3, tr, cols), lambda i, cr: (0, i, 0)), blk, blk, blk]
    args = [chip, q, r2, w, m, v]
    aliases = {}
    if prev is not None:
        in_specs += [ANY] * 4
        args += list(prev)
        aliases = {6 + i: i for i in range(4)}
    grid_spec = pltpu.PrefetchScalarGridSpec(
        num_scalar_prefetch=1, grid=(rows // tr,), in_specs=in_specs, out_specs=[blk, blk, blk, blk])
    out = jax.ShapeDtypeStruct((n_layers, rows, cols), F32)
    return pl.pallas_call(
        body, name="adamw_shard", grid_spec=grid_spec, out_shape=[out, out, out, out], input_output_aliases=aliases,
        compiler_params=pltpu.CompilerParams(dimension_semantics=("parallel",)),
    )(*args)


def _adamw_replicated(parts, w, m, v):
    rows, cols = w.shape
    tr = _tile(rows, 512, 8)

    def body(p_ref, w_ref, m_ref, v_ref, g_ref, d_ref, m2_ref, v2_ref):
        g = p_ref[0]
        for k in range(1, N_DEV):
            g = g + p_ref[k]
        d, m2, v2 = _adam_math(g, w_ref[...], m_ref[...], v_ref[...])
        g_ref[...] = g
        d_ref[...] = d
        m2_ref[...] = m2
        v2_ref[...] = v2

    blk = pl.BlockSpec((tr, cols), lambda i: (i, 0))
    out = jax.ShapeDtypeStruct((rows, cols), F32)
    outs, _ = _pcall(body, name="adamw_replicated", grid=(rows // tr,),
                     in_specs=[pl.BlockSpec((N_DEV, tr, cols), lambda i: (0, i, 0)), blk, blk, blk],
                     out_specs=[blk, blk, blk, blk], out_shape=[out, out, out, out], sem=("parallel",),
                     args=(parts, w, m, v))
    return outs


def _ffn_fwd(x, gain, wg, wu, wd, carried=None):
    S, D = x.shape
    nsh, fs, _ = wg.shape
    tm = _tile(S, 512, 16)

    def body(x_ref, g_ref, wg_ref, wu_ref, wd_ref, y_ref, hn_ref, act_ref, ga_ref, gb_ref, acc_ref):
        j = pl.program_id(1)

        @pl.when(j == 0)
        def _():
            xhat, _ = _rms_hat(x_ref[...])
            hn_ref[...] = (xhat * g_ref[...]).astype(BF)
            acc_ref[...] = jnp.zeros_like(acc_ref)

        hn = hn_ref[...]
        a = _dot(hn, wg_ref[...], NT)
        b = _dot(hn, wu_ref[...], NT)
        sg = jax.nn.sigmoid(a)
        sil = a * sg
        act = (sil * b).astype(BF)
        act_ref[...] = act
        ga_ref[...] = (b * (sg * (1.0 + a * (1.0 - sg)))).astype(BF)
        gb_ref[...] = sil.astype(BF)
        acc_ref[...] += _dot(act, wd_ref[...])

        @pl.when(j == nsh - 1)
        def _():
            y_ref[...] = x_ref[...] + 0.5 * acc_ref[...]

    row = pl.BlockSpec((tm, D), lambda m, j: (m, 0))
    hid = pl.BlockSpec((None, tm, fs), lambda m, j: (j, m, 0))
    return _pcall(
        body, name="ffn_fwd", grid=(S // tm, nsh),
        in_specs=[row, pl.BlockSpec((1, D), lambda m, j: (0, 0)),
                  pl.BlockSpec((None, fs, D), lambda m, j: (j, 0, 0)),
                  pl.BlockSpec((None, fs, D), lambda m, j: (j, 0, 0)),
                  pl.BlockSpec((None, fs, D), lambda m, j: (j, 0, 0))],
        out_specs=[row, row, hid, hid, hid],
        out_shape=[jax.ShapeDtypeStruct((S, D), F32), jax.ShapeDtypeStruct((S, D), BF)]
        + [jax.ShapeDtypeStruct((nsh, S, fs), BF)] * 3,
        scratch=[pltpu.VMEM((tm, D), F32)], sem=("parallel", "arbitrary"), vmem=58,
        args=(x, gain, wg, wu, wd), carried=carried)


def _ffn_bwd_dx(dy, x, gain, ga, gb, wg, wu, wd, carried=None):
    S, D = x.shape
    nsh, fs, _ = wg.shape
    tm = _tile(S, 512, 16)

    def body(dy_ref, x_ref, g_ref, ga_ref, gb_ref, wg_ref, wu_ref, wd_ref,
             dx_ref, da_ref, db_ref, dyh_ref, dg_ref, acc_ref):
        m, j = pl.program_id(0), pl.program_id(1)

        @pl.when(j == 0)
        def _():
            dyh_ref[...] = (0.5 * dy_ref[...]).astype(BF)
            acc_ref[...] = jnp.zeros_like(acc_ref)

        dact = _dot(dyh_ref[...], wd_ref[...], NT)
        da = (dact * ga_ref[...].astype(F32)).astype(BF)
        db = (dact * gb_ref[...].astype(F32)).astype(BF)
        da_ref[...] = da
        db_ref[...] = db
        acc_ref[...] += _dot(da, wg_ref[...])
        acc_ref[...] += _dot(db, wu_ref[...])

        @pl.when(j == nsh - 1)
        def _():
            dx, dgain = _rms_bwd(acc_ref[...], x_ref[...], g_ref[...], dy_ref[...])
            dx_ref[...] = dx
            _accum_rows(dg_ref, dgain, m == 0)

    row = pl.BlockSpec((tm, D), lambda m, j: (m, 0))
    row1 = pl.BlockSpec((tm, D), lambda m, j: (m, 0), pipeline_mode=pl.Buffered(1))
    hid = pl.BlockSpec((None, tm, fs), lambda m, j: (j, m, 0))
    hshape = jax.ShapeDtypeStruct((nsh, S, fs), BF)
    return _pcall(
        body, name="ffn_bwd_dx", grid=(S // tm, nsh),
        in_specs=[row1, row1, pl.BlockSpec((1, D), lambda m, j: (0, 0)), hid, hid,
                  pl.BlockSpec((None, fs, D), lambda m, j: (j, 0, 0)),
                  pl.BlockSpec((None, fs, D), lambda m, j: (j, 0, 0)),
                  pl.BlockSpec((None, fs, D), lambda m, j: (j, 0, 0))],
        out_specs=[row1, hid, hid, row1, pl.BlockSpec((8, D), lambda m, j: (0, 0))],
        out_shape=[jax.ShapeDtypeStruct((S, D), F32), hshape, hshape,
                   jax.ShapeDtypeStruct((S, D), BF), jax.ShapeDtypeStruct((8, D), F32)],
        scratch=[pltpu.VMEM((tm, D), F32)], sem=("arbitrary", "arbitrary"), vmem=60,
        args=(dy, x, gain, ga, gb, wg, wu, wd), carried=carried)


def _opspec(arr, kind, tm, grid_mj):
    if kind == "full":
        return pl.BlockSpec((tm, arr.shape[1]), lambda *g: (grid_mj(*g)[0], 0)), arr.shape[1]
    if kind == "cols":
        n = arr.shape[1] // N_DEV
        return pl.BlockSpec((tm, n), lambda *g: grid_mj(*g)), n
    n = arr.shape[2]
    return pl.BlockSpec((None, tm, n), lambda *g: (grid_mj(*g)[1], grid_mj(*g)[0], 0)), n


def _wgrad(a, a_kind, b, b_kind, name, carried=None):
    S = a.shape[0] if a_kind != "stack" else a.shape[1]
    tm = _tile(S, 1024, 16)
    mj = lambda j, m: (m, j)
    a_spec, ka = _opspec(a, a_kind, tm, mj)
    b_spec, nb = _opspec(b, b_kind, tm, mj)
    n_m = S // tm

    def body(a_ref, b_ref, o_ref, acc_ref):
        m = pl.program_id(1)

        @pl.when(m == 0)
        def _():
            acc_ref[...] = jnp.zeros_like(acc_ref)

        acc_ref[...] += _dot(a_ref[...].astype(BF), b_ref[...].astype(BF), TN)

        @pl.when(m == n_m - 1)
        def _():
            o_ref[...] = acc_ref[...].astype(BF)

    outs, cout = _pcall(
        body, name=name, grid=(N_DEV, n_m), in_specs=[a_spec, b_spec],
        out_specs=[pl.BlockSpec((None, ka, nb), lambda j, m: (j, 0, 0))],
        out_shape=[jax.ShapeDtypeStruct((N_DEV, ka, nb), BF)],
        scratch=[pltpu.VMEM((ka, nb), F32)], sem=("parallel", "arbitrary"), args=(a, b), carried=carried)
    return outs[0], cout


def _normproj(x, gain, w, name, carried=None):
    S, D = x.shape
    _, n, _ = w.shape
    tm = _tile(S, 512, 16)

    def body(x_ref, g_ref, w_ref, hn_ref, y_ref):
        @pl.when(pl.program_id(1) == 0)
        def _():
            xhat, _ = _rms_hat(x_ref[...])
            hn_ref[...] = (xhat * g_ref[...]).astype(BF)

        y_ref[...] = _dot(hn_ref[...], w_ref[...], NT)

    row = pl.BlockSpec((tm, D), lambda m, j: (m, 0))
    return _pcall(
        body, name=name, grid=(S // tm, N_DEV),
        in_specs=[row, pl.BlockSpec((1, D), lambda m, j: (0, 0)), pl.BlockSpec((None, n, D), lambda m, j: (j, 0, 0))],
        out_specs=[row, pl.BlockSpec((tm, n), lambda m, j: (m, j))],
        out_shape=[jax.ShapeDtypeStruct((S, D), BF), jax.ShapeDtypeStruct((S, N_DEV * n), F32)],
        sem=("parallel", "arbitrary"), args=(x, gain, w), carried=carried)


def _rowproj(x, t, w, name, carried=None):
    S, D = x.shape
    _, k, _ = w.shape
    tm = _tile(S, 512, 16)

    def body(x_ref, t_ref, w_ref, y_ref):
        j = pl.program_id(1)
        part = _dot(t_ref[...], w_ref[...])

        @pl.when(j == 0)
        def _():
            y_ref[...] = x_ref[...] + part

        @pl.when(j > 0)
        def _():
            y_ref[...] += part

    row = pl.BlockSpec((tm, D), lambda m, j: (m, 0))
    outs, cout = _pcall(
        body, name=name, grid=(S // tm, N_DEV),
        in_specs=[row, pl.BlockSpec((tm, k), lambda m, j: (m, j)), pl.BlockSpec((None, k, D), lambda m, j: (j, 0, 0))],
        out_specs=[row], out_shape=[jax.ShapeDtypeStruct((S, D), F32)],
        sem=("parallel", "arbitrary"), args=(x, t, w), carried=carried)
    return outs[0], cout


def _dgrad_row(dy, w, out_dtype, name):
    S, D = dy.shape
    _, k, _ = w.shape
    tm = _tile(S, 512, 16)

    def body(dy_ref, w_ref, dt_ref, dyb_ref):
        @pl.when(pl.program_id(1) == 0)
        def _():
            dyb_ref[...] = dy_ref[...].astype(BF)

        dt_ref[...] = _dot(dyb_ref[...], w_ref[...], NT).astype(out_dtype)

    outs, _ = _pcall(
        body, name=name, grid=(S // tm, N_DEV),
        in_specs=[pl.BlockSpec((tm, D), lambda m, j: (m, 0)), pl.BlockSpec((None, k, D), lambda m, j: (j, 0, 0))],
        out_specs=[pl.BlockSpec((tm, k), lambda m, j: (m, j))],
        out_shape=[jax.ShapeDtypeStruct((S, N_DEV * k), out_dtype)],
        scratch=[pltpu.VMEM((tm, D), BF)], sem=("parallel", "arbitrary"), args=(dy, w))
    return outs[0]


def _dgrad_col_norm(dres, x, gain, dz, w, name, carried=None):
    S, D = x.shape
    _, n, _ = w.shape
    tm = _tile(S, 512, 16)

    def body(dres_ref, x_ref, g_ref, dz_ref, w_ref, dx_ref, dg_ref, acc_ref):
        m, j = pl.program_id(0), pl.program_id(1)

        @pl.when(j == 0)
        def _():
            acc_ref[...] = jnp.zeros_like(acc_ref)

        acc_ref[...] += _dot(dz_ref[...], w_ref[...])

        @pl.when(j == N_DEV - 1)
        def _():
            dx, dgain = _rms_bwd(acc_ref[...], x_ref[...], g_ref[...], dres_ref[...])
            dx_ref[...] = dx
            _accum_rows(dg_ref, dgain, m == 0)

    row = pl.BlockSpec((tm, D), lambda m, j: (m, 0), pipeline_mode=pl.Buffered(1))
    return _pcall(
        body, name=name, grid=(S // tm, N_DEV),
        in_specs=[row, row, pl.BlockSpec((1, D), lambda m, j: (0, 0)), pl.BlockSpec((tm, n), lambda m, j: (m, j)),
                  pl.BlockSpec((None, n, D), lambda m, j: (j, 0, 0))],
        out_specs=[row, pl.BlockSpec((8, D), lambda m, j: (0, 0))],
        out_shape=[jax.ShapeDtypeStruct((S, D), F32), jax.ShapeDtypeStruct((8, D), F32)],
        scratch=[pltpu.VMEM((tm, D), F32)], sem=("arbitrary", "arbitrary"),
        args=(dres, x, gain, dz, w), carried=carried)


def _headnorm_fwd(xa, width, gain_row, scale, name, carried=None):
    S = xa.shape[0]
    cb = _tile(width, 1024, LANE)
    tm = _tile(S, 512, 8)

    def body(x_ref, g_ref, y_ref):
        for c in range(cb // LANE):
            sl = slice(c * LANE, (c + 1) * LANE)
            xhat, _ = _rms_hat(x_ref[:, sl])
            y_ref[:, sl] = xhat * (g_ref[:, sl] * scale)

    outs, couts = _pcall(
        body, name=name, grid=(S // tm, width // cb),
        in_specs=[pl.BlockSpec((tm, cb), lambda m, c: (m, c)), pl.BlockSpec((1, cb), lambda m, c: (0, c))],
        out_specs=[pl.BlockSpec((tm, cb), lambda m, c: (m, c))],
        out_shape=[jax.ShapeDtypeStruct((S, width), F32)], sem=("parallel", "parallel"), args=(xa, gain_row),
        carried=carried)
    return outs[0], couts


def _headnorm_bwd(dy, xa, gain_row, scale, name, tail=None):
    S, width = dy.shape
    cb = _tile(width, 1024, LANE)
    tm = _tile(S, 512, 16)
    ncb = width // cb
    ntail = 0 if tail is None else tail.shape[1] // cb

    def body(dy_ref, x_ref, g_ref, *rest):
        dx_ref, dg_ref = rest[-2:]
        c, m = pl.program_id(0), pl.program_id(1)

        @pl.when(c < ncb)
        def _():
            rows = []
            for i in range(cb // LANE):
                sl = slice(i * LANE, (i + 1) * LANE)
                dx, dgain = _rms_bwd(dy_ref[:, sl] * scale, x_ref[:, sl], g_ref[:, sl], 0.0)
                dx_ref[:, sl] = dx.astype(BF)
                rows.append(dgain)
            _accum_rows(dg_ref, jnp.concatenate(rows, axis=1), m == 0)

        if tail is not None:
            @pl.when(c >= ncb)
            def _():
                dx_ref[...] = rest[0][...].astype(BF)

    head = lambda c: jnp.minimum(c, ncb - 1)
    in_specs = [pl.BlockSpec((tm, cb), lambda c, m: (jnp.where(c < ncb, m, 0), head(c))),
                pl.BlockSpec((tm, cb), lambda c, m: (jnp.where(c < ncb, m, 0), head(c))),
                pl.BlockSpec((1, cb), lambda c, m: (0, head(c)))]
    args = [dy, xa, gain_row]
    if tail is not None:
        in_specs.append(pl.BlockSpec((tm, cb), lambda c, m: (jnp.where(c >= ncb, m, 0), jnp.maximum(c - ncb, 0))))
        args.append(tail)
    outs, _ = _pcall(
        body, name=name, grid=(ncb + ntail, S // tm), in_specs=in_specs,
        out_specs=[pl.BlockSpec((tm, cb), lambda c, m: (m, c)), pl.BlockSpec((8, cb), lambda c, m: (0, head(c)))],
        out_shape=[jax.ShapeDtypeStruct((S, width + ntail * cb), BF), jax.ShapeDtypeStruct((8, width), F32)],
        sem=("arbitrary", "arbitrary"), args=args)
    return outs


def _causal():
    p = lax.broadcasted_iota(jnp.int32, (LANE, LANE), 0)
    q = lax.broadcasted_iota(jnp.int32, (LANE, LANE), 1)
    return p >= q


def _gmlp_fwd(z, v_gain, ws, bias, carried=None):
    S, dg2 = z.shape
    dg = dg2 // 2
    G = dg // LANE

    def body(z_ref, vg_ref, ws_ref, bias_ref, t_ref):
        u, _ = _gelu(z_ref[:, :dg])
        v, _ = _gelu(z_ref[:, dg:])
        vhat, _ = _rms_hat(v)
        vn = (vhat * vg_ref[...]).astype(BF)
        mask = _causal()
        for g in range(G):
            sl = slice(g * LANE, (g + 1) * LANE)
            wm = jnp.where(mask, ws_ref[g], 0.0).astype(BF)
            sv = _dot(wm, vn[:, sl]) + bias_ref[:, sl]
            t_ref[:, sl] = (u[:, sl] * sv).astype(BF)

    outs, couts = _pcall(
        body, name="gmlp_fwd", grid=(S // LANE,),
        in_specs=[pl.BlockSpec((LANE, dg2), lambda n: (n, 0)), pl.BlockSpec((1, dg), lambda n: (0, 0)),
                  pl.BlockSpec((G, LANE, LANE), lambda n: (0, 0, 0)), pl.BlockSpec((LANE, dg), lambda n: (0, 0))],
        out_specs=[pl.BlockSpec((LANE, dg), lambda n: (n, 0))],
        out_shape=[jax.ShapeDtypeStruct((S, dg), BF)], sem=("parallel",), args=(z, v_gain, ws, bias),
        carried=carried)
    return outs[0], couts


def _gmlp_bwd(z, dt, v_gain, ws, bias):
    S, dg2 = z.shape
    dg = dg2 // 2
    G = dg // LANE

    def body(z_ref, dt_ref, vg_ref, ws_ref, bias_ref, dz_ref, dws_ref, db_ref, dvg_ref, dvn_ref):
        n = pl.program_id(0)
        zu, zv = z_ref[:, :dg], z_ref[:, dg:]
        u, tu = _gelu(zu)
        v, tv = _gelu(zv)
        vhat, r = _rms_hat(v)
        vn = (vhat * vg_ref[...]).astype(BF)
        mask = _causal()

        @pl.when(n == 0)
        def _():
            dws_ref[...] = jnp.zeros_like(dws_ref)
            db_ref[...] = jnp.zeros_like(db_ref)

        for g in range(G):
            sl = slice(g * LANE, (g + 1) * LANE)
            wm = jnp.where(mask, ws_ref[g], 0.0).astype(BF)
            sv = _dot(wm, vn[:, sl]) + bias_ref[:, sl]
            dtg = dt_ref[:, sl].astype(F32)
            dz_ref[:, sl] = (dtg * sv * _gelu_grad(zu[:, sl], tu[:, sl])).astype(BF)
            dsv = dtg * u[:, sl]
            dsvb = dsv.astype(BF)
            dvn_ref[:, sl] = _dot(wm, dsvb, TN)
            dws_ref[g] += jnp.where(mask, _dot(dsvb, vn[:, sl], NT), 0.0)
            db_ref[:, sl] += jnp.broadcast_to(jnp.sum(dsv, axis=1, keepdims=True), (LANE, LANE))

        dvn = dvn_ref[...]
        dxhat = dvn * vg_ref[...]
        dv = r * (dxhat - vhat * jnp.mean(dxhat * vhat, axis=-1, keepdims=True))
        dz_ref[:, dg:] = (dv * _gelu_grad(zv, tv)).astype(BF)
        _accum_rows(dvg_ref, jnp.sum(dvn * vhat, axis=0, keepdims=True), n == 0)

    outs, _ = _pcall(
        body, name="gmlp_bwd", grid=(S // LANE,),
        in_specs=[pl.BlockSpec((LANE, dg2), lambda n: (n, 0)), pl.BlockSpec((LANE, dg), lambda n: (n, 0)),
                  pl.BlockSpec((1, dg), lambda n: (0, 0)), pl.BlockSpec((G, LANE, LANE), lambda n: (0, 0, 0)),
                  pl.BlockSpec((LANE, dg), lambda n: (0, 0))],
        out_specs=[pl.BlockSpec((LANE, dg2), lambda n: (n, 0)), pl.BlockSpec((G, LANE, LANE), lambda n: (0, 0, 0)),
                   pl.BlockSpec((LANE, dg), lambda n: (0, 0)), pl.BlockSpec((8, dg), lambda n: (0, 0))],
        out_shape=[jax.ShapeDtypeStruct((S, dg2), BF), jax.ShapeDtypeStruct((G, LANE, LANE), F32),
                   jax.ShapeDtypeStruct((LANE, dg), F32), jax.ShapeDtypeStruct((8, dg), F32)],
        scratch=[pltpu.VMEM((LANE, dg), F32)], sem=("arbitrary",), args=(z, dt, v_gain, ws, bias))
    return outs


N_ENT = ATT_WIN // LANE


def _rows(ref, start, dil):
    return ref[pl.ds(start, LANE), :] if dil == 1 else ref[pl.ds(start, LANE, stride=dil), :]


def _rows_store(ref, start, dil, val):
    if dil == 1:
        ref[pl.ds(start, LANE), :] = val
    else:
        ref[pl.ds(start, LANE, stride=dil), :] = val


def _slope_times_dil(h, n_heads, dil, shape):
    hv = jnp.zeros(shape, F32) + (h + 1).astype(F32)
    return jnp.exp(hv * (-8.0 / n_heads * math.log(2.0))) * float(dil)


def _band_bias(h, n_heads, dil, has_prev):
    qi = lax.broadcasted_iota(jnp.int32, (LANE, 2 * LANE), 0)
    kj = lax.broadcasted_iota(jnp.int32, (LANE, 2 * LANE), 1)
    delta = qi + LANE - kj
    valid = (delta >= 0) & (delta <= LANE) & ((kj >= LANE) | has_prev)
    return jnp.where(valid, -_slope_times_dil(h, n_heads, dil, (LANE, 2 * LANE)) * delta.astype(F32), NEG)


def _attn_geom(S, H, g, dil):
    pb = LANE * dil
    nblk = ATT_WIN // pb
    nw = S // ATT_WIN
    win = lambda c0: pl.BlockSpec((ATT_WIN, LANE), lambda h, w: (w, c0 + h))
    prev = lambda c0: pl.BlockSpec((pb, LANE), lambda h, w: (jnp.maximum(w * nblk - 1, 0), c0 + h))
    nxt = lambda c0: pl.BlockSpec((pb, LANE), lambda h, w: (jnp.minimum((w + 1) * nblk, nw * nblk - 1), c0 + h))
    return pb, nblk, nw, g * H, win, prev, nxt


def _stage_band(dst, cur_ref, prev_ref, dil, pb, nblk):
    for blk in range(nblk):
        for r in range(dil):
            e = blk * dil + r
            dst[e, :LANE] = _rows(prev_ref, r, dil) if blk == 0 else _rows(cur_ref, (blk - 1) * pb + r, dil)
            dst[e, LANE:] = _rows(cur_ref, blk * pb + r, dil)


def _stage(dst, ref, dil, pb, nblk, lead=None):
    for blk in range(nblk):
        for r in range(dil):
            val = _rows(ref, blk * pb + r, dil)
            if lead is None:
                dst[blk * dil + r] = val
            else:
                dst[lead, blk * dil + r] = val


def _unstage(ref, src, dil, pb, nblk):
    for blk in range(nblk):
        for r in range(dil):
            _rows_store(ref, blk * pb + r, dil, src[blk * dil + r])


def _attn_fwd(q, k, kv, g, dil, carried=None):
    S = q.shape[0]
    H = q.shape[1] // (3 * LANE)
    pb, nblk, nw, col, win, prev, _ = _attn_geom(S, H, g, dil)
    vcol = 3 * H + col

    def body(q_ref, kc_ref, kp_ref, vc_ref, vp_ref, o_ref, l_ref, qs, ks, vs, os_, ls):
        h, w = pl.program_id(0), pl.program_id(1)
        _stage(qs, q_ref, dil, pb, nblk)
        _stage_band(ks, kc_ref, kp_ref, dil, pb, nblk)
        _stage_band(vs, vc_ref, vp_ref, dil, pb, nblk)

        def run(lo, hi, bias):
            def step(e, carry):
                s = _dot(qs[e].astype(BF), ks[e].astype(BF), NT) + bias
                mx = jnp.max(s, axis=-1, keepdims=True)
                p = jnp.exp(s - mx)
                l = jnp.sum(p, axis=-1, keepdims=True)
                os_[e] = _dot((p / l).astype(BF), vs[e].astype(BF))
                ls[e] = jnp.broadcast_to(mx + jnp.log(l), (LANE, LANE))
                return carry
            if hi > lo:
                lax.fori_loop(lo, hi, step, 0, unroll=True)

        run(0, dil, _band_bias(h, H, dil, w > 0))
        run(dil, N_ENT, _band_bias(h, H, dil, True))
        _unstage(o_ref, os_, dil, pb, nblk)
        _unstage(l_ref, ls, dil, pb, nblk)

    out = jax.ShapeDtypeStruct((S, H * LANE), F32)
    sq = pltpu.VMEM((N_ENT, LANE, LANE), F32)
    sk = pltpu.VMEM((N_ENT, 2 * LANE, LANE), F32)
    return _pcall(
        body, name="attn_fwd_d%d" % dil, grid=(H, nw),
        in_specs=[win(col), win(col), prev(col), win(vcol), prev(vcol)],
        out_specs=[win(0), win(0)], out_shape=[out, out], scratch=[sq, sk, sk, sq, sq],
        sem=("parallel", "parallel"), args=(q, k, k, kv, kv), carried=carried)


def _attn_combine(os_, ls_, carried=None):
    S, C = os_[0].shape
    tm = _tile(S, 256, 16)

    def body(o0, o1, o2, l0, l1, l2, o_ref, lse_ref):
        a, b, c = l0[...], l1[...], l2[...]
        mx = jnp.maximum(jnp.maximum(a, b), c)
        ea, eb, ec = jnp.exp(a - mx), jnp.exp(b - mx), jnp.exp(c - mx)
        den = ea + eb + ec
        o_ref[...] = ((ea * o0[...] + eb * o1[...] + ec * o2[...]) / den).astype(BF)
        lse_ref[...] = mx + jnp.log(den)

    blk = pl.BlockSpec((tm, C), lambda m: (m, 0))
    return _pcall(
        body, name="attn_combine", grid=(S // tm,), in_specs=[blk] * 6, out_specs=[blk, blk],
        out_shape=[jax.ShapeDtypeStruct((S, C), BF), jax.ShapeDtypeStruct((S, C), F32)],
        sem=("parallel",), args=(*os_, *ls_), carried=carried)


def _attn_delta(do, o):
    S, C = do.shape
    tm = _tile(S, 512, 16)

    def body(do_ref, o_ref, d_ref):
        for c in range(C // LANE):
            sl = slice(c * LANE, (c + 1) * LANE)
            prod = do_ref[:, sl].astype(BF).astype(F32) * o_ref[:, sl].astype(F32)
            d_ref[:, sl] = jnp.broadcast_to(jnp.sum(prod, axis=-1, keepdims=True), (tm, LANE))

    blk = pl.BlockSpec((tm, C), lambda m: (m, 0))
    outs, _ = _pcall(
        body, name="attn_delta", grid=(S // tm,), in_specs=[blk, blk], out_specs=[blk],
        out_shape=[jax.ShapeDtypeStruct((S, C), F32)], sem=("parallel",), args=(do, o))
    return outs[0]


def _attn_bwd_dq(q, k, kv, do, lse, dl, g, dil, dq_prev):
    S = q.shape[0]
    H = q.shape[1] // (3 * LANE)
    pb, nblk, nw, col, win, prev, _ = _attn_geom(S, H, g, dil)
    vcol = 3 * H + col

    def body(q_ref, kc_ref, kp_ref, vc_ref, vp_ref, do_ref, l_ref, d_ref, *rest):
        dq_ref, qs, ks, vs, dos, ls, ds_, dqs = rest[-8:]
        h, w = pl.program_id(0), pl.program_id(1)
        _stage(qs, q_ref, dil, pb, nblk)
        _stage_band(ks, kc_ref, kp_ref, dil, pb, nblk)
        _stage_band(vs, vc_ref, vp_ref, dil, pb, nblk)
        _stage(dos, do_ref, dil, pb, nblk)
        _stage(ls, l_ref, dil, pb, nblk)
        _stage(ds_, d_ref, dil, pb, nblk)

        def run(lo, hi, bias):
            def step(e, carry):
                kb = ks[e].astype(BF)
                s = _dot(qs[e].astype(BF), kb, NT) + bias
                p = jnp.exp(s - ls[e][:, :1])
                dp = _dot(dos[e].astype(BF), vs[e].astype(BF), NT)
                dsc = p * (dp - ds_[e][:, :1])
                dqs[e] = _dot(dsc.astype(BF), kb)
                return carry
            if hi > lo:
                lax.fori_loop(lo, hi, step, 0, unroll=True)

        run(0, dil, _band_bias(h, H, dil, w > 0))
        run(dil, N_ENT, _band_bias(h, H, dil, True))
        _unstage(dq_ref, dqs, dil, pb, nblk)

    sq = pltpu.VMEM((N_ENT, LANE, LANE), F32)
    sk = pltpu.VMEM((N_ENT, 2 * LANE, LANE), F32)
    in_specs = [win(col), win(col), prev(col), win(vcol), prev(vcol), win(0), win(0), win(0)]
    args = [q, k, k, kv, kv, do, lse, dl]
    aliases = {}
    if dq_prev is not None:
        in_specs.append(ANY)
        args.append(dq_prev)
        aliases = {8: 0}
    outs, _ = _pcall(
        body, name="attn_bwd_dq_d%d" % dil, grid=(H, nw), in_specs=in_specs,
        out_specs=[win(col)], out_shape=[jax.ShapeDtypeStruct((S, 3 * H * LANE), F32)],
        scratch=[sq, sk, sk, sq, sq, sq, sq], sem=("parallel", "parallel"), args=args, aliases=aliases)
    return outs[0]


def _attn_bwd_dkv(q, k, kv, do, lse, dl, g, dil, prev_out, accumulate):
    S = q.shape[0]
    H = q.shape[1] // (3 * LANE)
    pb, nblk, nw, col, win, _, nxt = _attn_geom(S, H, g, dil)
    vcol = 3 * H + col
    n_q = N_ENT + dil

    def body(k_ref, v_ref, qc_ref, qn_ref, doc_ref, don_ref, lc_ref, ln_ref, dc_ref, dn_ref, *rest):
        dk_ref, dv_ref, ks, vs, qs, dos, ls, ds_, dks, dvs = rest[-10:]
        h, w = pl.program_id(0), pl.program_id(1)
        _stage(ks, k_ref, dil, pb, nblk)
        _stage(vs, v_ref, dil, pb, nblk)
        for dst, cur, nx in ((qs, qc_ref, qn_ref), (dos, doc_ref, don_ref), (ls, lc_ref, ln_ref), (ds_, dc_ref, dn_ref)):
            _stage(dst, cur, dil, pb, nblk)
            for r in range(dil):
                dst[N_ENT + r] = _rows(nx, r, dil)
        qi = lax.broadcasted_iota(jnp.int32, (LANE, LANE), 0)
        kj = lax.broadcasted_iota(jnp.int32, (LANE, LANE), 1)
        sd = _slope_times_dil(h, H, dil, (LANE, LANE))
        bias_c = jnp.where(qi >= kj, -sd * (qi - kj).astype(F32), NEG)

        def run(lo, hi, has_next):
            bias_n = jnp.where((qi <= kj) & has_next, -sd * (qi + LANE - kj).astype(F32), NEG)

            def step(e, carry):
                kb = ks[e].astype(BF)
                vb = vs[e].astype(BF)
                dk = jnp.zeros((LANE, LANE), F32)
                dv = jnp.zeros((LANE, LANE), F32)
                for eq, bias in ((e, bias_c), (e + dil, bias_n)):
                    qb = qs[eq].astype(BF)
                    dob = dos[eq].astype(BF)
                    s = _dot(qb, kb, NT) + bias
                    p = jnp.exp(s - ls[eq][:, :1])
                    dp = _dot(dob, vb, NT)
                    dsc = p * (dp - ds_[eq][:, :1])
                    dv = dv + _dot(p.astype(BF), dob, TN)
                    dk = dk + _dot(dsc.astype(BF), qb, TN)
                dks[e] = dk
                dvs[e] = dv
                return carry
            if hi > lo:
                lax.fori_loop(lo, hi, step, 0, unroll=True)

        run(0, N_ENT - dil, True)
        run(N_ENT - dil, N_ENT, w < nw - 1)
        if accumulate:
            pk_ref, pv_ref = rest[0], rest[1]
            dk_ref[...] = pk_ref[...]
            dv_ref[...] = pv_ref[...]
            for blk in range(nblk):
                for r in range(dil):
                    e, start = blk * dil + r, blk * pb + r
                    _rows_store(dk_ref, start, dil, _rows(dk_ref, start, dil) + dks[e])
                    _rows_store(dv_ref, start, dil, _rows(dv_ref, start, dil) + dvs[e])
        else:
            _unstage(dk_ref, dks, dil, pb, nblk)
            _unstage(dv_ref, dvs, dil, pb, nblk)

    s1 = pltpu.VMEM((N_ENT, LANE, LANE), F32)
    s2 = pltpu.VMEM((n_q, LANE, LANE), F32)
    out = jax.ShapeDtypeStruct((S, 3 * H * LANE), F32)
    in_specs = [win(col), win(vcol), win(col), nxt(col), win(0), nxt(0), win(0), nxt(0), win(0), nxt(0)]
    args = [k, kv, q, q, do, do, lse, lse, dl, dl]
    aliases = {}
    if prev_out is not None:
        in_specs += [win(col), win(col)] if accumulate else [ANY, ANY]
        args += list(prev_out)
        aliases = {10: 0, 11: 1}
    outs, _ = _pcall(
        body, name="attn_bwd_dkv_d%d%s" % (dil, "_acc" if accumulate else ""), grid=(H, nw), in_specs=in_specs,
        out_specs=[win(col), win(col)], out_shape=[out, out],
        scratch=[s1, s1, s2, s2, s2, s2, s1, s1], sem=("parallel", "parallel"), vmem=56, args=args, aliases=aliases)
    return outs


def _loss_head(y, target):
    S, D = y.shape
    tm = _tile(S, 512, 8)

    def body(y_ref, t_ref, dy_ref, l_ref):
        e = y_ref[...] - t_ref[...]
        dy_ref[...] = e * (1.0 / D)
        part = jnp.broadcast_to(jnp.sum(jnp.sum(e * e, axis=1, keepdims=True), axis=0, keepdims=True) * (0.5 / D), (8, LANE))
        _accum_rows(l_ref, part, pl.program_id(0) == 0)

    blk = pl.BlockSpec((tm, D), lambda m: (m, 0))
    outs, _ = _pcall(
        body, name="loss_head", grid=(S // tm,), in_specs=[blk, blk],
        out_specs=[blk, pl.BlockSpec((8, LANE), lambda m: (0, 0))],
        out_shape=[jax.ShapeDtypeStruct((S, D), F32), jax.ShapeDtypeStruct((8, LANE), F32)],
        sem=("arbitrary",), args=(y, target))
    return outs


class _Queue:
    def __init__(self):
        self.items = []

    def push(self, kind, key, arr):
        self.items.append((kind, key, arr))

    def take(self, budget_us, kinds=None):
        taken, left, spent = [], [], 0.0
        for item in self.items:
            if spent >= budget_us or (kinds is not None and item[0] not in kinds):
                left.append(item)
                continue
            taken.append(item)
            spent += COST_US_PER_ELEM[item[0]] * (item[2].size / item[2].shape[0] if item[0] != "gather" else item[2].size)
        self.items = left
        return taken

    def take_keys(self, keys):
        taken = [it for it in self.items if it[1] in keys]
        self.items = [it for it in self.items if it[1] not in keys]
        return taken

    def take_kind(self, kind):
        taken = [it for it in self.items if it[0] == kind]
        self.items = [it for it in self.items if it[0] != kind]
        return taken


def kernel(x, ffn1_norm, ffn1_w_gate, ffn1_w_up, ffn1_w_down, mix_norm, ffn2_norm, ffn2_w_gate, ffn2_w_up, ffn2_w_down, gmlp_w_in, gmlp_v_norm, gmlp_w_s, gmlp_b_s, gmlp_w_out, kv_norm, w_kv, k_norm, attn_w_q, attn_q_norm, attn_w_o, loss_target, m_ffn1_norm, m_ffn1_w_gate, m_ffn1_w_up, m_ffn1_w_down, m_mix_norm, m_ffn2_norm, m_ffn2_w_gate, m_ffn2_w_up, m_ffn2_w_down, m_gmlp_w_in, m_gmlp_v_norm, m_gmlp_w_s, m_gmlp_b_s, m_gmlp_w_out, m_kv_norm, m_w_kv, m_k_norm, m_attn_w_q, m_attn_q_norm, m_attn_w_o, v_ffn1_norm, v_ffn1_w_gate, v_ffn1_w_up, v_ffn1_w_down, v_mix_norm, v_ffn2_norm, v_ffn2_w_gate, v_ffn2_w_up, v_ffn2_w_down, v_gmlp_w_in, v_gmlp_v_norm, v_gmlp_w_s, v_gmlp_b_s, v_gmlp_w_out, v_kv_norm, v_w_kv, v_k_norm, v_attn_w_q, v_attn_q_norm, v_attn_w_o):
    names = ["ffn1_norm", "ffn1_w_gate", "ffn1_w_up", "ffn1_w_down", "mix_norm", "ffn2_norm", "ffn2_w_gate",
             "ffn2_w_up", "ffn2_w_down", "gmlp_w_in", "gmlp_v_norm", "gmlp_w_s", "gmlp_b_s", "gmlp_w_out",
             "kv_norm", "w_kv", "k_norm", "attn_w_q", "attn_q_norm", "attn_w_o"]
    W = dict(zip(names, [ffn1_norm, ffn1_w_gate, ffn1_w_up, ffn1_w_down, mix_norm, ffn2_norm, ffn2_w_gate,
                         ffn2_w_up, ffn2_w_down, gmlp_w_in, gmlp_v_norm, gmlp_w_s, gmlp_b_s, gmlp_w_out,
                         kv_norm, w_kv, k_norm, attn_w_q, attn_q_norm, attn_w_o]))
    M = dict(zip(names, [m_ffn1_norm, m_ffn1_w_gate, m_ffn1_w_up, m_ffn1_w_down, m_mix_norm, m_ffn2_norm, m_ffn2_w_gate,
                         m_ffn2_w_up, m_ffn2_w_down, m_gmlp_w_in, m_gmlp_v_norm, m_gmlp_w_s, m_gmlp_b_s, m_gmlp_w_out,
                         m_kv_norm, m_w_kv, m_k_norm, m_attn_w_q, m_attn_q_norm, m_attn_w_o]))
    V = dict(zip(names, [v_ffn1_norm, v_ffn1_w_gate, v_ffn1_w_up, v_ffn1_w_down, v_mix_norm, v_ffn2_norm, v_ffn2_w_gate,
                         v_ffn2_w_up, v_ffn2_w_down, v_gmlp_w_in, v_gmlp_v_norm, v_gmlp_w_s, v_gmlp_b_s, v_gmlp_w_out,
                         v_kv_norm, v_w_kv, v_k_norm, v_attn_w_q, v_attn_q_norm, v_attn_w_o]))

    depth = ffn1_norm.shape[0]
    n_a = gmlp_w_in.shape[0]
    S, D = x.shape[1], x.shape[2]
    H = D // LANE
    n_grp = len(DILATIONS)
    hw = H * LANE
    xi, yi, ci = _me()
    core = jnp.reshape(ci, (1,)).astype(jnp.int32)
    chip = jnp.reshape(2 * xi + yi, (1,)).astype(jnp.int32)
    dev = 4 * xi + 2 * yi + ci
    q_scale = LANE ** -0.5
    ffn_names = (("f1", ("ffn1_norm", "ffn1_w_gate", "ffn1_w_up", "ffn1_w_down")),
                 ("f2", ("ffn2_norm", "ffn2_w_gate", "ffn2_w_up", "ffn2_w_down")))
    transposed = ("ffn1_w_gate", "ffn1_w_up", "ffn2_w_gate", "ffn2_w_up")

    def gain(v):
        return v.reshape(1, -1)

    def head_gain(g3):
        return jnp.tile(g3[:, None, :], (1, H, 1)).reshape(1, n_grp * hw)

    def bf_t(w):
        return jnp.swapaxes(w, 0, 1).astype(BF)

    gq = _Queue()
    gathered = {}
    gq.push("gather", "v_norm", jnp.pad(gmlp_v_norm, ((0, 8 - n_a), (0, 0))))
    for l in range(depth):
        for tag, (_, wgn, wun, wdn) in ffn_names:
            if tag == "f2":
                if l < n_a:
                    gq.push("gather", ("gmlp_w_in", l), bf_t(gmlp_w_in[l]))
                    gq.push("gather", ("gmlp_w_out", l), gmlp_w_out[l].astype(BF))
                else:
                    gq.push("gather", ("attn_w_q", l - n_a), bf_t(attn_w_q[l - n_a]))
                    gq.push("gather", ("attn_w_o", l - n_a), attn_w_o[l - n_a].astype(BF))
            gq.push("gather", (wgn, l), bf_t(W[wgn][l]))
            gq.push("gather", (wun, l), bf_t(W[wun][l]))
            gq.push("gather", (wdn, l), W[wdn][l].astype(BF))
        if l == n_a - 1:
            gq.push("gather", "w_kv", bf_t(w_kv))

    def land(items, outs):
        for (_, key, _), o in zip(items, outs):
            gathered[key] = o

    def need(*keys):
        items = gq.take_keys([k for k in keys if k not in gathered])
        if items:
            land(items, _comm_only(_Carried([(k, a) for k, _, a in items]), "allgather"))
        return [gathered[k] for k in keys]

    def carry(q, kind):
        items = q.take(BUDGET_US[kind])
        return items, _Carried([(k, a) for k, _, a in items])

    v_all = need("v_norm", *[(n, 0) for n in ffn_names[0][1][1:]])[0]
    v_gain_all = jnp.transpose(v_all[:, :n_a], (1, 0, 2)).reshape(n_a, -1)

    cur = x.reshape(S, D)
    saved = []
    k_sh = kv_raw = kv_hn = kv_x = k_gain = None

    for l in range(depth):
        rec = {}
        for tag, (nn, wgn, wun, wdn) in ffn_names:
            if tag == "f2":
                rec["mix_x"] = cur
                if l < n_a:
                    w_in, w_out = need(("gmlp_w_in", l), ("gmlp_w_out", l))
                    bias = jnp.repeat(gmlp_b_s[l].T, LANE, axis=1)
                    items, car = carry(gq, "normproj")
                    (hm, z), couts = _normproj(cur, gain(mix_norm[l]), w_in, "gmlp_in", car)
                    land(items, couts)
                    items, car = carry(gq, "gmlp_fwd")
                    t, couts = _gmlp_fwd(z, gain(v_gain_all[l]), gmlp_w_s[l], bias, car)
                    land(items, couts)
                    items, car = carry(gq, "rowproj")
                    cur, couts = _rowproj(cur, t, w_out, "proj_out", car)
                    land(items, couts)
                    rec.update(w_in=w_in, w_out=w_out, bias=bias, hm=hm, z=z, t=t)
                else:
                    jj = l - n_a
                    w_q, w_o = need(("attn_w_q", jj), ("attn_w_o", jj))
                    items, car = carry(gq, "normproj")
                    (hm, q_raw), couts = _normproj(cur, gain(mix_norm[l]), w_q, "attn_q", car)
                    land(items, couts)
                    qg = head_gain(attn_q_norm[jj])
                    items, car = carry(gq, "headnorm")
                    q, couts = _headnorm_fwd(q_raw, n_grp * hw, qg, q_scale, "headnorm_q", car)
                    land(items, couts)
                    outs = []
                    for g, dil in enumerate(DILATIONS):
                        items, car = carry(gq, "attn_fwd")
                        og, couts = _attn_fwd(q, k_sh, kv_raw, g, dil, car)
                        land(items, couts)
                        outs.append(og)
                    items, car = carry(gq, "attn_combine")
                    (o, lse), couts = _attn_combine([o_ for o_, _ in outs], [l_ for _, l_ in outs], car)
                    land(items, couts)
                    items, car = carry(gq, "rowproj")
                    cur, couts = _rowproj(cur, o, w_o, "proj_out", car)
                    land(items, couts)
                    rec.update(w_q=w_q, w_o=w_o, hm=hm, q_raw=q_raw, qg=qg, q=q, o=o, lse=lse)
            wg, wu, wd = need((wgn, l), (wun, l), (wdn, l))
            rec[tag + "_x"] = cur
            items, car = carry(gq, "ffn_fwd")
            (cur, hn, act, ga, gb), couts = _ffn_fwd(cur, gain(W[nn][l]), wg, wu, wd, car)
            land(items, couts)
            rec[tag] = (wg, wu, wd, hn, act, ga, gb)
        if l == n_a - 1:
            (w_kv_g,) = need("w_kv")
            kv_x = cur
            items, car = carry(gq, "normproj")
            (kv_hn, kv_raw), couts = _normproj(cur, gain(kv_norm), w_kv_g, "kv_proj", car)
            land(items, couts)
            k_gain = head_gain(k_norm)
            items, car = carry(gq, "headnorm")
            k_sh, couts = _headnorm_fwd(kv_raw, n_grp * hw, k_gain, 1.0, "headnorm_k", car)
            land(items, couts)
        saved.append(rec)

    dcur, loss_part = _loss_head(cur, loss_target.reshape(S, D))
    loss = lax.psum(loss_part[0, 0], ("x", "y", "c"))

    rq = _Queue()
    reduced = {}
    small = {}

    def rs_land(items, outs):
        for (kind, key, arr), o in zip(items, outs):
            if kind == "core":
                rq.push("chip", key, _pair_add(arr, o, core))
            else:
                reduced[key] = (arr, o)

    def rs_carry(kind, kinds=None):
        items = rq.take(BUDGET_US[kind], kinds)
        return items, _Carried([(k, a) for k, _, a in items])

    def wgrad(key, a, a_kind, b, b_kind, name):
        last_block = key[1] == 0 and key[0].startswith("ffn1")
        items, car = rs_carry("wgrad_last" if last_block else "wgrad", None if last_block else ("core",))
        p, couts = _wgrad(a, a_kind, b, b_kind, name, car)
        rs_land(items, couts)
        rq.push("core", key, p)

    def put(name, l, val, n_layers):
        small.setdefault(name, [None] * n_layers)[l] = val

    dk_dv = None
    for l in reversed(range(depth)):
        rec = saved[l]
        if l == n_a - 1:
            dkv, dkg = _headnorm_bwd(dk_dv[0], kv_raw, k_gain, 1.0, "headnorm_k_bwd", tail=dk_dv[1])
            wgrad(("w_kv", 0), kv_hn, "full", dkv, "cols", "wgrad_kv")
            items, car = rs_carry("dgrad_col_norm")
            (dcur, dg), couts = _dgrad_col_norm(dcur, kv_x, gain(kv_norm), dkv, gathered["w_kv"], "dgrad_kv", car)
            rs_land(items, couts)
            small["kv_norm"] = [dg[0]]
            small["k_norm"] = [dkg[0].reshape(n_grp, H, LANE).sum(axis=1)]
        for tag, (nn, wgn, wun, wdn) in reversed(ffn_names):
            wg, wu, wd, hn, act, ga, gb = rec[tag]
            items, car = rs_carry("ffn_bwd_dx")
            (dcur, da, db, dyh, dg), couts = _ffn_bwd_dx(dcur, rec[tag + "_x"], gain(W[nn][l]), ga, gb, wg, wu, wd, car)
            rs_land(items, couts)
            wgrad((wgn, l), da, "stack", hn, "full", "wgrad_ffn_in")
            wgrad((wun, l), db, "stack", hn, "full", "wgrad_ffn_in")
            wgrad((wdn, l), act, "stack", dyh, "full", "wgrad_ffn_out")
            put(nn, l, dg[0], depth)
            if tag == "f2":
                mix_x = rec["mix_x"]
                if l < n_a:
                    dt = _dgrad_row(dcur, rec["w_out"], BF, "dgrad_gmlp_out")
                    wgrad(("gmlp_w_out", l), rec["t"], "cols", dcur, "full", "wgrad_proj_out")
                    dz, dws, dbias, dvg = _gmlp_bwd(rec["z"], dt, gain(v_gain_all[l]), gmlp_w_s[l], rec["bias"])
                    wgrad(("gmlp_w_in", l), rec["hm"], "full", dz, "cols", "wgrad_gmlp_in")
                    items, car = rs_carry("dgrad_col_norm")
                    (dcur, dg), couts = _dgrad_col_norm(dcur, mix_x, gain(mix_norm[l]), dz, rec["w_in"], "dgrad_gmlp_in", car)
                    rs_land(items, couts)
                    put("gmlp_w_s", l, dws, n_a)
                    put("gmlp_b_s", l, dbias[:, ::LANE].T, n_a)
                    put("gmlp_v_norm", l, dvg[0], n_a)
                else:
                    jj = l - n_a
                    do = _dgrad_row(dcur, rec["w_o"], F32, "dgrad_attn_out")
                    wgrad(("attn_w_o", jj), rec["o"], "cols", dcur, "full", "wgrad_proj_out")
                    dl = _attn_delta(do, rec["o"])
                    dq = None
                    first_layer = dk_dv is None
                    for g, dil in enumerate(DILATIONS):
                        dq = _attn_bwd_dq(rec["q"], k_sh, kv_raw, do, rec["lse"], dl, g, dil, dq)
                        dk_dv = _attn_bwd_dkv(rec["q"], k_sh, kv_raw, do, rec["lse"], dl, g, dil, dk_dv,
                                              accumulate=not first_layer)
                    dq_raw, dqg = _headnorm_bwd(dq, rec["q_raw"], rec["qg"], q_scale, "headnorm_q_bwd")
                    wgrad(("attn_w_q", jj), rec["hm"], "full", dq_raw, "cols", "wgrad_attn_q")
                    items, car = rs_carry("dgrad_col_norm")
                    (dcur, dg), couts = _dgrad_col_norm(dcur, mix_x, gain(mix_norm[l]), dq_raw, rec["w_q"], "dgrad_attn_q", car)
                    rs_land(items, couts)
                    put("attn_q_norm", jj, dqg[0].reshape(n_grp, H, LANE).sum(axis=1), depth - n_a)
                put("mix_norm", l, dg[0], depth)
    grad_x = dcur.reshape(1, S, D)

    for kind in ("core", "chip"):
        items = rq.take_kind(kind)
        if items:
            rs_land(items, _comm_only(_Carried([(k, a) for k, _, a in items]), "rs_%s_exchange" % kind))

    out_g, out_d, out_m, out_v = {}, {}, {}, {}
    for name in names:
        if (name, 0) not in reduced:
            continue
        if name in transposed:
            as3 = lambda t: jnp.swapaxes(t, 1, 2)
        else:
            as3 = (lambda t: t[None]) if W[name].ndim == 2 else (lambda t: t)
        res = None
        for l in range(as3(W[name]).shape[0]):
            qsum, r2 = reduced[(name, l)]
            res = _adamw_shard(qsum, r2, chip, as3(W[name]), as3(M[name]), as3(V[name]), l, res)
        for dct, val in zip((out_g, out_d, out_m, out_v), res):
            dct[name] = jnp.swapaxes(val, 1, 2) if name in transposed else (val[0] if W[name].ndim == 2 else val)

    small_names = [n for n in names if n in small]
    full_shape = {n: (W[n].shape if n != "gmlp_v_norm" else (n_a, v_gain_all.shape[1])) for n in small_names}
    flat = jnp.concatenate([jnp.stack(small[n]).reshape(-1) if W[n].ndim > 1 else small[n][0].reshape(-1)
                            for n in small_names])
    n_flat = flat.shape[0]
    rows = -(-n_flat // (8 * LANE)) * 8

    def pack(parts_list):
        v = jnp.concatenate([p.reshape(-1) for p in parts_list])
        return jnp.pad(v, (0, rows * LANE - n_flat)).reshape(rows, LANE)

    def full_of(dct, n, fill):
        if n != "gmlp_v_norm":
            return dct[n]
        sh = dct[n].shape[1]
        return lax.dynamic_update_slice(jnp.full(full_shape[n], fill, F32), dct[n], (0, dev * sh))

    (g_all,) = _comm_only(_Carried([("gather", pack([flat]))]), "allgather_small_grads")
    w_p = pack([full_of(W, n, 0.0) for n in small_names])
    m_p = pack([full_of(M, n, 0.0) for n in small_names])
    v_p = pack([full_of(V, n, 1.0) for n in small_names])
    packed = _adamw_replicated(g_all, w_p, m_p, v_p)
    offs = 0
    for n in small_names:
        size = math.prod(full_shape[n])
        for dct, arr in zip((out_g, out_d, out_m, out_v), packed):
            val = arr.reshape(-1)[offs:offs + size].reshape(full_shape[n])
            if n == "gmlp_v_norm":
                sh = W[n].shape[1]
                val = lax.dynamic_slice(val, (0, dev * sh), (n_a, sh))
            dct[n] = val
        offs += size

    return (loss, grad_x, *[out_g[n] for n in names], *[out_d[n] for n in names],
            *[out_m[n] for n in names], *[out_v[n] for n in names])
```

```python
import math

import jax
import jax.numpy as jnp
from jax import lax
from jax.experimental import pallas as pl
from jax.experimental.pallas import tpu as pltpu

F32 = jnp.float32
BF = jnp.bfloat16
N_DEV = 8
N_CHIP = 4
LANE = 128
ATT_WIN = 16 * LANE
EPS = 1e-6
NEG = -1e30
DILATIONS = (1, 4, 16)
ADAM_LR, ADAM_B1, ADAM_B2, ADAM_EPS, ADAM_WD, ADAM_STEP = 0.001, 0.9, 0.999, 1e-08, 0.01, 10
GELU_C0, GELU_C1 = 0.7978845608028654, 0.044715
VMEM_MB = 2 ** 20
BUDGET_US = {"ffn_fwd": 380.0, "normproj": 150.0, "rowproj": 100.0, "gmlp_fwd": 50.0, "headnorm": 75.0,
             "attn_fwd": 120.0, "attn_combine": 75.0,
             "ffn_bwd_dx": 500.0, "wgrad": 60.0, "wgrad_last": 110.0, "dgrad_col_norm": 110.0}
COST_US_PER_ELEM = {"gather": 0.8e-4, "core": 1.4e-5, "chip": 9.0e-5}

MESH_T = pl.DeviceIdType.MESH
ANY = pl.BlockSpec(memory_space=pl.ANY)
DMA_SEM = pltpu.SemaphoreType.DMA
NT = (((1,), (1,)), ((), ()))
TN = (((0,), (0,)), ((), ()))


def _tile(n, target, mult):
    best = None
    for t in range(mult, min(n, target) + 1, mult):
        if n % t == 0:
            best = t
    if best is None:
        best = n
    return best


def _dot(a, b, dims=None):
    if dims is None:
        return jnp.dot(a, b, preferred_element_type=F32)
    return lax.dot_general(a, b, dims, preferred_element_type=F32)


def _rms_hat(xv):
    r = lax.rsqrt(jnp.mean(xv * xv, axis=-1, keepdims=True) + EPS)
    return xv * r, r


def _rms_bwd(dhn, xv, gain, dres):
    xhat, r = _rms_hat(xv)
    dxhat = dhn * gain
    dx = dres + r * (dxhat - xhat * jnp.mean(dxhat * xhat, axis=-1, keepdims=True))
    return dx, jnp.sum(dhn * xhat, axis=0, keepdims=True)


def _accum_rows(ref, row, first):
    val = jnp.broadcast_to(row, ref.shape)

    @pl.when(first)
    def _():
        ref[...] = val

    @pl.when(jnp.logical_not(first))
    def _():
        ref[...] += val


def _gelu(z):
    t = jnp.tanh(GELU_C0 * (z + GELU_C1 * z * z * z))
    return 0.5 * z * (1.0 + t), t


def _gelu_grad(z, t):
    return 0.5 * (1.0 + t) + 0.5 * z * (1.0 - t * t) * GELU_C0 * (1.0 + 3.0 * GELU_C1 * z * z)


def _me():
    return lax.axis_index("x"), lax.axis_index("y"), lax.axis_index("c")


def _gather_phase(phase, x_ref, out_ref, send_sems, recv_sems, local_sem):
    x, y, c = _me()
    me, sibling = (x, y, c), (x, y, 1 - c)
    x_nbr, y_nbr, diag = (1 - x, y), (x, 1 - y), (1 - x, 1 - y)
    src_chip = (x + (1 - 2 * x) * (1 - c), y + (1 - 2 * y) * c)
    dst_chip = (x + (1 - 2 * x) * c, y + (1 - 2 * y) * (1 - c))

    def slot(px, py, pc):
        return out_ref.at[4 * px + 2 * py + pc]

    def copy(k, block, to, src=None):
        return pltpu.make_async_remote_copy(
            src_ref=slot(*block) if src is None else src, dst_ref=slot(*block),
            send_sem=send_sems.at[k], recv_sem=recv_sems.at[k], device_id=to, device_id_type=MESH_T)

    mine = pltpu.make_async_copy(x_ref, slot(*me), local_sem)
    first = [copy(0, me, sibling, src=x_ref), copy(1, me, (*x_nbr, c), src=x_ref), copy(2, me, (*y_nbr, c), src=x_ref)]
    second = [copy(3, (*x_nbr, c), sibling), copy(4, (*y_nbr, c), sibling), copy(5, (*src_chip, c), (*dst_chip, c))]
    third = [copy(6, (*diag, c), sibling)]
    if phase == 0:
        mine.start()
        for cp in first:
            cp.start()
    elif phase == 1:
        copy(1, (*x_nbr, c), me).wait_recv()
        copy(2, (*y_nbr, c), me).wait_recv()
        for cp in second:
            cp.start()
    elif phase == 2:
        copy(5, (*diag, c), me).wait_recv()
        third[0].start()
    else:
        copy(0, sibling, me).wait_recv()
        copy(3, (*x_nbr, 1 - c), me).wait_recv()
        copy(4, (*y_nbr, 1 - c), me).wait_recv()
        copy(6, (*diag, 1 - c), me).wait_recv()
        for cp in first + second + third:
            cp.wait_send()
        mine.wait()


def _exchange_phase(phase, kind, src_ref, dst_ref, send_sems, recv_sems):
    x, y, c = _me()
    if kind == "core":
        plan = [(2 * k + (1 - c), k, (x, y, 1 - c)) for k in range(N_CHIP)]
    else:
        plan = [(2 * px + py, t, (px, py, c)) for t, (px, py) in enumerate([(1 - x, y), (x, 1 - y), (1 - x, 1 - y)])]
    cps = [pltpu.make_async_remote_copy(
        src_ref=src_ref.at[s], dst_ref=dst_ref.at[d], send_sem=send_sems.at[i], recv_sem=recv_sems.at[i],
        device_id=to, device_id_type=MESH_T) for i, (s, d, to) in enumerate(plan)]
    if phase == 0:
        for cp in cps:
            cp.start()
    elif phase == N_PHASES - 1:
        for cp in cps:
            cp.wait()


_N_COPIES = {"gather": 7, "core": N_CHIP, "chip": 3}
N_PHASES = 4


class _Carried:
    def __init__(self, items=()):
        self.items = list(items)

    def arrays(self):
        return [a for _, a in self.items]

    def out_shapes(self):
        lead = {"gather": lambda a: (N_DEV,) + a.shape, "core": lambda a: (N_CHIP,) + a.shape[1:],
                "chip": lambda a: (3,) + a.shape[1:]}
        return [jax.ShapeDtypeStruct(lead[k](a), a.dtype) for k, a in self.items]

    def scratch(self):
        res = []
        for k, _ in self.items:
            res += [DMA_SEM((_N_COPIES[k],)), DMA_SEM((_N_COPIES[k],))]
            if k == "gather":
                res.append(DMA_SEM(()))
        return res

    def emit(self, phase, in_refs, out_refs, scr):
        i = 0
        for (kind, _), src, dst in zip(self.items, in_refs, out_refs):
            if kind == "gather":
                _gather_phase(phase, src, dst, scr[i], scr[i + 1], scr[i + 2])
                i += 3
            else:
                _exchange_phase(phase, kind, src, dst, scr[i], scr[i + 1])
                i += 2


def _comm_only(carried, name):
    nc = len(carried.items)

    def body(*refs):
        for phase in range(N_PHASES):
            carried.emit(phase, refs[:nc], refs[nc:2 * nc], refs[2 * nc:])

    return pl.pallas_call(
        body, name=name, out_shape=carried.out_shapes(), in_specs=[ANY] * nc, out_specs=[ANY] * nc,
        scratch_shapes=carried.scratch(),
    )(*carried.arrays())


def _pcall(main, *, name, grid, in_specs, out_specs, out_shape, args, scratch=(), sem=None, vmem=48,
           carried=None, aliases=None):
    params = pltpu.CompilerParams(dimension_semantics=sem, vmem_limit_bytes=vmem * VMEM_MB)
    n_in, n_out, n_scr = len(in_specs), len(out_specs), len(scratch)
    if carried is None or not carried.items:
        outs = pl.pallas_call(
            main, name=name, grid=grid, in_specs=in_specs, out_specs=out_specs, out_shape=out_shape,
            scratch_shapes=list(scratch), compiler_params=params, input_output_aliases=aliases or {},
        )(*args)
        return list(outs), []
    nc = len(carried.items)
    total = math.prod(grid)

    def body(*refs):
        ins, cin = refs[:n_in], refs[n_in:n_in + nc]
        o0 = n_in + nc
        outs, cout = refs[o0:o0 + n_out], refs[o0 + n_out:o0 + n_out + nc]
        s0 = o0 + n_out + nc
        scr, cscr = refs[s0:s0 + n_scr], refs[s0 + n_scr:]
        step = 0
        for d, n in enumerate(grid):
            step = step * n + pl.program_id(d)

        @pl.when(step == 0)
        def _():
            carried.emit(0, cin, cout, cscr)

        main(*ins, *outs, *scr)

        for phase, at in ((1, (5 * total) // 8), (2, (7 * total) // 8), (3, total - 1)):
            @pl.when(step == min(at, total - 1))
            def _(phase=phase):
                carried.emit(phase, cin, cout, cscr)

    outs = pl.pallas_call(
        body, name=name, grid=grid, in_specs=list(in_specs) + [ANY] * nc, out_specs=list(out_specs) + [ANY] * nc,
        out_shape=list(out_shape) + carried.out_shapes(), scratch_shapes=list(scratch) + carried.scratch(),
        compiler_params=params, input_output_aliases=aliases or {},
    )(*args, *carried.arrays())
    return list(outs[:n_out]), list(outs[n_out:])


def _pair_add(p, r1, core):
    _, rows, cols = p.shape
    tr = _tile(rows, 512, 16)
    p4 = p.reshape(N_CHIP, 2, rows, cols)

    def body(core_ref, p_ref, r_ref, q_ref):
        q_ref[...] = (p_ref[...].astype(F32) + r_ref[...].astype(F32)).astype(BF)

    grid_spec = pltpu.PrefetchScalarGridSpec(
        num_scalar_prefetch=1, grid=(N_CHIP, rows // tr),
        in_specs=[pl.BlockSpec((None, None, tr, cols), lambda k, i, cr: (k, cr[0], i, 0)),
                  pl.BlockSpec((None, tr, cols), lambda k, i, cr: (k, i, 0))],
        out_specs=pl.BlockSpec((None, tr, cols), lambda k, i, cr: (k, i, 0)))
    return pl.pallas_call(
        body, name="pair_add", grid_spec=grid_spec, out_shape=jax.ShapeDtypeStruct((N_CHIP, rows, cols), BF),
        compiler_params=pltpu.CompilerParams(dimension_semantics=("parallel", "parallel")),
    )(core, p4, r1)


def _adam_math(g, w, m, v):
    m2 = ADAM_B1 * m + (1.0 - ADAM_B1) * g
    v2 = ADAM_B2 * v + (1.0 - ADAM_B2) * (g * g)
    m_hat = m2 / (1.0 - ADAM_B1 ** ADAM_STEP)
    v_hat = v2 / (1.0 - ADAM_B2 ** ADAM_STEP)
    delta = -ADAM_LR * (m_hat / (jnp.sqrt(v_hat) + ADAM_EPS) + ADAM_WD * w)
    return delta, m2, v2


def _adamw_shard(q, r2, chip, w, m, v, layer, prev):
    n_layers, rows, cols = w.shape
    tr = _tile(rows, 256, 16)

    def body(chip_ref, q_ref, r_ref, w_ref, m_ref, v_ref, *rest):
        g_ref, d_ref, m2_ref, v2_ref = rest[-4:]
        g = q_ref[...].astype(F32) + r_ref[0].astype(F32) + r_ref[1].astype(F32) + r_ref[2].astype(F32)
        d, m2, v2 = _adam_math(g, w_ref[...], m_ref[...], v_ref[...])
        g_ref[...] = g
        d_ref[...] = d
        m2_ref[...] = m2
        v2_ref[...] = v2

    blk = pl.BlockSpec((None, tr, cols), lambda i, cr: (layer, i, 0))
    in_specs = [pl.BlockSpec((None, tr, cols), lambda i, cr: (cr[0], i, 0)),
                pl.BlockSpec((3, tr, cols), lambda i, cr: (0, i, 0)), blk, blk, blk]
    args = [chip, q, r2, w, m, v]
    aliases = {}
    if prev is not None:
        in_specs += [ANY] * 4
        args += list(prev)
        aliases = {6 + i: i for i in range(4)}
    grid_spec = pltpu.PrefetchScalarGridSpec(
        num_scalar_prefetch=1, grid=(rows // tr,), in_specs=in_specs, out_specs=[blk, blk, blk, blk])
    out = jax.ShapeDtypeStruct((n_layers, rows, cols), F32)
    return pl.pallas_call(
        body, name="adamw_shard", grid_spec=grid_spec, out_shape=[out, out, out, out], input_output_aliases=aliases,
        compiler_params=pltpu.CompilerParams(dimension_semantics=("parallel",)),
    )(*args)


def _adamw_replicated(parts, w, m, v):
    rows, cols = w.shape
    tr = _tile(rows, 512, 8)

    def body(p_ref, w_ref, m_ref, v_ref, g_ref, d_ref, m2_ref, v2_ref):
        g = p_ref[0]
        for k in range(1, N_DEV):
            g = g + p_ref[k]
        d, m2, v2 = _adam_math(g, w_ref[...], m_ref[...], v_ref[...])
        g_ref[...] = g
        d_ref[...] = d
        m2_ref[...] = m2
        v2_ref[...] = v2

    blk = pl.BlockSpec((tr, cols), lambda i: (i, 0))
    out = jax.ShapeDtypeStruct((rows, cols), F32)
    outs, _ = _pcall(body, name="adamw_replicated", grid=(rows // tr,),
                     in_specs=[pl.BlockSpec((N_DEV, tr, cols), lambda i: (0, i, 0)), blk, blk, blk],
                     out_specs=[blk, blk, blk, blk], out_shape=[out, out, out, out], sem=("parallel",),
                     args=(parts, w, m, v))
    return outs


def _ffn_fwd(x, gain, wg, wu, wd, carried=None):
    S, D = x.shape
    nsh, fs, _ = wg.shape
    tm = _tile(S, 512, 16)

    def body(x_ref, g_ref, wg_ref, wu_ref, wd_ref, y_ref, hn_ref, act_ref, ga_ref, gb_ref, acc_ref):
        j = pl.program_id(1)

        @pl.when(j == 0)
        def _():
            xhat, _ = _rms_hat(x_ref[...])
            hn_ref[...] = (xhat * g_ref[...]).astype(BF)
            acc_ref[...] = jnp.zeros_like(acc_ref)

        hn = hn_ref[...]
        a = _dot(hn, wg_ref[...], NT)
        b = _dot(hn, wu_ref[...], NT)
        sg = jax.nn.sigmoid(a)
        sil = a * sg
        act = (sil * b).astype(BF)
        act_ref[...] = act
        ga_ref[...] = (b * (sg * (1.0 + a * (1.0 - sg)))).astype(BF)
        gb_ref[...] = sil.astype(BF)
        acc_ref[...] += _dot(act, wd_ref[...])

        @pl.when(j == nsh - 1)
        def _():
            y_ref[...] = x_ref[...] + 0.5 * acc_ref[...]

    row = pl.BlockSpec((tm, D), lambda m, j: (m, 0))
    hid = pl.BlockSpec((None, tm, fs), lambda m, j: (j, m, 0))
    return _pcall(
        body, name="ffn_fwd", grid=(S // tm, nsh),
        in_specs=[row, pl.BlockSpec((1, D), lambda m, j: (0, 0)),
                  pl.BlockSpec((None, fs, D), lambda m, j: (j, 0, 0)),
                  pl.BlockSpec((None, fs, D), lambda m, j: (j, 0, 0)),
                  pl.BlockSpec((None, fs, D), lambda m, j: (j, 0, 0))],
        out_specs=[row, row, hid, hid, hid],
        out_shape=[jax.ShapeDtypeStruct((S, D), F32), jax.ShapeDtypeStruct((S, D), BF)]
        + [jax.ShapeDtypeStruct((nsh, S, fs), BF)] * 3,
        scratch=[pltpu.VMEM((tm, D), F32)], sem=("parallel", "arbitrary"), vmem=58,
        args=(x, gain, wg, wu, wd), carried=carried)


def _ffn_bwd_dx(dy, x, gain, ga, gb, wg, wu, wd, carried=None):
    S, D = x.shape
    nsh, fs, _ = wg.shape
    tm = _tile(S, 512, 16)

    def body(dy_ref, x_ref, g_ref, ga_ref, gb_ref, wg_ref, wu_ref, wd_ref,
             dx_ref, da_ref, db_ref, dyh_ref, dg_ref, acc_ref):
        m, j = pl.program_id(0), pl.program_id(1)

        @pl.when(j == 0)
        def _():
            dyh_ref[...] = (0.5 * dy_ref[...]).astype(BF)
            acc_ref[...] = jnp.zeros_like(acc_ref)

        half = -(-(fs // 2) // LANE) * LANE
        for c0, c1 in ((0, half), (half, fs)) if 0 < half < fs else ((0, fs),):
            dact = _dot(dyh_ref[...], wd_ref[c0:c1, :], NT)
            da = (dact * ga_ref[:, c0:c1].astype(F32)).astype(BF)
            db = (dact * gb_ref[:, c0:c1].astype(F32)).astype(BF)
            da_ref[:, c0:c1] = da
            db_ref[:, c0:c1] = db
            acc_ref[...] += _dot(da, wg_ref[c0:c1, :])
            acc_ref[...] += _dot(db, wu_ref[c0:c1, :])

        @pl.when(j == nsh - 1)
        def _():
            dx, dgain = _rms_bwd(acc_ref[...], x_ref[...], g_ref[...], dy_ref[...])
            dx_ref[...] = dx
            _accum_rows(dg_ref, dgain, m == 0)

    row = pl.BlockSpec((tm, D), lambda m, j: (m, 0))
    row1 = pl.BlockSpec((tm, D), lambda m, j: (m, 0), pipeline_mode=pl.Buffered(1))
    hid = pl.BlockSpec((None, tm, fs), lambda m, j: (j, m, 0))
    hshape = jax.ShapeDtypeStruct((nsh, S, fs), BF)
    return _pcall(
        body, name="ffn_bwd_dx", grid=(S // tm, nsh),
        in_specs=[row1, row1, pl.BlockSpec((1, D), lambda m, j: (0, 0)), hid, hid,
                  pl.BlockSpec((None, fs, D), lambda m, j: (j, 0, 0)),
                  pl.BlockSpec((None, fs, D), lambda m, j: (j, 0, 0)),
                  pl.BlockSpec((None, fs, D), lambda m, j: (j, 0, 0))],
        out_specs=[row1, hid, hid, row1, pl.BlockSpec((8, D), lambda m, j: (0, 0))],
        out_shape=[jax.ShapeDtypeStruct((S, D), F32), hshape, hshape,
                   jax.ShapeDtypeStruct((S, D), BF), jax.ShapeDtypeStruct((8, D), F32)],
        scratch=[pltpu.VMEM((tm, D), F32)], sem=("arbitrary", "arbitrary"), vmem=60,
        args=(dy, x, gain, ga, gb, wg, wu, wd), carried=carried)


def _opspec(arr, kind, tm, grid_mj):
    if kind == "full":
        return pl.BlockSpec((tm, arr.shape[1]), lambda *g: (grid_mj(*g)[0], 0)), arr.shape[1]
    if kind == "cols":
        n = arr.shape[1] // N_DEV
        return pl.BlockSpec((tm, n), lambda *g: grid_mj(*g)), n
    n = arr.shape[2]
    return pl.BlockSpec((None, tm, n), lambda *g: (grid_mj(*g)[1], grid_mj(*g)[0], 0)), n


def _wgrad(a, a_kind, b, b_kind, name, carried=None):
    S = a.shape[0] if a_kind != "stack" else a.shape[1]
    tm = _tile(S, 1024, 16)
    mj = lambda j, m: (m, j)
    a_spec, ka = _opspec(a, a_kind, tm, mj)
    b_spec, nb = _opspec(b, b_kind, tm, mj)
    n_m = S // tm

    def body(a_ref, b_ref, o_ref, acc_ref):
        m = pl.program_id(1)

        @pl.when(m == 0)
        def _():
            acc_ref[...] = jnp.zeros_like(acc_ref)

        acc_ref[...] += _dot(a_ref[...].astype(BF), b_ref[...].astype(BF), TN)

        @pl.when(m == n_m - 1)
        def _():
            o_ref[...] = acc_ref[...].astype(BF)

    outs, cout = _pcall(
        body, name=name, grid=(N_DEV, n_m), in_specs=[a_spec, b_spec],
        out_specs=[pl.BlockSpec((None, ka, nb), lambda j, m: (j, 0, 0))],
        out_shape=[jax.ShapeDtypeStruct((N_DEV, ka, nb), BF)],
        scratch=[pltpu.VMEM((ka, nb), F32)], sem=("parallel", "arbitrary"), args=(a, b), carried=carried)
    return outs[0], cout


def _normproj(x, gain, w, name, carried=None):
    S, D = x.shape
    _, n, _ = w.shape
    tm = _tile(S, 512, 16)

    def body(x_ref, g_ref, w_ref, hn_ref, y_ref):
        @pl.when(pl.program_id(1) == 0)
        def _():
            xhat, _ = _rms_hat(x_ref[...])
            hn_ref[...] = (xhat * g_ref[...]).astype(BF)

        y_ref[...] = _dot(hn_ref[...], w_ref[...], NT)

    row = pl.BlockSpec((tm, D), lambda m, j: (m, 0))
    return _pcall(
        body, name=name, grid=(S // tm, N_DEV),
        in_specs=[row, pl.BlockSpec((1, D), lambda m, j: (0, 0)), pl.BlockSpec((None, n, D), lambda m, j: (j, 0, 0))],
        out_specs=[row, pl.BlockSpec((tm, n), lambda m, j: (m, j))],
        out_shape=[jax.ShapeDtypeStruct((S, D), BF), jax.ShapeDtypeStruct((S, N_DEV * n), F32)],
        sem=("parallel", "arbitrary"), args=(x, gain, w), carried=carried)


def _rowproj(x, t, w, name, carried=None):
    S, D = x.shape
    _, k, _ = w.shape
    tm = _tile(S, 512, 16)

    def body(x_ref, t_ref, w_ref, y_ref):
        j = pl.program_id(1)
        part = _dot(t_ref[...], w_ref[...])

        @pl.when(j == 0)
        def _():
            y_ref[...] = x_ref[...] + part

        @pl.when(j > 0)
        def _():
            y_ref[...] += part

    row = pl.BlockSpec((tm, D), lambda m, j: (m, 0))
    outs, cout = _pcall(
        body, name=name, grid=(S // tm, N_DEV),
        in_specs=[row, pl.BlockSpec((tm, k), lambda m, j: (m, j)), pl.BlockSpec((None, k, D), lambda m, j: (j, 0, 0))],
        out_specs=[row], out_shape=[jax.ShapeDtypeStruct((S, D), F32)],
        sem=("parallel", "arbitrary"), args=(x, t, w), carried=carried)
    return outs[0], cout


def _dgrad_row(dy, w, out_dtype, name):
    S, D = dy.shape
    _, k, _ = w.shape
    tm = _tile(S, 512, 16)

    def body(dy_ref, w_ref, dt_ref, dyb_ref):
        @pl.when(pl.program_id(1) == 0)
        def _():
            dyb_ref[...] = dy_ref[...].astype(BF)

        dt_ref[...] = _dot(dyb_ref[...], w_ref[...], NT).astype(out_dtype)

    outs, _ = _pcall(
        body, name=name, grid=(S // tm, N_DEV),
        in_specs=[pl.BlockSpec((tm, D), lambda m, j: (m, 0)), pl.BlockSpec((None, k, D), lambda m, j: (j, 0, 0))],
        out_specs=[pl.BlockSpec((tm, k), lambda m, j: (m, j))],
        out_shape=[jax.ShapeDtypeStruct((S, N_DEV * k), out_dtype)],
        scratch=[pltpu.VMEM((tm, D), BF)], sem=("parallel", "arbitrary"), args=(dy, w))
    return outs[0]


def _dgrad_col_norm(dres, x, gain, dz, w, name, carried=None):
    S, D = x.shape
    _, n, _ = w.shape
    tm = _tile(S, 512, 16)

    def body(dres_ref, x_ref, g_ref, dz_ref, w_ref, dx_ref, dg_ref, acc_ref):
        m, j = pl.program_id(0), pl.program_id(1)

        @pl.when(j == 0)
        def _():
            acc_ref[...] = jnp.zeros_like(acc_ref)

        acc_ref[...] += _dot(dz_ref[...], w_ref[...])

        @pl.when(j == N_DEV - 1)
        def _():
            dx, dgain = _rms_bwd(acc_ref[...], x_ref[...], g_ref[...], dres_ref[...])
            dx_ref[...] = dx
            _accum_rows(dg_ref, dgain, m == 0)

    row = pl.BlockSpec((tm, D), lambda m, j: (m, 0), pipeline_mode=pl.Buffered(1))
    return _pcall(
        body, name=name, grid=(S // tm, N_DEV),
        in_specs=[row, row, pl.BlockSpec((1, D), lambda m, j: (0, 0)), pl.BlockSpec((tm, n), lambda m, j: (m, j)),
                  pl.BlockSpec((None, n, D), lambda m, j: (j, 0, 0))],
        out_specs=[row, pl.BlockSpec((8, D), lambda m, j: (0, 0))],
        out_shape=[jax.ShapeDtypeStruct((S, D), F32), jax.ShapeDtypeStruct((8, D), F32)],
        scratch=[pltpu.VMEM((tm, D), F32)], sem=("arbitrary", "arbitrary"),
        args=(dres, x, gain, dz, w), carried=carried)


def _headnorm_fwd(xa, width, gain_row, scale, name, carried=None):
    S = xa.shape[0]
    cb = _tile(width, 1024, LANE)
    tm = _tile(S, 512, 8)

    def body(x_ref, g_ref, y_ref):
        for c in range(cb // LANE):
            sl = slice(c * LANE, (c + 1) * LANE)
            xhat, _ = _rms_hat(x_ref[:, sl])
            y_ref[:, sl] = xhat * (g_ref[:, sl] * scale)

    outs, couts = _pcall(
        body, name=name, grid=(S // tm, width // cb),
        in_specs=[pl.BlockSpec((tm, cb), lambda m, c: (m, c)), pl.BlockSpec((1, cb), lambda m, c: (0, c))],
        out_specs=[pl.BlockSpec((tm, cb), lambda m, c: (m, c))],
        out_shape=[jax.ShapeDtypeStruct((S, width), F32)], sem=("parallel", "parallel"), args=(xa, gain_row),
        carried=carried)
    return outs[0], couts


def _headnorm_bwd(dy, xa, gain_row, scale, name, tail=None):
    S, width = dy.shape
    cb = _tile(width, 1024, LANE)
    tm = _tile(S, 512, 16)
    ncb = width // cb
    ntail = 0 if tail is None else tail.shape[1] // cb

    def body(dy_ref, x_ref, g_ref, *rest):
        dx_ref, dg_ref = rest[-2:]
        c, m = pl.program_id(0), pl.program_id(1)

        @pl.when(c < ncb)
        def _():
            rows = []
            for i in range(cb // LANE):
                sl = slice(i * LANE, (i + 1) * LANE)
                dx, dgain = _rms_bwd(dy_ref[:, sl] * scale, x_ref[:, sl], g_ref[:, sl], 0.0)
                dx_ref[:, sl] = dx.astype(BF)
                rows.append(dgain)
            _accum_rows(dg_ref, jnp.concatenate(rows, axis=1), m == 0)

        if tail is not None:
            @pl.when(c >= ncb)
            def _():
                dx_ref[...] = rest[0][...].astype(BF)

    head = lambda c: jnp.minimum(c, ncb - 1)
    in_specs = [pl.BlockSpec((tm, cb), lambda c, m: (jnp.where(c < ncb, m, 0), head(c))),
                pl.BlockSpec((tm, cb), lambda c, m: (jnp.where(c < ncb, m, 0), head(c))),
                pl.BlockSpec((1, cb), lambda c, m: (0, head(c)))]
    args = [dy, xa, gain_row]
    if tail is not None:
        in_specs.append(pl.BlockSpec((tm, cb), lambda c, m: (jnp.where(c >= ncb, m, 0), jnp.maximum(c - ncb, 0))))
        args.append(tail)
    outs, _ = _pcall(
        body, name=name, grid=(ncb + ntail, S // tm), in_specs=in_specs,
        out_specs=[pl.BlockSpec((tm, cb), lambda c, m: (m, c)), pl.BlockSpec((8, cb), lambda c, m: (0, head(c)))],
        out_shape=[jax.ShapeDtypeStruct((S, width + ntail * cb), BF), jax.ShapeDtypeStruct((8, width), F32)],
        sem=("arbitrary", "arbitrary"), args=args)
    return outs


def _causal():
    p = lax.broadcasted_iota(jnp.int32, (LANE, LANE), 0)
    q = lax.broadcasted_iota(jnp.int32, (LANE, LANE), 1)
    return p >= q


def _gmlp_fwd(z, v_gain, ws, bias, carried=None):
    S, dg2 = z.shape
    dg = dg2 // 2
    G = dg // LANE

    def body(z_ref, vg_ref, ws_ref, bias_ref, t_ref):
        u, _ = _gelu(z_ref[:, :dg])
        v, _ = _gelu(z_ref[:, dg:])
        vhat, _ = _rms_hat(v)
        vn = (vhat * vg_ref[...]).astype(BF)
        mask = _causal()
        for g in range(G):
            sl = slice(g * LANE, (g + 1) * LANE)
            wm = jnp.where(mask, ws_ref[g], 0.0).astype(BF)
            sv = _dot(wm, vn[:, sl]) + bias_ref[:, sl]
            t_ref[:, sl] = (u[:, sl] * sv).astype(BF)

    outs, couts = _pcall(
        body, name="gmlp_fwd", grid=(S // LANE,),
        in_specs=[pl.BlockSpec((LANE, dg2), lambda n: (n, 0)), pl.BlockSpec((1, dg), lambda n: (0, 0)),
                  pl.BlockSpec((G, LANE, LANE), lambda n: (0, 0, 0)), pl.BlockSpec((LANE, dg), lambda n: (0, 0))],
        out_specs=[pl.BlockSpec((LANE, dg), lambda n: (n, 0))],
        out_shape=[jax.ShapeDtypeStruct((S, dg), BF)], sem=("parallel",), args=(z, v_gain, ws, bias),
        carried=carried)
    return outs[0], couts


def _gmlp_bwd(z, dt, v_gain, ws, bias):
    S, dg2 = z.shape
    dg = dg2 // 2
    G = dg // LANE

    def body(z_ref, dt_ref, vg_ref, ws_ref, bias_ref, dz_ref, dws_ref, db_ref, dvg_ref, dvn_ref):
        n = pl.program_id(0)
        zu, zv = z_ref[:, :dg], z_ref[:, dg:]
        u, tu = _gelu(zu)
        v, tv = _gelu(zv)
        vhat, r = _rms_hat(v)
        vn = (vhat * vg_ref[...]).astype(BF)
        mask = _causal()

        @pl.when(n == 0)
        def _():
            dws_ref[...] = jnp.zeros_like(dws_ref)
            db_ref[...] = jnp.zeros_like(db_ref)

        for g in range(G):
            sl = slice(g * LANE, (g + 1) * LANE)
            wm = jnp.where(mask, ws_ref[g], 0.0).astype(BF)
            sv = _dot(wm, vn[:, sl]) + bias_ref[:, sl]
            dtg = dt_ref[:, sl].astype(F32)
            dz_ref[:, sl] = (dtg * sv * _gelu_grad(zu[:, sl], tu[:, sl])).astype(BF)
            dsv = dtg * u[:, sl]
            dsvb = dsv.astype(BF)
            dvn_ref[:, sl] = _dot(wm, dsvb, TN)
            dws_ref[g] += jnp.where(mask, _dot(dsvb, vn[:, sl], NT), 0.0)
            db_ref[:, sl] += jnp.broadcast_to(jnp.sum(dsv, axis=1, keepdims=True), (LANE, LANE))

        dvn = dvn_ref[...]
        dxhat = dvn * vg_ref[...]
        dv = r * (dxhat - vhat * jnp.mean(dxhat * vhat, axis=-1, keepdims=True))
        dz_ref[:, dg:] = (dv * _gelu_grad(zv, tv)).astype(BF)
        _accum_rows(dvg_ref, jnp.sum(dvn * vhat, axis=0, keepdims=True), n == 0)

    outs, _ = _pcall(
        body, name="gmlp_bwd", grid=(S // LANE,),
        in_specs=[pl.BlockSpec((LANE, dg2), lambda n: (n, 0)), pl.BlockSpec((LANE, dg), lambda n: (n, 0)),
                  pl.BlockSpec((1, dg), lambda n: (0, 0)), pl.BlockSpec((G, LANE, LANE), lambda n: (0, 0, 0)),
                  pl.BlockSpec((LANE, dg), lambda n: (0, 0))],
        out_specs=[pl.BlockSpec((LANE, dg2), lambda n: (n, 0)), pl.BlockSpec((G, LANE, LANE), lambda n: (0, 0, 0)),
                   pl.BlockSpec((LANE, dg), lambda n: (0, 0)), pl.BlockSpec((8, dg), lambda n: (0, 0))],
        out_shape=[jax.ShapeDtypeStruct((S, dg2), BF), jax.ShapeDtypeStruct((G, LANE, LANE), F32),
                   jax.ShapeDtypeStruct((LANE, dg), F32), jax.ShapeDtypeStruct((8, dg), F32)],
        scratch=[pltpu.VMEM((LANE, dg), F32)], sem=("arbitrary",), args=(z, dt, v_gain, ws, bias))
    return outs


N_ENT = ATT_WIN // LANE


def _rows(ref, start, dil):
    return ref[pl.ds(start, LANE), :] if dil == 1 else ref[pl.ds(start, LANE, stride=dil), :]


def _rows_store(ref, start, dil, val):
    if dil == 1:
        ref[pl.ds(start, LANE), :] = val
    else:
        ref[pl.ds(start, LANE, stride=dil), :] = val


def _slope_times_dil(h, n_heads, dil, shape):
    hv = jnp.zeros(shape, F32) + (h + 1).astype(F32)
    return jnp.exp(hv * (-8.0 / n_heads * math.log(2.0))) * float(dil)


def _band_bias(h, n_heads, dil, has_prev):
    qi = lax.broadcasted_iota(jnp.int32, (LANE, 2 * LANE), 0)
    kj = lax.broadcasted_iota(jnp.int32, (LANE, 2 * LANE), 1)
    delta = qi + LANE - kj
    valid = (delta >= 0) & (delta <= LANE) & ((kj >= LANE) | has_prev)
    return jnp.where(valid, -_slope_times_dil(h, n_heads, dil, (LANE, 2 * LANE)) * delta.astype(F32), NEG)


def _attn_geom(S, H, g, dil):
    pb = LANE * dil
    nblk = ATT_WIN // pb
    nw = S // ATT_WIN
    win = lambda c0: pl.BlockSpec((ATT_WIN, LANE), lambda h, w: (w, c0 + h))
    prev = lambda c0: pl.BlockSpec((pb, LANE), lambda h, w: (jnp.maximum(w * nblk - 1, 0), c0 + h))
    nxt = lambda c0: pl.BlockSpec((pb, LANE), lambda h, w: (jnp.minimum((w + 1) * nblk, nw * nblk - 1), c0 + h))
    return pb, nblk, nw, g * H, win, prev, nxt


def _stage_band(dst, cur_ref, prev_ref, dil, pb, nblk):
    for blk in range(nblk):
        for r in range(dil):
            e = blk * dil + r
            dst[e, :LANE] = _rows(prev_ref, r, dil) if blk == 0 else _rows(cur_ref, (blk - 1) * pb + r, dil)
            dst[e, LANE:] = _rows(cur_ref, blk * pb + r, dil)


def _stage(dst, ref, dil, pb, nblk, lead=None):
    for blk in range(nblk):
        for r in range(dil):
            val = _rows(ref, blk * pb + r, dil)
            if lead is None:
                dst[blk * dil + r] = val
            else:
                dst[lead, blk * dil + r] = val


def _unstage(ref, src, dil, pb, nblk):
    for blk in range(nblk):
        for r in range(dil):
            _rows_store(ref, blk * pb + r, dil, src[blk * dil + r])


def _attn_fwd(q, k, kv, g, dil, carried=None):
    S = q.shape[0]
    H = q.shape[1] // (3 * LANE)
    pb, nblk, nw, col, win, prev, _ = _attn_geom(S, H, g, dil)
    vcol = 3 * H + col

    def body(q_ref, kc_ref, kp_ref, vc_ref, vp_ref, o_ref, l_ref, qs, ks, vs, os_, ls):
        h, w = pl.program_id(0), pl.program_id(1)
        _stage(qs, q_ref, dil, pb, nblk)
        _stage_band(ks, kc_ref, kp_ref, dil, pb, nblk)
        _stage_band(vs, vc_ref, vp_ref, dil, pb, nblk)

        def run(lo, hi, bias):
            def step(e, carry):
                s = _dot(qs[e].astype(BF), ks[e].astype(BF), NT) + bias
                mx = jnp.max(s, axis=-1, keepdims=True)
                p = jnp.exp(s - mx)
                l = jnp.sum(p, axis=-1, keepdims=True)
                os_[e] = _dot((p / l).astype(BF), vs[e].astype(BF))
                ls[e] = jnp.broadcast_to(mx + jnp.log(l), (LANE, LANE))
                return carry
            if hi > lo:
                lax.fori_loop(lo, hi, step, 0, unroll=True)

        run(0, dil, _band_bias(h, H, dil, w > 0))
        run(dil, N_ENT, _band_bias(h, H, dil, True))
        _unstage(o_ref, os_, dil, pb, nblk)
        _unstage(l_ref, ls, dil, pb, nblk)

    out = jax.ShapeDtypeStruct((S, H * LANE), F32)
    sq = pltpu.VMEM((N_ENT, LANE, LANE), F32)
    sk = pltpu.VMEM((N_ENT, 2 * LANE, LANE), F32)
    return _pcall(
        body, name="attn_fwd_d%d" % dil, grid=(H, nw),
        in_specs=[win(col), win(col), prev(col), win(vcol), prev(vcol)],
        out_specs=[win(0), win(0)], out_shape=[out, out], scratch=[sq, sk, sk, sq, sq],
        sem=("parallel", "parallel"), args=(q, k, k, kv, kv), carried=carried)


def _attn_combine(os_, ls_, carried=None):
    S, C = os_[0].shape
    tm = _tile(S, 256, 16)

    def body(o0, o1, o2, l0, l1, l2, o_ref, lse_ref):
        a, b, c = l0[...], l1[...], l2[...]
        mx = jnp.maximum(jnp.maximum(a, b), c)
        ea, eb, ec = jnp.exp(a - mx), jnp.exp(b - mx), jnp.exp(c - mx)
        den = ea + eb + ec
        o_ref[...] = ((ea * o0[...] + eb * o1[...] + ec * o2[...]) / den).astype(BF)
        lse_ref[...] = mx + jnp.log(den)

    blk = pl.BlockSpec((tm, C), lambda m: (m, 0))
    return _pcall(
        body, name="attn_combine", grid=(S // tm,), in_specs=[blk] * 6, out_specs=[blk, blk],
        out_shape=[jax.ShapeDtypeStruct((S, C), BF), jax.ShapeDtypeStruct((S, C), F32)],
        sem=("parallel",), args=(*os_, *ls_), carried=carried)


def _attn_delta(do, o):
    S, C = do.shape
    tm = _tile(S, 512, 16)

    def body(do_ref, o_ref, d_ref):
        for c in range(C // LANE):
            sl = slice(c * LANE, (c + 1) * LANE)
            prod = do_ref[:, sl].astype(BF).astype(F32) * o_ref[:, sl].astype(F32)
            d_ref[:, sl] = jnp.broadcast_to(jnp.sum(prod, axis=-1, keepdims=True), (tm, LANE))

    blk = pl.BlockSpec((tm, C), lambda m: (m, 0))
    outs, _ = _pcall(
        body, name="attn_delta", grid=(S // tm,), in_specs=[blk, blk], out_specs=[blk],
        out_shape=[jax.ShapeDtypeStruct((S, C), F32)], sem=("parallel",), args=(do, o))
    return outs[0]


def _attn_bwd_dq(q, k, kv, do, lse, dl, g, dil, dq_prev):
    S = q.shape[0]
    H = q.shape[1] // (3 * LANE)
    pb, nblk, nw, col, win, prev, _ = _attn_geom(S, H, g, dil)
    vcol = 3 * H + col

    def body(q_ref, kc_ref, kp_ref, vc_ref, vp_ref, do_ref, l_ref, d_ref, *rest):
        dq_ref, qs, ks, vs, dos, ls, ds_, dqs = rest[-8:]
        h, w = pl.program_id(0), pl.program_id(1)
        _stage(qs, q_ref, dil, pb, nblk)
        _stage_band(ks, kc_ref, kp_ref, dil, pb, nblk)
        _stage_band(vs, vc_ref, vp_ref, dil, pb, nblk)
        _stage(dos, do_ref, dil, pb, nblk)
        _stage(ls, l_ref, dil, pb, nblk)
        _stage(ds_, d_ref, dil, pb, nblk)

        def run(lo, hi, bias):
            def step(e, carry):
                kb = ks[e].astype(BF)
                s = _dot(qs[e].astype(BF), kb, NT) + bias
                p = jnp.exp(s - ls[e][:, :1])
                dp = _dot(dos[e].astype(BF), vs[e].astype(BF), NT)
                dsc = p * (dp - ds_[e][:, :1])
                dqs[e] = _dot(dsc.astype(BF), kb)
                return carry
            if hi > lo:
                lax.fori_loop(lo, hi, step, 0, unroll=True)

        run(0, dil, _band_bias(h, H, dil, w > 0))
        run(dil, N_ENT, _band_bias(h, H, dil, True))
        _unstage(dq_ref, dqs, dil, pb, nblk)

    sq = pltpu.VMEM((N_ENT, LANE, LANE), F32)
    sk = pltpu.VMEM((N_ENT, 2 * LANE, LANE), F32)
    in_specs = [win(col), win(col), prev(col), win(vcol), prev(vcol), win(0), win(0), win(0)]
    args = [q, k, k, kv, kv, do, lse, dl]
    aliases = {}
    if dq_prev is not None:
        in_specs.append(ANY)
        args.append(dq_prev)
        aliases = {8: 0}
    outs, _ = _pcall(
        body, name="attn_bwd_dq_d%d" % dil, grid=(H, nw), in_specs=in_specs,
        out_specs=[win(col)], out_shape=[jax.ShapeDtypeStruct((S, 3 * H * LANE), F32)],
        scratch=[sq, sk, sk, sq, sq, sq, sq], sem=("parallel", "parallel"), args=args, aliases=aliases)
    return outs[0]


def _attn_bwd_dkv(q, k, kv, do, lse, dl, g, dil, prev_out, accumulate):
    S = q.shape[0]
    H = q.shape[1] // (3 * LANE)
    pb, nblk, nw, col, win, _, nxt = _attn_geom(S, H, g, dil)
    vcol = 3 * H + col
    n_q = N_ENT + dil

    def body(k_ref, v_ref, qc_ref, qn_ref, doc_ref, don_ref, lc_ref, ln_ref, dc_ref, dn_ref, *rest):
        dk_ref, dv_ref, ks, vs, qs, dos, ls, ds_, dks, dvs = rest[-10:]
        h, w = pl.program_id(0), pl.program_id(1)
        _stage(ks, k_ref, dil, pb, nblk)
        _stage(vs, v_ref, dil, pb, nblk)
        for dst, cur, nx in ((qs, qc_ref, qn_ref), (dos, doc_ref, don_ref), (ls, lc_ref, ln_ref), (ds_, dc_ref, dn_ref)):
            _stage(dst, cur, dil, pb, nblk)
            for r in range(dil):
                dst[N_ENT + r] = _rows(nx, r, dil)
        qi = lax.broadcasted_iota(jnp.int32, (LANE, LANE), 0)
        kj = lax.broadcasted_iota(jnp.int32, (LANE, LANE), 1)
        sd = _slope_times_dil(h, H, dil, (LANE, LANE))
        bias_c = jnp.where(qi >= kj, -sd * (qi - kj).astype(F32), NEG)

        def run(lo, hi, has_next):
            bias_n = jnp.where((qi <= kj) & has_next, -sd * (qi + LANE - kj).astype(F32), NEG)

            def step(e, carry):
                kb = ks[e].astype(BF)
                vb = vs[e].astype(BF)
                dk = jnp.zeros((LANE, LANE), F32)
                dv = jnp.zeros((LANE, LANE), F32)
                for eq, bias in ((e, bias_c), (e + dil, bias_n)):
                    qb = qs[eq].astype(BF)
                    dob = dos[eq].astype(BF)
                    s = _dot(qb, kb, NT) + bias
                    p = jnp.exp(s - ls[eq][:, :1])
                    dp = _dot(dob, vb, NT)
                    dsc = p * (dp - ds_[eq][:, :1])
                    dv = dv + _dot(p.astype(BF), dob, TN)
                    dk = dk + _dot(dsc.astype(BF), qb, TN)
                dks[e] = dk
                dvs[e] = dv
                return carry
            if hi > lo:
                lax.fori_loop(lo, hi, step, 0, unroll=True)

        run(0, N_ENT - dil, True)
        run(N_ENT - dil, N_ENT, w < nw - 1)
        if accumulate:
            pk_ref, pv_ref = rest[0], rest[1]
            dk_ref[...] = pk_ref[...]
            dv_ref[...] = pv_ref[...]
            for blk in range(nblk):
                for r in range(dil):
                    e, start = blk * dil + r, blk * pb + r
                    _rows_store(dk_ref, start, dil, _rows(dk_ref, start, dil) + dks[e])
                    _rows_store(dv_ref, start, dil, _rows(dv_ref, start, dil) + dvs[e])
        else:
            _unstage(dk_ref, dks, dil, pb, nblk)
            _unstage(dv_ref, dvs, dil, pb, nblk)

    s1 = pltpu.VMEM((N_ENT, LANE, LANE), F32)
    s2 = pltpu.VMEM((n_q, LANE, LANE), F32)
    out = jax.ShapeDtypeStruct((S, 3 * H * LANE), F32)
    in_specs = [win(col), win(vcol), win(col), nxt(col), win(0), nxt(0), win(0), nxt(0), win(0), nxt(0)]
    args = [k, kv, q, q, do, do, lse, lse, dl, dl]
    aliases = {}
    if prev_out is not None:
        in_specs += [win(col), win(col)] if accumulate else [ANY, ANY]
        args += list(prev_out)
        aliases = {10: 0, 11: 1}
    outs, _ = _pcall(
        body, name="attn_bwd_dkv_d%d%s" % (dil, "_acc" if accumulate else ""), grid=(H, nw), in_specs=in_specs,
        out_specs=[win(col), win(col)], out_shape=[out, out],
        scratch=[s1, s1, s2, s2, s2, s2, s1, s1], sem=("parallel", "parallel"), vmem=56, args=args, aliases=aliases)
    return outs


def _loss_head(y, target):
    S, D = y.shape
    tm = _tile(S, 512, 8)

    def body(y_ref, t_ref, dy_ref, l_ref):
        e = y_ref[...] - t_ref[...]
        dy_ref[...] = e * (1.0 / D)
        part = jnp.broadcast_to(jnp.sum(jnp.sum(e * e, axis=1, keepdims=True), axis=0, keepdims=True) * (0.5 / D), (8, LANE))
        _accum_rows(l_ref, part, pl.program_id(0) == 0)

    blk = pl.BlockSpec((tm, D), lambda m: (m, 0))
    outs, _ = _pcall(
        body, name="loss_head", grid=(S // tm,), in_specs=[blk, blk],
        out_specs=[blk, pl.BlockSpec((8, LANE), lambda m: (0, 0))],
        out_shape=[jax.ShapeDtypeStruct((S, D), F32), jax.ShapeDtypeStruct((8, LANE), F32)],
        sem=("arbitrary",), args=(y, target))
    return outs


class _Queue:
    def __init__(self):
        self.items = []

    def push(self, kind, key, arr):
        self.items.append((kind, key, arr))

    def take(self, budget_us, kinds=None):
        taken, left, spent = [], [], 0.0
        for item in self.items:
            if spent >= budget_us or (kinds is not None and item[0] not in kinds):
                left.append(item)
                continue
            taken.append(item)
            spent += COST_US_PER_ELEM[item[0]] * (item[2].size / item[2].shape[0] if item[0] != "gather" else item[2].size)
        self.items = left
        return taken

    def take_keys(self, keys):
        taken = [it for it in self.items if it[1] in keys]
        self.items = [it for it in self.items if it[1] not in keys]
        return taken

    def take_kind(self, kind):
        taken = [it for it in self.items if it[0] == kind]
        self.items = [it for it in self.items if it[0] != kind]
        return taken


def kernel(x, ffn1_norm, ffn1_w_gate, ffn1_w_up, ffn1_w_down, mix_norm, ffn2_norm, ffn2_w_gate, ffn2_w_up, ffn2_w_down, gmlp_w_in, gmlp_v_norm, gmlp_w_s, gmlp_b_s, gmlp_w_out, kv_norm, w_kv, k_norm, attn_w_q, attn_q_norm, attn_w_o, loss_target, m_ffn1_norm, m_ffn1_w_gate, m_ffn1_w_up, m_ffn1_w_down, m_mix_norm, m_ffn2_norm, m_ffn2_w_gate, m_ffn2_w_up, m_ffn2_w_down, m_gmlp_w_in, m_gmlp_v_norm, m_gmlp_w_s, m_gmlp_b_s, m_gmlp_w_out, m_kv_norm, m_w_kv, m_k_norm, m_attn_w_q, m_attn_q_norm, m_attn_w_o, v_ffn1_norm, v_ffn1_w_gate, v_ffn1_w_up, v_ffn1_w_down, v_mix_norm, v_ffn2_norm, v_ffn2_w_gate, v_ffn2_w_up, v_ffn2_w_down, v_gmlp_w_in, v_gmlp_v_norm, v_gmlp_w_s, v_gmlp_b_s, v_gmlp_w_out, v_kv_norm, v_w_kv, v_k_norm, v_attn_w_q, v_attn_q_norm, v_attn_w_o):
    names = ["ffn1_norm", "ffn1_w_gate", "ffn1_w_up", "ffn1_w_down", "mix_norm", "ffn2_norm", "ffn2_w_gate",
             "ffn2_w_up", "ffn2_w_down", "gmlp_w_in", "gmlp_v_norm", "gmlp_w_s", "gmlp_b_s", "gmlp_w_out",
             "kv_norm", "w_kv", "k_norm", "attn_w_q", "attn_q_norm", "attn_w_o"]
    W = dict(zip(names, [ffn1_norm, ffn1_w_gate, ffn1_w_up, ffn1_w_down, mix_norm, ffn2_norm, ffn2_w_gate,
                         ffn2_w_up, ffn2_w_down, gmlp_w_in, gmlp_v_norm, gmlp_w_s, gmlp_b_s, gmlp_w_out,
                         kv_norm, w_kv, k_norm, attn_w_q, attn_q_norm, attn_w_o]))
    M = dict(zip(names, [m_ffn1_norm, m_ffn1_w_gate, m_ffn1_w_up, m_ffn1_w_down, m_mix_norm, m_ffn2_norm, m_ffn2_w_gate,
                         m_ffn2_w_up, m_ffn2_w_down, m_gmlp_w_in, m_gmlp_v_norm, m_gmlp_w_s, m_gmlp_b_s, m_gmlp_w_out,
                         m_kv_norm, m_w_kv, m_k_norm, m_attn_w_q, m_attn_q_norm, m_attn_w_o]))
    V = dict(zip(names, [v_ffn1_norm, v_ffn1_w_gate, v_ffn1_w_up, v_ffn1_w_down, v_mix_norm, v_ffn2_norm, v_ffn2_w_gate,
                         v_ffn2_w_up, v_ffn2_w_down, v_gmlp_w_in, v_gmlp_v_norm, v_gmlp_w_s, v_gmlp_b_s, v_gmlp_w_out,
                         v_kv_norm, v_w_kv, v_k_norm, v_attn_w_q, v_attn_q_norm, v_attn_w_o]))

    depth = ffn1_norm.shape[0]
    n_a = gmlp_w_in.shape[0]
    S, D = x.shape[1], x.shape[2]
    H = D // LANE
    n_grp = len(DILATIONS)
    hw = H * LANE
    xi, yi, ci = _me()
    core = jnp.reshape(ci, (1,)).astype(jnp.int32)
    chip = jnp.reshape(2 * xi + yi, (1,)).astype(jnp.int32)
    dev = 4 * xi + 2 * yi + ci
    q_scale = LANE ** -0.5
    ffn_names = (("f1", ("ffn1_norm", "ffn1_w_gate", "ffn1_w_up", "ffn1_w_down")),
                 ("f2", ("ffn2_norm", "ffn2_w_gate", "ffn2_w_up", "ffn2_w_down")))
    transposed = ("ffn1_w_gate", "ffn1_w_up", "ffn2_w_gate", "ffn2_w_up")

    def gain(v):
        return v.reshape(1, -1)

    def head_gain(g3):
        return jnp.tile(g3[:, None, :], (1, H, 1)).reshape(1, n_grp * hw)

    def bf_t(w):
        return jnp.swapaxes(w, 0, 1).astype(BF)

    gq = _Queue()
    gathered = {}
    gq.push("gather", "v_norm", jnp.pad(gmlp_v_norm, ((0, 8 - n_a), (0, 0))))
    for l in range(depth):
        for tag, (_, wgn, wun, wdn) in ffn_names:
            if tag == "f2":
                if l < n_a:
                    gq.push("gather", ("gmlp_w_in", l), bf_t(gmlp_w_in[l]))
                    gq.push("gather", ("gmlp_w_out", l), gmlp_w_out[l].astype(BF))
                else:
                    gq.push("gather", ("attn_w_q", l - n_a), bf_t(attn_w_q[l - n_a]))
                    gq.push("gather", ("attn_w_o", l - n_a), attn_w_o[l - n_a].astype(BF))
            gq.push("gather", (wgn, l), bf_t(W[wgn][l]))
            gq.push("gather", (wun, l), bf_t(W[wun][l]))
            gq.push("gather", (wdn, l), W[wdn][l].astype(BF))
        if l == n_a - 1:
            gq.push("gather", "w_kv", bf_t(w_kv))

    def land(items, outs):
        for (_, key, _), o in zip(items, outs):
            gathered[key] = o

    def need(*keys):
        items = gq.take_keys([k for k in keys if k not in gathered])
        if items:
            land(items, _comm_only(_Carried([(k, a) for k, _, a in items]), "allgather"))
        return [gathered[k] for k in keys]

    def carry(q, kind):
        items = q.take(BUDGET_US[kind])
        return items, _Carried([(k, a) for k, _, a in items])

    v_all = need("v_norm", *[(n, 0) for n in ffn_names[0][1][1:]])[0]
    v_gain_all = jnp.transpose(v_all[:, :n_a], (1, 0, 2)).reshape(n_a, -1)

    cur = x.reshape(S, D)
    saved = []
    k_sh = kv_raw = kv_hn = kv_x = k_gain = None

    for l in range(depth):
        rec = {}
        for tag, (nn, wgn, wun, wdn) in ffn_names:
            if tag == "f2":
                rec["mix_x"] = cur
                if l < n_a:
                    w_in, w_out = need(("gmlp_w_in", l), ("gmlp_w_out", l))
                    bias = jnp.repeat(gmlp_b_s[l].T, LANE, axis=1)
                    items, car = carry(gq, "normproj")
                    (hm, z), couts = _normproj(cur, gain(mix_norm[l]), w_in, "gmlp_in", car)
                    land(items, couts)
                    items, car = carry(gq, "gmlp_fwd")
                    t, couts = _gmlp_fwd(z, gain(v_gain_all[l]), gmlp_w_s[l], bias, car)
                    land(items, couts)
                    items, car = carry(gq, "rowproj")
                    cur, couts = _rowproj(cur, t, w_out, "proj_out", car)
                    land(items, couts)
                    rec.update(w_in=w_in, w_out=w_out, bias=bias, hm=hm, z=z, t=t)
                else:
                    jj = l - n_a
                    w_q, w_o = need(("attn_w_q", jj), ("attn_w_o", jj))
                    items, car = carry(gq, "normproj")
                    (hm, q_raw), couts = _normproj(cur, gain(mix_norm[l]), w_q, "attn_q", car)
                    land(items, couts)
                    qg = head_gain(attn_q_norm[jj])
                    items, car = carry(gq, "headnorm")
                    q, couts = _headnorm_fwd(q_raw, n_grp * hw, qg, q_scale, "headnorm_q", car)
                    land(items, couts)
                    outs = []
                    for g, dil in enumerate(DILATIONS):
                        items, car = carry(gq, "attn_fwd")
                        og, couts = _attn_fwd(q, k_sh, kv_raw, g, dil, car)
                        land(items, couts)
                        outs.append(og)
                    items, car = carry(gq, "attn_combine")
                    (o, lse), couts = _attn_combine([o_ for o_, _ in outs], [l_ for _, l_ in outs], car)
                    land(items, couts)
                    items, car = carry(gq, "rowproj")
                    cur, couts = _rowproj(cur, o, w_o, "proj_out", car)
                    land(items, couts)
                    rec.update(w_q=w_q, w_o=w_o, hm=hm, q_raw=q_raw, qg=qg, q=q, o=o, lse=lse)
            wg, wu, wd = need((wgn, l), (wun, l), (wdn, l))
            rec[tag + "_x"] = cur
            items, car = carry(gq, "ffn_fwd")
            (cur, hn, act, ga, gb), couts = _ffn_fwd(cur, gain(W[nn][l]), wg, wu, wd, car)
            land(items, couts)
            rec[tag] = (wg, wu, wd, hn, act, ga, gb)
        if l == n_a - 1:
            (w_kv_g,) = need("w_kv")
            kv_x = cur
            items, car = carry(gq, "normproj")
            (kv_hn, kv_raw), couts = _normproj(cur, gain(kv_norm), w_kv_g, "kv_proj", car)
            land(items, couts)
            k_gain = head_gain(k_norm)
            items, car = carry(gq, "headnorm")
            k_sh, couts = _headnorm_fwd(kv_raw, n_grp * hw, k_gain, 1.0, "headnorm_k", car)
            land(items, couts)
        saved.append(rec)

    dcur, loss_part = _loss_head(cur, loss_target.reshape(S, D))
    loss = lax.psum(loss_part[0, 0], ("x", "y", "c"))

    rq = _Queue()
    reduced = {}
    small = {}

    def rs_land(items, outs):
        for (kind, key, arr), o in zip(items, outs):
            if kind == "core":
                rq.push("chip", key, _pair_add(arr, o, core))
            else:
                reduced[key] = (arr, o)

    def rs_carry(kind, kinds=None):
        items = rq.take(BUDGET_US[kind], kinds)
        return items, _Carried([(k, a) for k, _, a in items])

    def wgrad(key, a, a_kind, b, b_kind, name):
        last_block = key[1] == 0 and key[0].startswith("ffn1")
        items, car = rs_carry("wgrad_last" if last_block else "wgrad", None if last_block else ("core",))
        p, couts = _wgrad(a, a_kind, b, b_kind, name, car)
        rs_land(items, couts)
        rq.push("core", key, p)

    def put(name, l, val, n_layers):
        small.setdefault(name, [None] * n_layers)[l] = val

    dk_dv = None
    for l in reversed(range(depth)):
        rec = saved[l]
        if l == n_a - 1:
            dkv, dkg = _headnorm_bwd(dk_dv[0], kv_raw, k_gain, 1.0, "headnorm_k_bwd", tail=dk_dv[1])
            wgrad(("w_kv", 0), kv_hn, "full", dkv, "cols", "wgrad_kv")
            items, car = rs_carry("dgrad_col_norm")
            (dcur, dg), couts = _dgrad_col_norm(dcur, kv_x, gain(kv_norm), dkv, gathered["w_kv"], "dgrad_kv", car)
            rs_land(items, couts)
            small["kv_norm"] = [dg[0]]
            small["k_norm"] = [dkg[0].reshape(n_grp, H, LANE).sum(axis=1)]
        for tag, (nn, wgn, wun, wdn) in reversed(ffn_names):
            wg, wu, wd, hn, act, ga, gb = rec[tag]
            items, car = rs_carry("ffn_bwd_dx")
            (dcur, da, db, dyh, dg), couts = _ffn_bwd_dx(dcur, rec[tag + "_x"], gain(W[nn][l]), ga, gb, wg, wu, wd, car)
            rs_land(items, couts)
            wgrad((wgn, l), da, "stack", hn, "full", "wgrad_ffn_in")
            wgrad((wun, l), db, "stack", hn, "full", "wgrad_ffn_in")
            wgrad((wdn, l), act, "stack", dyh, "full", "wgrad_ffn_out")
            put(nn, l, dg[0], depth)
            if tag == "f2":
                mix_x = rec["mix_x"]
                if l < n_a:
                    dt = _dgrad_row(dcur, rec["w_out"], BF, "dgrad_gmlp_out")
                    wgrad(("gmlp_w_out", l), rec["t"], "cols", dcur, "full", "wgrad_proj_out")
                    dz, dws, dbias, dvg = _gmlp_bwd(rec["z"], dt, gain(v_gain_all[l]), gmlp_w_s[l], rec["bias"])
                    wgrad(("gmlp_w_in", l), rec["hm"], "full", dz, "cols", "wgrad_gmlp_in")
                    items, car = rs_carry("dgrad_col_norm")
                    (dcur, dg), couts = _dgrad_col_norm(dcur, mix_x, gain(mix_norm[l]), dz, rec["w_in"], "dgrad_gmlp_in", car)
                    rs_land(items, couts)
                    put("gmlp_w_s", l, dws, n_a)
                    put("gmlp_b_s", l, dbias[:, ::LANE].T, n_a)
                    put("gmlp_v_norm", l, dvg[0], n_a)
                else:
                    jj = l - n_a
                    do = _dgrad_row(dcur, rec["w_o"], F32, "dgrad_attn_out")
                    wgrad(("attn_w_o", jj), rec["o"], "cols", dcur, "full", "wgrad_proj_out")
                    dl = _attn_delta(do, rec["o"])
                    dq = None
                    first_layer = dk_dv is None
                    for g, dil in enumerate(DILATIONS):
                        dq = _attn_bwd_dq(rec["q"], k_sh, kv_raw, do, rec["lse"], dl, g, dil, dq)
                        dk_dv = _attn_bwd_dkv(rec["q"], k_sh, kv_raw, do, rec["lse"], dl, g, dil, dk_dv,
                                              accumulate=not first_layer)
                    dq_raw, dqg = _headnorm_bwd(dq, rec["q_raw"], rec["qg"], q_scale, "headnorm_q_bwd")
                    wgrad(("attn_w_q", jj), rec["hm"], "full", dq_raw, "cols", "wgrad_attn_q")
                    items, car = rs_carry("dgrad_col_norm")
                    (dcur, dg), couts = _dgrad_col_norm(dcur, mix_x, gain(mix_norm[l]), dq_raw, rec["w_q"], "dgrad_attn_q", car)
                    rs_land(items, couts)
                    put("attn_q_norm", jj, dqg[0].reshape(n_grp, H, LANE).sum(axis=1), depth - n_a)
                put("mix_norm", l, dg[0], depth)
    grad_x = dcur.reshape(1, S, D)

    for kind in ("core", "chip"):
        items = rq.take_kind(kind)
        if items:
            rs_land(items, _comm_only(_Carried([(k, a) for k, _, a in items]), "rs_%s_exchange" % kind))

    out_g, out_d, out_m, out_v = {}, {}, {}, {}
    for name in names:
        if (name, 0) not in reduced:
            continue
        if name in transposed:
            as3 = lambda t: jnp.swapaxes(t, 1, 2)
        else:
            as3 = (lambda t: t[None]) if W[name].ndim == 2 else (lambda t: t)
        res = None
        for l in range(as3(W[name]).shape[0]):
            qsum, r2 = reduced[(name, l)]
            res = _adamw_shard(qsum, r2, chip, as3(W[name]), as3(M[name]), as3(V[name]), l, res)
        for dct, val in zip((out_g, out_d, out_m, out_v), res):
            dct[name] = jnp.swapaxes(val, 1, 2) if name in transposed else (val[0] if W[name].ndim == 2 else val)

    small_names = [n for n in names if n in small]
    full_shape = {n: (W[n].shape if n != "gmlp_v_norm" else (n_a, v_gain_all.shape[1])) for n in small_names}
    flat = jnp.concatenate([jnp.stack(small[n]).reshape(-1) if W[n].ndim > 1 else small[n][0].reshape(-1)
                            for n in small_names])
    n_flat = flat.shape[0]
    rows = -(-n_flat // (8 * LANE)) * 8

    def pack(parts_list):
        v = jnp.concatenate([p.reshape(-1) for p in parts_list])
        return jnp.pad(v, (0, rows * LANE - n_flat)).reshape(rows, LANE)

    def full_of(dct, n, fill):
        if n != "gmlp_v_norm":
            return dct[n]
        sh = dct[n].shape[1]
        return lax.dynamic_update_slice(jnp.full(full_shape[n], fill, F32), dct[n], (0, dev * sh))

    (g_all,) = _comm_only(_Carried([("gather", pack([flat]))]), "allgather_small_grads")
    w_p = pack([full_of(W, n, 0.0) for n in small_names])
    m_p = pack([full_of(M, n, 0.0) for n in small_names])
    v_p = pack([full_of(V, n, 1.0) for n in small_names])
    packed = _adamw_replicated(g_all, w_p, m_p, v_p)
    offs = 0
    for n in small_names:
        size = math.prod(full_shape[n])
        for dct, arr in zip((out_g, out_d, out_m, out_v), packed):
            val = arr.reshape(-1)[offs:offs + size].reshape(full_shape[n])
            if n == "gmlp_v_norm":
                sh = W[n].shape[1]
                val = lax.dynamic_slice(val, (0, dev * sh), (n_a, sh))
            dct[n] = val
        offs += size

    return (loss, grad_x, *[out_g[n] for n in names], *[out_d[n] for n in names],
            *[out_m[n] for n in names], *[out_v[n] for n in names])
```
